```python
import math
import jax
import jax.numpy as jnp
from jax import lax
import numpy as np

D_MODEL = 1024
BATCH = 8
SEQ = 2048
DEPTH = 2
DEC_BATCH = 128
DEC_SEQ = 1
PAST_LEN = 16384
PAGE_SIZE = 128

N_EVEN = (DEPTH + 1) // 2
N_ODD = DEPTH // 2
EXPAND = 2
W_MIX = EXPAND * D_MODEL
H_A = 8
DK_A = 128
DV_A = 128
CONV_W = 4
CONV_CH_A = H_A * (2 * DK_A + DV_A)
CHUNK_A = 64
H_B = 8
DK_B = 64
DV_B = 128
CHUNK_B = 64
H_C = 4
DK_C = 128
DV_C = 256
GLA_RANK = 16
GLA_TAU = 16.0
CHUNK_C = 16
H_D = 4
DK_D = 128
DV_D = 256
CHUNK_D = 64
ROPE_BASE = 10000.0
EPS = 1e-6

EVEN_SPLIT = [CONV_CH_A, H_A, H_A, H_B * DK_B, H_B * DK_B, H_B * DV_B, 2 * H_B, H_B * DV_B, W_MIX]
ODD_SPLIT = [H_C * DK_C, H_C * DK_C, H_C * DV_C, GLA_RANK, H_D * DK_D, H_D * DK_D, H_D * DV_D, W_MIX]
P_EVEN = sum(EVEN_SPLIT)
P_ODD = sum(ODD_SPLIT)

kernel_name = 'hybrid_gdn_mlstm_gla_retnet_step'


def _split(x, sizes):
    idx = [int(i) for i in np.cumsum(sizes)[:-1]]
    return jnp.split(x, idx, axis=-1)


def _rms(x, w):
    xf = x.astype(jnp.float32)
    y = xf * lax.rsqrt(jnp.mean(xf * xf, axis=-1, keepdims=True) + EPS)
    return (y * w.astype(jnp.float32)).astype(x.dtype)


def _l2n(x):
    return x * lax.rsqrt(jnp.sum(x * x, axis=-1, keepdims=True) + EPS)


def _to_chunks(x, L):
    B, T = x.shape[:2]
    return jnp.moveaxis(x.reshape((B, T // L, L) + x.shape[2:]), 1, 0)


def _from_chunks(y):
    y = jnp.moveaxis(y, 0, 1)
    return y.reshape((y.shape[0], y.shape[1] * y.shape[2]) + y.shape[3:])


def _causal_conv(u, buf, w):
    T = u.shape[1]
    full = jnp.concatenate([buf, u], axis=1)
    y = full[:, 0:T] * w[0]
    for j in range(1, CONV_W):
        y = y + full[:, j:j + T] * w[j]
    return y, full[:, T:]


def _rotary(x, pos):
    half = x.shape[-1] // 2
    inv = ROPE_BASE ** (-jnp.arange(half, dtype=jnp.float32) / half)
    ang = pos.astype(jnp.float32)[:, None] * inv[None, :]
    cos = jnp.cos(ang)[:, None, :]
    sin = jnp.sin(ang)[:, None, :]
    x1, x2 = x[..., :half], x[..., half:]
    return jnp.concatenate([x1 * cos - x2 * sin, x1 * sin + x2 * cos], axis=-1)


def _gated_delta(q, k, v, beta, g, S0):
    L = math.gcd(q.shape[1], CHUNK_A)
    tri_incl = jnp.tril(jnp.ones((L, L), bool))
    tri_strict = jnp.tril(jnp.ones((L, L), bool), -1)
    eye = jnp.eye(L, dtype=jnp.float32)

    def step(S, inp):
        qc, kc, vc, bc, gc = inp
        bh = jnp.moveaxis(jnp.cumsum(gc, axis=1), 1, 2)
        decay = jnp.exp(jnp.where(tri_incl, bh[..., :, None] - bh[..., None, :], -jnp.inf))
        beta_h = jnp.moveaxis(bc, 1, 2)
        kk = jnp.einsum('bthd,bshd->bhts', kc, kc)
        A = jnp.where(tri_strict, decay * kk, 0.0) * beta_h[..., :, None]
        gam = jnp.exp(bh)[..., None]
        rhs = beta_h[..., None] * (jnp.moveaxis(vc, 1, 2) - gam * jnp.einsum('bthd,bhde->bhte', kc, S))
        U = lax.linalg.triangular_solve(A + eye, rhs, left_side=True, lower=True, unit_diagonal=True)
        qk = jnp.einsum('bthd,bshd->bhts', qc, kc) * decay
        o = gam * jnp.einsum('bthd,bhde->bhte', qc, S) + jnp.einsum('bhts,bhse->bhte', qk, U)
        last = bh[..., -1:]
        S_new = jnp.exp(last)[..., None] * S + jnp.einsum('bshd,bhs,bhse->bhde', kc, jnp.exp(last - bh), U)
        return S_new, jnp.moveaxis(o, 1, 2)

    S, o = lax.scan(step, S0, tuple(_to_chunks(a, L) for a in (q, k, v, beta, g)))
    return _from_chunks(o), S


def _mlstm(q, k, v, ig, fg, C0, n0, m0):
    L = math.gcd(q.shape[1], CHUNK_B)
    tri = jnp.tril(jnp.ones((L, L), bool))

    def step(carry, inp):
        C, n, m = carry
        qc, kc, vc, ic, fc = inp
        b = jnp.moveaxis(jnp.cumsum(jax.nn.log_sigmoid(fc), axis=1), 1, 2)
        ih = jnp.moveaxis(ic, 1, 2)
        D = jnp.where(tri, b[..., :, None] - b[..., None, :] + ih[..., None, :], -jnp.inf)
        w0 = b + m[..., None]
        m_t = jnp.maximum(w0, jnp.max(D, axis=-1))
        P = jnp.exp(D - m_t[..., None]) * jnp.einsum('bthd,bshd->bhts', qc, kc)
        s0 = jnp.exp(w0 - m_t)
        num = s0[..., None] * jnp.einsum('bthd,bhde->bhte', qc, C) + jnp.einsum('bhts,bshe->bhte', P, vc)
        den = s0 * jnp.einsum('bthd,bhd->bht', qc, n) + jnp.sum(P, axis=-1)
        h = num / jnp.maximum(jnp.abs(den), jnp.exp(-m_t))[..., None]
        m_end = m_t[..., -1]
        we = jnp.exp(b[..., -1:] - b + ih - m_end[..., None])
        se = jnp.exp(b[..., -1] + m - m_end)
        C_new = se[..., None, None] * C + jnp.einsum('bshd,bhs,bshe->bhde', kc, we, vc)
        n_new = se[..., None] * n + jnp.einsum('bshd,bhs->bhd', kc, we)
        return (C_new, n_new, m_end), jnp.moveaxis(h, 1, 2)

    (C, n, m), h = lax.scan(step, (C0, n0, m0), tuple(_to_chunks(a, L) for a in (q, k, v, ig, fg)))
    return _from_chunks(h), C, n, m


def _gla(q, k, v, g, S0):
    L = math.gcd(q.shape[1], CHUNK_C)
    tri = jnp.tril(jnp.ones((L, L), bool))[None, :, :, None, None]

    def step(S, inp):
        qc, kc, vc, gc = inp
        b = jnp.cumsum(gc, axis=1)
        dec = jnp.exp(jnp.where(tri, b[:, :, None] - b[:, None, :], -jnp.inf))
        att = jnp.sum(qc[:, :, None] * kc[:, None] * dec, axis=-1)
        o = jnp.einsum('bthd,bhde->bthe', qc * jnp.exp(b), S) + jnp.einsum('btsh,bshe->bthe', att, vc)
        bl = b[:, -1]
        S_new = jnp.exp(bl)[..., None] * S + jnp.einsum('bshd,bshe->bhde', kc * jnp.exp(bl[:, None] - b), vc)
        return S_new, o

    S, o = lax.scan(step, S0, tuple(_to_chunks(a, L) for a in (q, k, v, g)))
    return _from_chunks(o), S


def _retention(q, k, v, S0):
    L = math.gcd(q.shape[1], CHUNK_D)
    lg = jnp.log(1.0 - 2.0 ** (-5.0 - jnp.arange(H_D, dtype=jnp.float32)))
    idx = jnp.arange(L, dtype=jnp.float32)
    rel = idx[:, None] - idx[None, :]
    Dm = jnp.where(rel >= 0, jnp.exp(lg[:, None, None] * jnp.maximum(rel, 0.0)), 0.0)
    inner = jnp.exp(lg[None, :] * (idx[:, None] + 1.0))
    end = jnp.exp(lg[None, :] * (L - 1.0 - idx[:, None]))
    gL = jnp.exp(lg * L)

    def step(S, inp):
        qc, kc, vc = inp
        att = jnp.einsum('bthd,bshd->bhts', qc, kc) * Dm
        o = jnp.einsum('bthd,bhde->bthe', qc, S) * inner[None, :, :, None] + jnp.einsum('bhts,bshe->bthe', att, vc)
        S_new = gL[:, None, None] * S + jnp.einsum('bshd,bshe->bhde', kc * end[None, :, :, None], vc)
        return S_new, o

    S, o = lax.scan(step, S0, tuple(_to_chunks(a, L) for a in (q, k, v)))
    return _from_chunks(o), S


def _even_mixer(h, s_gdn, s_conv, s_mc, s_mn, s_mm, w_in, w_out, conv_w, a_log, dt_bias,
                gdn_norm_w, gate_b, mlstm_norm_w):
    B, T, _ = h.shape
    f32 = jnp.float32
    proj = (h @ w_in).astype(f32)
    u_a, beta_pre, a_pre, q_b, k_b, v_b, if_pre, o_pre, z = _split(proj, EVEN_SPLIT)
    u_conv, new_conv = _causal_conv(u_a, s_conv.astype(f32), conv_w.astype(f32))
    q_a, k_a, v_a = _split(jax.nn.silu(u_conv), [H_A * DK_A, H_A * DK_A, H_A * DV_A])
    q_a = _l2n(q_a.reshape(B, T, H_A, DK_A)) * (DK_A ** -0.5)
    k_a = _l2n(k_a.reshape(B, T, H_A, DK_A))
    v_a = v_a.reshape(B, T, H_A, DV_A)
    beta = jax.nn.sigmoid(beta_pre)
    g_a = -jnp.exp(a_log.astype(f32)) * jax.nn.softplus(a_pre + dt_bias.astype(f32))
    o_a, new_gdn = _gated_delta(q_a, k_a, v_a, beta, g_a, s_gdn.astype(f32))
    o_a = _rms(o_a, gdn_norm_w).reshape(B, T, H_A * DV_A)
    if_pre = if_pre + gate_b.astype(f32)
    q_b = q_b.reshape(B, T, H_B, DK_B)
    k_b = k_b.reshape(B, T, H_B, DK_B) * (DK_B ** -0.5)
    v_b = v_b.reshape(B, T, H_B, DV_B)
    h_b, new_mc, new_mn, new_mm = _mlstm(q_b, k_b, v_b, if_pre[..., :H_B], if_pre[..., H_B:],
                                         s_mc.astype(f32), s_mn.astype(f32), s_mm.astype(f32))
    h_b = jax.nn.sigmoid(o_pre) * _rms(h_b, mlstm_norm_w).reshape(B, T, H_B * DV_B)
    y = jnp.concatenate([o_a, h_b], axis=-1) * jax.nn.silu(z)
    return y.astype(h.dtype) @ w_out, new_gdn, new_conv, new_mc, new_mn, new_mm


def _odd_mixer(h, pos, s_gla, s_ret, w_in, w_out, gla_w2, gla_b2, gla_norm_w, ret_norm_w):
    B, T, _ = h.shape
    f32 = jnp.float32
    proj = (h @ w_in).astype(f32)
    q_c, k_c, v_c, g_lr, q_d, k_d, v_d, z = _split(proj, ODD_SPLIT)
    g_c = jax.nn.log_sigmoid(g_lr @ gla_w2.astype(f32) + gla_b2.astype(f32)) / GLA_TAU
    o_c, new_gla = _gla(q_c.reshape(B, T, H_C, DK_C) * (DK_C ** -0.5), k_c.reshape(B, T, H_C, DK_C),
                        v_c.reshape(B, T, H_C, DV_C), g_c.reshape(B, T, H_C, DK_C), s_gla.astype(f32))
    o_c = _rms(o_c, gla_norm_w).reshape(B, T, H_C * DV_C)
    q_d = _rotary(q_d.reshape(B, T, H_D, DK_D), pos)
    k_d = _rotary(k_d.reshape(B, T, H_D, DK_D), pos) * (DK_D ** -0.5)
    o_d, new_ret = _retention(q_d, k_d, v_d.reshape(B, T, H_D, DV_D), s_ret.astype(f32))
    o_d = _rms(o_d, ret_norm_w).reshape(B, T, H_D * DV_D)
    y = jnp.concatenate([o_c, o_d], axis=-1) * jax.nn.silu(z)
    return y.astype(h.dtype) @ w_out, new_gla, new_ret


def _trunk(x, c, pos, st_gdn, st_conv, st_mc, st_mn, st_mm, st_gla, st_ret,
           ada_w, ada_b, norm_w, ev_w_in, ev_w_out, gdn_conv_w, gdn_a_log, gdn_dt_bias, gdn_norm_w,
           mlstm_gate_b, mlstm_norm_w, od_w_in, od_w_out, gla_w2, gla_b2, gla_norm_w, ret_norm_w,
           final_norm_w):
    dt = x.dtype
    n_gdn, n_conv, n_mc, n_mn, n_mm, n_gla, n_ret = [], [], [], [], [], [], []
    cs = jax.nn.silu(c)
    for layer in range(DEPTH):
        mod = cs @ ada_w[layer] + ada_b[layer]
        shift, scale, gate = jnp.split(mod[:, None, :], 3, axis=-1)
        h = _rms(x, norm_w[layer]) * (1.0 + scale) + shift
        j = layer // 2
        if layer % 2 == 0:
            y, s1, s2, s3, s4, s5 = _even_mixer(h, st_gdn[j], st_conv[j], st_mc[j], st_mn[j], st_mm[j],
                                                ev_w_in[j], ev_w_out[j], gdn_conv_w[j], gdn_a_log[j],
                                                gdn_dt_bias[j], gdn_norm_w[j], mlstm_gate_b[j], mlstm_norm_w[j])
            n_gdn.append(s1.astype(dt))
            n_conv.append(s2.astype(dt))
            n_mc.append(s3.astype(dt))
            n_mn.append(s4.astype(dt))
            n_mm.append(s5.astype(dt))
        else:
            y, s6, s7 = _odd_mixer(h, pos, st_gla[j], st_ret[j], od_w_in[j], od_w_out[j], gla_w2[j],
                                   gla_b2[j], gla_norm_w[j], ret_norm_w[j])
            n_gla.append(s6.astype(dt))
            n_ret.append(s7.astype(dt))
        x = x + gate * y
    return (_rms(x, final_norm_w), jnp.stack(n_gdn), jnp.stack(n_conv), jnp.stack(n_mc),
            jnp.stack(n_mn), jnp.stack(n_mm), jnp.stack(n_gla), jnp.stack(n_ret))


def setup_inputs(seed: int = 0) -> dict:
    key = jax.random.key(seed)
    k = jax.random.split(key, 32)
    f32 = jnp.float32

    def nrm(i, shape, s=1.0):
        return s * jax.random.normal(k[i], shape, f32)

    dt0 = jnp.exp(jax.random.uniform(k[18], (N_EVEN, H_A), f32, math.log(1e-3), math.log(1e-1)))
    return {
        'x_prompt': nrm(0, (BATCH, SEQ, D_MODEL)),
        'x_sample': nrm(1, (DEC_BATCH, DEC_SEQ, D_MODEL)),
        'c_prompt': nrm(2, (BATCH, D_MODEL)),
        'c_sample': nrm(3, (DEC_BATCH, D_MODEL)),
        'state_gdn': nrm(4, (N_EVEN, DEC_BATCH, H_A, DK_A, DV_A), 0.1),
        'state_gdn_conv': nrm(5, (N_EVEN, DEC_BATCH, CONV_W - 1, CONV_CH_A)),
        'state_mlstm_c': nrm(6, (N_EVEN, DEC_BATCH, H_B, DK_B, DV_B), 0.1),
        'state_mlstm_n': nrm(7, (N_EVEN, DEC_BATCH, H_B, DK_B), 0.3),
        'state_mlstm_m': nrm(8, (N_EVEN, DEC_BATCH, H_B)),
        'state_gla': nrm(9, (N_ODD, DEC_BATCH, H_C, DK_C, DV_C), 0.1),
        'state_ret': nrm(10, (N_ODD, DEC_BATCH, H_D, DK_D, DV_D), 0.1),
        'ada_w': nrm(11, (DEPTH, D_MODEL, 3 * D_MODEL), 0.5 * D_MODEL ** -0.5),
        'ada_b': nrm(12, (DEPTH, 3 * D_MODEL), 0.02),
        'norm_w': 1.0 + nrm(13, (DEPTH, D_MODEL), 0.05),
        'ev_w_in': nrm(14, (N_EVEN, D_MODEL, P_EVEN), D_MODEL ** -0.5),
        'ev_w_out': nrm(15, (N_EVEN, W_MIX, D_MODEL), W_MIX ** -0.5),
        'gdn_conv_w': nrm(16, (N_EVEN, CONV_W, CONV_CH_A), 0.5),
        'gdn_a_log': jnp.log(jax.random.uniform(k[17], (N_EVEN, H_A), f32, 1.0, 16.0)),
        'gdn_dt_bias': dt0 + jnp.log(-jnp.expm1(-dt0)),
        'gdn_norm_w': 1.0 + nrm(19, (N_EVEN, DV_A), 0.05),
        'mlstm_gate_b': jnp.concatenate([nrm(20, (N_EVEN, H_B), 0.1),
                                         jnp.linspace(3.0, 6.0, H_B, dtype=f32)[None, :] + nrm(21, (N_EVEN, H_B), 0.1)], axis=-1),
        'mlstm_norm_w': 1.0 + nrm(22, (N_EVEN, DV_B), 0.05),
        'od_w_in': nrm(23, (N_ODD, D_MODEL, P_ODD), D_MODEL ** -0.5),
        'od_w_out': nrm(24, (N_ODD, W_MIX, D_MODEL), W_MIX ** -0.5),
        'gla_w2': nrm(25, (N_ODD, GLA_RANK, H_C * DK_C), GLA_RANK ** -0.5),
        'gla_b2': nrm(26, (N_ODD, H_C * DK_C), 0.1),
        'gla_norm_w': 1.0 + nrm(27, (N_ODD, DV_C), 0.05),
        'ret_norm_w': 1.0 + nrm(28, (N_ODD, DV_D), 0.05),
        'final_norm_w': 1.0 + nrm(29, (D_MODEL,), 0.05),
    }


def reference(x_prompt, x_sample, c_prompt, c_sample, state_gdn, state_gdn_conv, state_mlstm_c,
              state_mlstm_n, state_mlstm_m, state_gla, state_ret, ada_w, ada_b, norm_w, ev_w_in,
              ev_w_out, gdn_conv_w, gdn_a_log, gdn_dt_bias, gdn_norm_w, mlstm_gate_b, mlstm_norm_w,
              od_w_in, od_w_out, gla_w2, gla_b2, gla_norm_w, ret_norm_w, final_norm_w):
    weights = (ada_w, ada_b, norm_w, ev_w_in, ev_w_out, gdn_conv_w, gdn_a_log, gdn_dt_bias, gdn_norm_w,
               mlstm_gate_b, mlstm_norm_w, od_w_in, od_w_out, gla_w2, gla_b2, gla_norm_w, ret_norm_w,
               final_norm_w)
    f32 = jnp.float32
    Bp, Tp = x_prompt.shape[0], x_prompt.shape[1]
    z_gdn = jnp.zeros((N_EVEN, Bp, H_A, DK_A, DV_A), f32)
    z_conv = jnp.zeros((N_EVEN, Bp, CONV_W - 1, CONV_CH_A), f32)
    z_mc = jnp.zeros((N_EVEN, Bp, H_B, DK_B, DV_B), f32)
    z_mn = jnp.zeros((N_EVEN, Bp, H_B, DK_B), f32)
    z_mm = jnp.zeros((N_EVEN, Bp, H_B), f32)
    z_gla = jnp.zeros((N_ODD, Bp, H_C, DK_C, DV_C), f32)
    z_ret = jnp.zeros((N_ODD, Bp, H_D, DK_D, DV_D), f32)
    pos_p = jnp.arange(Tp)
    pos_s = PAST_LEN + jnp.arange(x_sample.shape[1])
    y_prompt, p_gdn, p_conv, p_mc, p_mn, p_mm, p_gla, p_ret = _trunk(
        x_prompt, c_prompt, pos_p, z_gdn, z_conv, z_mc, z_mn, z_mm, z_gla, z_ret, *weights)
    y_sample, s_gdn, s_conv, s_mc, s_mn, s_mm, s_gla, s_ret = _trunk(
        x_sample, c_sample, pos_s, state_gdn, state_gdn_conv, state_mlstm_c, state_mlstm_n,
        state_mlstm_m, state_gla, state_ret, *weights)
    return (y_prompt, y_sample, p_gdn, p_conv, p_mc, p_mn, p_mm, p_gla, p_ret,
            s_gdn, s_conv, s_mc, s_mn, s_mm, s_gla, s_ret)
```

```python
import functools
import math

import jax
import jax.numpy as jnp
from jax import lax
from jax.experimental import pallas as pl
from jax.experimental.pallas import tpu as pltpu

F32 = jnp.float32
BF16 = jnp.bfloat16
HI = lax.Precision.HIGHEST

D_MODEL = 1024
W_MIX = 2048
H_A, DK_A, DV_A, CONV_W = 8, 128, 128, 4
CONV_CH = H_A * (2 * DK_A + DV_A)
H_B, DK_B, DV_B = 8, 64, 128
H_C, DK_C, DV_C, GLA_RANK, GLA_TAU = 4, 128, 256, 16, 16.0
H_D, DK_D, DV_D = 4, 128, 256
ROPE_BASE = 10000.0
PAST_LEN = 16384
EPS = 1e-6
LANE = 128

CHUNK = 64
CHUNK_C = 16

E_U, E_SM, E_QK, E_V, E_OP, E_Z = 0, 3072, 3584, 4608, 5632, 6656
P_E = 8704
O_QC, O_KC, O_VC, O_GLR, O_QD, O_KD, O_VD, O_Z = 0, 512, 1024, 2048, 2304, 2816, 3328, 4352
P_O = 6400

VMEM_LIMIT = 56 * 1024 * 1024


def _sigmoid(x):
    return 1.0 / (1.0 + jnp.exp(-x))


def _softplus(x):
    return jnp.maximum(x, 0.0) + jnp.log(1.0 + jnp.exp(-jnp.abs(x)))


def _log_sigmoid(x):
    return -_softplus(-x)


def _silu(x):
    return x * _sigmoid(x)


def _rms_rows(x, w):
    return x * lax.rsqrt(jnp.mean(x * x, axis=-1, keepdims=True) + EPS) * w


def _l2n(x):
    return x * lax.rsqrt(jnp.sum(x * x, axis=-1, keepdims=True) + EPS)


def _dot(a, b, precision=None):
    return jnp.dot(a, b, preferred_element_type=F32, precision=precision)


def _dot_nt(a, b, precision=None):
    return lax.dot_general(a, b, (((1,), (1,)), ((), ())), preferred_element_type=F32,
                           precision=precision)


def _dot_tn(a, b, precision=None):
    return lax.dot_general(a, b, (((0,), (0,)), ((), ())), preferred_element_type=F32,
                           precision=precision)


def _eye(n):
    r = lax.broadcasted_iota(jnp.int32, (n, n), 0)
    c = lax.broadcasted_iota(jnp.int32, (n, n), 1)
    return (r == c).astype(F32)


def _transpose(x):
    return _dot_nt(_eye(x.shape[1]), x, precision=HI)


def _tri_masks(n):
    r = lax.broadcasted_iota(jnp.int32, (n, n), 0)
    c = lax.broadcasted_iota(jnp.int32, (n, n), 1)
    return r >= c, r > c


def _mod_kernel(c_ref, w_ref, b_ref, o_ref):
    cs = _silu(c_ref[...]).astype(BF16)
    o_ref[...] = _dot(cs, w_ref[...].astype(BF16)) + b_ref[...]


def _modulation(c_all, ada_w, ada_b):
    depth, d, d3 = ada_w.shape
    rows = c_all.shape[0]
    tn = 1024
    return pl.pallas_call(
        _mod_kernel,
        grid=(depth, d3 // tn),
        in_specs=[
            pl.BlockSpec((rows, d), lambda l, j: (0, 0)),
            pl.BlockSpec((None, d, tn), lambda l, j: (l, 0, j)),
            pl.BlockSpec((None, 1, tn), lambda l, j: (l, 0, j)),
        ],
        out_specs=pl.BlockSpec((None, rows, tn), lambda l, j: (l, 0, j)),
        out_shape=jax.ShapeDtypeStruct((depth, rows, d3), F32),
        compiler_params=pltpu.CompilerParams(
            dimension_semantics=("arbitrary", "arbitrary"), vmem_limit_bytes=VMEM_LIMIT),
        name="modulation",
    )(c_all, ada_w, ada_b.reshape(depth, 1, d3))


def _inproj_kernel(x_ref, mod_ref, nw_ref, w_ref, o_ref, h_ref):
    @pl.when(pl.program_id(1) == 0)
    def _():
        x = x_ref[...]
        y = _rms_rows(x, nw_ref[...])
        shift = mod_ref[:, 0:D_MODEL]
        scale = mod_ref[:, D_MODEL:2 * D_MODEL]
        h_ref[...] = (y * (1.0 + scale) + shift).astype(BF16)

    o_ref[...] = _dot(h_ref[...], w_ref[...])


def _inproj(x, mod, norm_w, w, tm, tn):
    n, d = x.shape
    p = w.shape[1]
    g, r, _ = mod.shape
    tiles_per_group = (n // g) // tm
    return pl.pallas_call(
        _inproj_kernel,
        grid=(n // tm, p // tn),
        in_specs=[
            pl.BlockSpec((tm, d), lambda i, j: (i, 0)),
            pl.BlockSpec((None, r, 3 * d), lambda i, j: (i // tiles_per_group, 0, 0)),
            pl.BlockSpec((1, d), lambda i, j: (0, 0)),
            pl.BlockSpec((d, tn), lambda i, j: (0, j)),
        ],
        out_specs=pl.BlockSpec((tm, tn), lambda i, j: (i, j)),
        out_shape=jax.ShapeDtypeStruct((n, p), F32),
        scratch_shapes=[pltpu.VMEM((tm, d), BF16)],
        compiler_params=pltpu.CompilerParams(
            dimension_semantics=("arbitrary", "arbitrary"), vmem_limit_bytes=VMEM_LIMIT),
        name="inproj",
    )(x, mod, norm_w.reshape(1, d), w)


def _outproj_kernel(y_ref, x_ref, mod_ref, w_ref, fw_ref, o_ref, *, final):
    acc = _dot(y_ref[...].astype(BF16), w_ref[...])
    gate = mod_ref[:, 2 * D_MODEL:3 * D_MODEL]
    xn = x_ref[...] + gate * acc
    if final:
        xn = _rms_rows(xn, fw_ref[...])
    o_ref[...] = xn


def _outproj(y, x, mod, w, final_w, tm, final):
    n, d = x.shape
    k = y.shape[1]
    g, r, _ = mod.shape
    tiles_per_group = (n // g) // tm
    return pl.pallas_call(
        functools.partial(_outproj_kernel, final=final),
        grid=(n // tm,),
        in_specs=[
            pl.BlockSpec((tm, k), lambda i: (i, 0)),
            pl.BlockSpec((tm, d), lambda i: (i, 0)),
            pl.BlockSpec((None, r, 3 * d), lambda i: (i // tiles_per_group, 0, 0)),
            pl.BlockSpec((k, d), lambda i: (0, 0)),
            pl.BlockSpec((1, d), lambda i: (0, 0)),
        ],
        out_specs=pl.BlockSpec((tm, d), lambda i: (i, 0)),
        out_shape=jax.ShapeDtypeStruct((n, d), F32),
        compiler_params=pltpu.CompilerParams(
            dimension_semantics=("arbitrary",), vmem_limit_bytes=VMEM_LIMIT),
        name="outproj",
    )(y, x, mod, w, final_w.reshape(1, d))


def _inv_unit_lower(a, n):
    x = _eye(n) - a
    p = a
    for _ in range(int(math.log2(n)) - 1):
        p = _dot(p, p, precision=HI)
        x = x + _dot(x, p, precision=HI)
    return x


def _even_prompt_kernel(proj_ref, sp_ref, cw_ref, gnw_ref, mnw_ref,
                        y_ref, gdn_ref, conv_ref, mc_ref, mn_ref, mm_ref,
                        ubuf_ref, qkv_ref):
    L = CHUNK
    c = pl.program_id(1)

    @pl.when(c == 0)
    def _():
        gdn_ref[...] = jnp.zeros_like(gdn_ref)
        mc_ref[...] = jnp.zeros_like(mc_ref)
        mn_ref[...] = jnp.zeros_like(mn_ref)
        mm_ref[...] = jnp.zeros_like(mm_ref)
        ubuf_ref[0:8, :] = jnp.zeros((8, CONV_CH), F32)

    tri_incl, tri_strict = _tri_masks(L)
    tri_f = tri_incl.astype(F32)

    u = proj_ref[:, E_U:E_U + CONV_CH]
    ubuf_ref[8:8 + L, :] = u
    cw = cw_ref[...]
    uc = (cw[0:1] * ubuf_ref[5:5 + L, :] + cw[1:2] * ubuf_ref[6:6 + L, :]
          + cw[2:3] * ubuf_ref[7:7 + L, :] + cw[3:4] * u)
    ubuf_ref[0:8, :] = u[L - 8:L, :]
    conv_ref[...] = u[L - 3:L, :]
    qkv_ref[...] = _silu(uc)

    sm = proj_ref[:, E_SM:E_SM + 512]
    beta = _sigmoid(sm[:, 0:128])
    g = -jnp.exp(sp_ref[1:2, :]) * _softplus(sm[:, 128:256] + sp_ref[0:1, :])
    ig = sm[:, 256:384] + sp_ref[2:3, :]
    lf = _log_sigmoid(sm[:, 384:512] + sp_ref[3:4, :])
    cum_g = _dot(tri_f, g, precision=HI)
    cum_f = _dot(tri_f, lf, precision=HI)
    cum_g_t = _transpose(cum_g)
    cum_f_t = _transpose(cum_f)
    ig_t = _transpose(ig)

    gnw = gnw_ref[...]
    mnw = mnw_ref[...]

    for h in range(H_A):
        q = qkv_ref[:, h * DK_A:(h + 1) * DK_A]
        k = qkv_ref[:, H_A * DK_A + h * DK_A:H_A * DK_A + (h + 1) * DK_A]
        v = qkv_ref[:, 2 * H_A * DK_A + h * DV_A:2 * H_A * DK_A + (h + 1) * DV_A]
        q = _l2n(q) * (DK_A ** -0.5)
        k = _l2n(k)
        bh_c = cum_g[:, h:h + 1]
        bh_r = cum_g_t[h:h + 1, :]
        diff = bh_c - bh_r
        decay = jnp.where(tri_incl, jnp.exp(jnp.where(tri_incl, diff, 0.0)), 0.0)
        beta_c = beta[:, h:h + 1]
        s = gdn_ref[h]
        sb = s.astype(BF16)
        qb = q.astype(BF16)
        kb = k.astype(BF16)
        kk = _dot_nt(kb, kb)
        a = jnp.where(tri_strict, decay * kk, 0.0) * beta_c
        gam = jnp.exp(bh_c)
        rhs = beta_c * (v - gam * _dot(kb, sb))
        uu = _dot(_inv_unit_lower(a, L), rhs, precision=HI)
        ub = uu.astype(BF16)
        qk = _dot_nt(qb, kb) * decay
        o = gam * _dot(qb, sb) + _dot(qk.astype(BF16), ub)
        last = bh_c[L - 1:L, :]
        kw = (k * jnp.exp(last - bh_c)).astype(BF16)
        gdn_ref[h] = jnp.exp(last) * s + _dot_tn(kw, ub)
        z = proj_ref[:, E_Z + h * DV_A:E_Z + (h + 1) * DV_A]
        y_ref[:, h * DV_A:(h + 1) * DV_A] = (_rms_rows(o, gnw) * _silu(z)).astype(BF16)

    for h in range(H_B):
        qk_in = proj_ref[:, E_QK + h * 128:E_QK + (h + 1) * 128]
        q = qk_in[:, 0:DK_B]
        k = qk_in[:, DK_B:2 * DK_B] * (DK_B ** -0.5)
        v = proj_ref[:, E_V + h * DV_B:E_V + (h + 1) * DV_B]
        b_c = cum_f[:, h:h + 1]
        b_r = cum_f_t[h:h + 1, :]
        ih_r = ig_t[h:h + 1, :]
        ih_c = ig[:, h:h + 1]
        m0 = mm_ref[0:1, h:h + 1]
        dmat = jnp.where(tri_incl, b_c - b_r + ih_r, -jnp.inf)
        w0 = b_c + m0
        m_t = jnp.maximum(w0, jnp.max(dmat, axis=-1, keepdims=True))
        qb = q.astype(BF16)
        kb = k.astype(BF16)
        vb = v.astype(BF16)
        p = jnp.exp(dmat - m_t) * _dot_nt(qb, kb)
        s0 = jnp.exp(w0 - m_t)
        cmat = mc_ref[h]
        n = mn_ref[h:h + 1, :]
        num = s0 * _dot(qb, cmat.astype(BF16)) + _dot(p.astype(BF16), vb)
        den = s0 * jnp.sum(q * n, axis=-1, keepdims=True) + jnp.sum(p, axis=-1, keepdims=True)
        hh = num / jnp.maximum(jnp.abs(den), jnp.exp(-m_t))
        m_end = m_t[L - 1:L, :]
        b_last = b_c[L - 1:L, :]
        we = jnp.exp(b_last - b_c + ih_c - m_end)
        se = jnp.exp(b_last + m0 - m_end)
        kw = k * we
        mc_ref[h] = se * cmat + _dot_tn(kw.astype(BF16), vb)
        mn_ref[h:h + 1, :] = se * n + jnp.sum(kw, axis=0, keepdims=True)
        mm_ref[0:1, h:h + 1] = m_end
        o_pre = proj_ref[:, E_OP + h * DV_B:E_OP + (h + 1) * DV_B]
        z = proj_ref[:, E_Z + H_A * DV_A + h * DV_B:E_Z + H_A * DV_A + (h + 1) * DV_B]
        y_ref[:, H_A * DV_A + h * DV_B:H_A * DV_A + (h + 1) * DV_B] = (
            _sigmoid(o_pre) * _rms_rows(hh, mnw) * _silu(z)).astype(BF16)


def _even_prompt(proj, sp, conv_w, gnw, mnw, batch, seq):
    nc = seq // CHUNK
    row = lambda b, c: (b * nc + c, 0)
    const2 = lambda b, c: (0, 0)
    return pl.pallas_call(
        _even_prompt_kernel,
        grid=(batch, nc),
        in_specs=[
            pl.BlockSpec((CHUNK, P_E), row),
            pl.BlockSpec((8, LANE), const2),
            pl.BlockSpec((CONV_W, CONV_CH), const2),
            pl.BlockSpec((1, DV_A), const2),
            pl.BlockSpec((1, DV_B), const2),
        ],
        out_specs=[
            pl.BlockSpec((CHUNK, W_MIX), row),
            pl.BlockSpec((None, H_A, DK_A, DV_A), lambda b, c: (b, 0, 0, 0)),
            pl.BlockSpec((None, CONV_W - 1, CONV_CH), lambda b, c: (b, 0, 0)),
            pl.BlockSpec((None, H_B, DK_B, DV_B), lambda b, c: (b, 0, 0, 0)),
            pl.BlockSpec((None, H_B, DK_B), lambda b, c: (b, 0, 0)),
            pl.BlockSpec((None, 1, LANE), lambda b, c: (b, 0, 0)),
        ],
        out_shape=[
            jax.ShapeDtypeStruct((batch * seq, W_MIX), BF16),
            jax.ShapeDtypeStruct((batch, H_A, DK_A, DV_A), F32),
            jax.ShapeDtypeStruct((batch, CONV_W - 1, CONV_CH), F32),
            jax.ShapeDtypeStruct((batch, H_B, DK_B, DV_B), F32),
            jax.ShapeDtypeStruct((batch, H_B, DK_B), F32),
            jax.ShapeDtypeStruct((batch, 1, LANE), F32),
        ],
        scratch_shapes=[pltpu.VMEM((8 + CHUNK, CONV_CH), F32), pltpu.VMEM((CHUNK, CONV_CH), F32)],
        compiler_params=pltpu.CompilerParams(
            dimension_semantics=("arbitrary", "arbitrary"), vmem_limit_bytes=VMEM_LIMIT),
        name="even_prompt",
    )(proj, sp, conv_w, gnw, mnw)


def _ret_log_gamma(h):
    return math.log(1.0 - 2.0 ** (-5.0 - h))


def _odd_prompt_kernel(proj_ref, w2_ref, b2_ref, cc_ref, ss_ref, cnw_ref, dnw_ref,
                       y_ref, gla_ref, ret_ref, st_ref, b_ref, oc_ref):
    L = CHUNK
    LC = CHUNK_C
    c = pl.program_id(1)
    nc = pl.num_programs(1)

    @pl.when(c == 0)
    def _():
        st_ref[...] = jnp.zeros_like(st_ref)
        ret_ref[...] = jnp.zeros_like(ret_ref)

    tri_incl, _ = _tri_masks(L)
    r64 = lax.broadcasted_iota(jnp.int32, (L, L), 0)
    c64 = lax.broadcasted_iota(jnp.int32, (L, L), 1)
    sub = int(math.log2(LC))
    blocktri = jnp.logical_and(r64 >= c64, (r64 >> sub) == (c64 >> sub)).astype(F32)
    tri16, _ = _tri_masks(LC)
    lane16 = lax.broadcasted_iota(jnp.int32, (LC, LC), 1)

    glr = proj_ref[:, O_GLR:O_GLR + 128].astype(BF16)
    gc = _log_sigmoid(_dot(glr, w2_ref[...]) + b2_ref[...]) / GLA_TAU
    b_ref[...] = _dot(blocktri, gc, precision=HI)

    cnw = cnw_ref[...]
    dnw = dnw_ref[...]

    for h in range(H_C):
        for j in range(L // LC):
            rows = slice(j * LC, (j + 1) * LC)
            q = proj_ref[rows, O_QC + h * DK_C:O_QC + (h + 1) * DK_C] * (DK_C ** -0.5)
            k = proj_ref[rows, O_KC + h * DK_C:O_KC + (h + 1) * DK_C]
            v = proj_ref[rows, O_VC + h * DV_C:O_VC + (h + 1) * DV_C].astype(BF16)
            b = b_ref[rows, h * DK_C:(h + 1) * DK_C]
            att = jnp.zeros((LC, LC), F32)
            for s_i in range(LC):
                w = q * k[s_i:s_i + 1, :] * jnp.exp(jnp.minimum(b - b[s_i:s_i + 1, :], 0.0))
                col = jnp.sum(w, axis=-1, keepdims=True)
                att = jnp.where(lane16 == s_i, col, att)
            att = jnp.where(tri16, att, 0.0)
            st = st_ref[h]
            o = _dot_nt((q * jnp.exp(b)).astype(BF16), st.astype(BF16)) + _dot(att.astype(BF16), v)
            oc_ref[rows, :] = o
            bl = b[LC - 1:LC, :]
            kw = (k * jnp.exp(bl - b)).astype(BF16)
            st_ref[h] = st * jnp.exp(bl) + _dot_tn(v, kw)
        z = proj_ref[:, O_Z + h * DV_C:O_Z + (h + 1) * DV_C]
        y_ref[:, h * DV_C:(h + 1) * DV_C] = (_rms_rows(oc_ref[...], cnw) * _silu(z)).astype(BF16)

    @pl.when(c == nc - 1)
    def _():
        for h in range(H_C):
            st = st_ref[h]
            gla_ref[h, :, 0:128] = _transpose(st[0:128, :])
            gla_ref[h, :, 128:256] = _transpose(st[128:256, :])

    cc = cc_ref[...]
    ss = ss_ref[...]
    rel = (r64 - c64).astype(F32)
    tcol = lax.broadcasted_iota(jnp.int32, (L, 1), 0).astype(F32)
    for h in range(H_D):
        lg = _ret_log_gamma(h)
        dm = jnp.where(tri_incl, jnp.exp(lg * jnp.maximum(rel, 0.0)), 0.0)
        inner = jnp.exp(lg * (tcol + 1.0))
        end = jnp.exp(lg * (L - 1.0 - tcol))
        q = proj_ref[:, O_QD + h * DK_D:O_QD + (h + 1) * DK_D]
        k = proj_ref[:, O_KD + h * DK_D:O_KD + (h + 1) * DK_D]
        q = q * cc + pltpu.roll(q, DK_D // 2, 1) * ss
        k = (k * cc + pltpu.roll(k, DK_D // 2, 1) * ss) * (DK_D ** -0.5)
        v = proj_ref[:, O_VD + h * DV_D:O_VD + (h + 1) * DV_D].astype(BF16)
        qb = q.astype(BF16)
        kb = k.astype(BF16)
        s = ret_ref[h]
        att = _dot_nt(qb, kb) * dm
        o = _dot(qb, s.astype(BF16)) * inner + _dot(att.astype(BF16), v)
        ret_ref[h] = math.exp(lg * L) * s + _dot_tn((k * end).astype(BF16), v)
        z = proj_ref[:, O_Z + H_C * DV_C + h * DV_D:O_Z + H_C * DV_C + (h + 1) * DV_D]
        y_ref[:, H_C * DV_C + h * DV_D:H_C * DV_C + (h + 1) * DV_D] = (
            _rms_rows(o, dnw) * _silu(z)).astype(BF16)


def _odd_prompt(proj, w2, b2, cc, ss, cnw, dnw, batch, seq):
    nc = seq // CHUNK
    row = lambda b, c: (b * nc + c, 0)
    const2 = lambda b, c: (0, 0)
    return pl.pallas_call(
        _odd_prompt_kernel,
        grid=(batch, nc),
        in_specs=[
            pl.BlockSpec((CHUNK, P_O), row),
            pl.BlockSpec((LANE, H_C * DK_C), const2),
            pl.BlockSpec((1, H_C * DK_C), const2),
            pl.BlockSpec((CHUNK, DK_D), lambda b, c: (c, 0)),
            pl.BlockSpec((CHUNK, DK_D), lambda b, c: (c, 0)),
            pl.BlockSpec((1, DV_C), const2),
            pl.BlockSpec((1, DV_D), const2),
        ],
        out_specs=[
            pl.BlockSpec((CHUNK, W_MIX), row),
            pl.BlockSpec((None, H_C, DK_C, DV_C), lambda b, c: (b, 0, 0, 0)),
            pl.BlockSpec((None, H_D, DK_D, DV_D), lambda b, c: (b, 0, 0, 0)),
        ],
        out_shape=[
            jax.ShapeDtypeStruct((batch * seq, W_MIX), BF16),
            jax.ShapeDtypeStruct((batch, H_C, DK_C, DV_C), F32),
            jax.ShapeDtypeStruct((batch, H_D, DK_D, DV_D), F32),
        ],
        scratch_shapes=[pltpu.VMEM((H_C, DV_C, DK_C), F32), pltpu.VMEM((CHUNK, H_C * DK_C), F32),
                        pltpu.VMEM((CHUNK, DV_C), F32)],
        compiler_params=pltpu.CompilerParams(
            dimension_semantics=("arbitrary", "arbitrary"), vmem_limit_bytes=VMEM_LIMIT),
        name="odd_prompt",
    )(proj, w2, b2, cc, ss, cnw, dnw)


DEC_BB = 8


def _even_decode_kernel(u_ref, sm_ref, qk_ref, v_ref, op_ref, z_ref,
                        conv_ref, gdn_ref, mc_ref, mn_ref, mm_ref,
                        sp_ref, cw_ref, gnw_ref, mnw_ref,
                        y_ref, conv_o, gdn_o, mc_o, mn_o, mm_o):
    gnw = gnw_ref[...]
    mnw = mnw_ref[...]

    def body(b, carry):
        sm = sm_ref[pl.ds(b, 1), :]
        beta = _sigmoid(sm[:, 0:128])
        gam = jnp.exp(-jnp.exp(sp_ref[1:2, :]) * _softplus(sm[:, 128:256] + sp_ref[0:1, :]))
        ig = sm[:, 256:384] + sp_ref[2:3, :]
        lf = _log_sigmoid(sm[:, 384:512] + sp_ref[3:4, :])
        m0 = mm_ref[pl.ds(b, 1), :]
        w0 = lf + m0
        m_t = jnp.maximum(w0, ig)
        s0 = jnp.exp(w0 - m_t)
        we = jnp.exp(ig - m_t)
        floor = jnp.exp(-m_t)
        mm_o[pl.ds(b, 1), :] = m_t

        u = u_ref[b]
        cs = conv_ref[b]
        cw = cw_ref[...]
        uc = cw[0] * cs[0] + cw[1] * cs[1] + cw[2] * cs[2] + cw[3] * u
        conv_o[b, 0] = cs[1]
        conv_o[b, 1] = cs[2]
        conv_o[b, 2] = u
        act = _silu(uc)
        q = _l2n(act[0:8]) * (DK_A ** -0.5)
        k = _l2n(act[8:16])
        v = act[16:24]
        qt = _transpose(q)
        kt = _transpose(k)
        z = z_ref[b]
        for h in range(H_A):
            s = gdn_ref[b, h]
            kc = kt[:, h:h + 1]
            g_h = gam[:, h:h + 1]
            ks = jnp.sum(kc * s, axis=0, keepdims=True)
            uu = beta[:, h:h + 1] * (v[h:h + 1, :] - g_h * ks)
            sn = g_h * s + kc * uu
            gdn_o[b, h] = sn
            o = jnp.sum(qt[:, h:h + 1] * sn, axis=0, keepdims=True)
            y_ref[b, h:h + 1, :] = _rms_rows(o, gnw) * _silu(z[h:h + 1, :])

        qk = qk_ref[b]
        qkt = _transpose(qk)
        vb = v_ref[b]
        op = op_ref[b]
        nmat = mn_ref[b]
        for h in range(H_B):
            cmat = mc_ref[b, h]
            qc = qkt[0:DK_B, h:h + 1]
            kc = qkt[DK_B:2 * DK_B, h:h + 1] * (DK_B ** -0.5)
            s0_h = s0[:, h:h + 1]
            we_h = we[:, h:h + 1]
            cn = s0_h * cmat + kc * (we_h * vb[h:h + 1, :])
            mc_o[b, h] = cn
            k_row = qk[h:h + 1, DK_B:2 * DK_B] * (DK_B ** -0.5)
            nn = s0_h * nmat[h:h + 1, :] + we_h * k_row
            mn_o[b, h:h + 1, :] = nn
            num = jnp.sum(qc * cn, axis=0, keepdims=True)
            den = jnp.sum(qk[h:h + 1, 0:DK_B] * nn, axis=-1, keepdims=True)
            hh = num / jnp.maximum(jnp.abs(den), floor[:, h:h + 1])
            y_ref[b, H_A + h:H_A + h + 1, :] = (
                _sigmoid(op[h:h + 1, :]) * _rms_rows(hh, mnw) * _silu(z[H_A + h:H_A + h + 1, :]))
        return carry

    lax.fori_loop(0, DEC_BB, body, 0)


def _even_decode(u, sm, qk, v, op, z, conv, gdn, mc, mn, mm, sp, cw, gnw, mnw):
    batch = u.shape[0]
    bb = DEC_BB
    i3 = lambda i: (i, 0, 0)
    i4 = lambda i: (i, 0, 0, 0)
    c2 = lambda i: (0, 0)
    return pl.pallas_call(
        _even_decode_kernel,
        grid=(batch // bb,),
        in_specs=[
            pl.BlockSpec((bb, 24, LANE), i3),
            pl.BlockSpec((bb, 512), lambda i: (i, 0)),
            pl.BlockSpec((bb, H_B, LANE), i3),
            pl.BlockSpec((bb, H_B, DV_B), i3),
            pl.BlockSpec((bb, H_B, DV_B), i3),
            pl.BlockSpec((bb, 16, LANE), i3),
            pl.BlockSpec((bb, CONV_W - 1, 24, LANE), i4),
            pl.BlockSpec((bb, H_A, DK_A, DV_A), i4),
            pl.BlockSpec((bb, H_B, DK_B, DV_B), i4),
            pl.BlockSpec((bb, H_B, DK_B), i3),
            pl.BlockSpec((bb, LANE), lambda i: (i, 0)),
            pl.BlockSpec((8, LANE), c2),
            pl.BlockSpec((CONV_W, 24, LANE), lambda i: (0, 0, 0)),
            pl.BlockSpec((1, DV_A), c2),
            pl.BlockSpec((1, DV_B), c2),
        ],
        out_specs=[
            pl.BlockSpec((bb, 16, LANE), i3),
            pl.BlockSpec((bb, CONV_W - 1, 24, LANE), i4),
            pl.BlockSpec((bb, H_A, DK_A, DV_A), i4),
            pl.BlockSpec((bb, H_B, DK_B, DV_B), i4),
            pl.BlockSpec((bb, H_B, DK_B), i3),
            pl.BlockSpec((bb, LANE), lambda i: (i, 0)),
        ],
        out_shape=[
            jax.ShapeDtypeStruct((batch, 16, LANE), F32),
            jax.ShapeDtypeStruct((batch, CONV_W - 1, 24, LANE), F32),
            jax.ShapeDtypeStruct((batch, H_A, DK_A, DV_A), F32),
            jax.ShapeDtypeStruct((batch, H_B, DK_B, DV_B), F32),
            jax.ShapeDtypeStruct((batch, H_B, DK_B), F32),
            jax.ShapeDtypeStruct((batch, LANE), F32),
        ],
        compiler_params=pltpu.CompilerParams(
            dimension_semantics=("arbitrary",), vmem_limit_bytes=VMEM_LIMIT),
        name="even_decode",
    )(u, sm, qk, v, op, z, conv, gdn, mc, mn, mm, sp, cw, gnw, mnw)


def _gla_gate_kernel(glr_ref, w2_ref, b2_ref, o_ref):
    o_ref[...] = _log_sigmoid(_dot(glr_ref[...].astype(BF16), w2_ref[...]) + b2_ref[...]) / GLA_TAU


def _gla_gate(glr, w2, b2):
    n = glr.shape[0]
    return pl.pallas_call(
        _gla_gate_kernel,
        out_shape=jax.ShapeDtypeStruct((n, H_C * DK_C), F32),
        name="gla_gate",
    )(glr, w2, b2)


def _odd_decode_kernel(qc_ref, kc_ref, vc_ref, g_ref, qd_ref, kd_ref, vd_ref, z_ref,
                       gla_ref, ret_ref, cc_ref, ss_ref, cnw_ref, dnw_ref,
                       y_ref, gla_o, ret_o):
    cnw = cnw_ref[...]
    dnw = dnw_ref[...]
    cc = cc_ref[...]
    ss = ss_ref[...]

    def body(b, carry):
        z = z_ref[b]
        q = qc_ref[b] * (DK_C ** -0.5)
        k = kc_ref[b]
        eg = jnp.exp(g_ref[b])
        t1 = _transpose(jnp.concatenate([q, k], axis=0))
        t2 = _transpose(jnp.concatenate([eg, eg], axis=0))
        v = vc_ref[b]
        for h in range(H_C):
            s = gla_ref[b, h]
            sn = t2[:, h:h + 1] * s + t1[:, H_C + h:H_C + h + 1] * v[h:h + 1, :]
            gla_o[b, h] = sn
            o = jnp.sum(t1[:, h:h + 1] * sn, axis=0, keepdims=True)
            y_ref[b, h:h + 1, :] = _rms_rows(o, cnw) * _silu(z[h:h + 1, :])
        qd = qd_ref[b]
        kd = kd_ref[b]
        qd = qd * cc + pltpu.roll(qd, DK_D // 2, 1) * ss
        kd = (kd * cc + pltpu.roll(kd, DK_D // 2, 1) * ss) * (DK_D ** -0.5)
        t3 = _transpose(jnp.concatenate([qd, kd], axis=0))
        vd = vd_ref[b]
        for h in range(H_D):
            gamma = math.exp(_ret_log_gamma(h))
            s = ret_ref[b, h]
            sn = gamma * s + t3[:, H_D + h:H_D + h + 1] * vd[h:h + 1, :]
            ret_o[b, h] = sn
            o = jnp.sum(t3[:, h:h + 1] * sn, axis=0, keepdims=True)
            y_ref[b, H_C + h:H_C + h + 1, :] = (
                _rms_rows(o, dnw) * _silu(z[H_C + h:H_C + h + 1, :]))
        return carry

    lax.fori_loop(0, DEC_BB, body, 0)


def _odd_decode(qc, kc, vc, g, qd, kd, vd, z, gla, ret, cc, ss, cnw, dnw):
    batch = qc.shape[0]
    bb = DEC_BB
    i3 = lambda i: (i, 0, 0)
    i4 = lambda i: (i, 0, 0, 0)
    c2 = lambda i: (0, 0)
    return pl.pallas_call(
        _odd_decode_kernel,
        grid=(batch // bb,),
        in_specs=[
            pl.BlockSpec((bb, H_C, DK_C), i3),
            pl.BlockSpec((bb, H_C, DK_C), i3),
            pl.BlockSpec((bb, H_C, DV_C), i3),
            pl.BlockSpec((bb, H_C, DK_C), i3),
            pl.BlockSpec((bb, H_D, DK_D), i3),
            pl.BlockSpec((bb, H_D, DK_D), i3),
            pl.BlockSpec((bb, H_D, DV_D), i3),
            pl.BlockSpec((bb, 8, DV_C), i3),
            pl.BlockSpec((bb, H_C, DK_C, DV_C), i4),
            pl.BlockSpec((bb, H_D, DK_D, DV_D), i4),
            pl.BlockSpec((1, DK_D), c2),
            pl.BlockSpec((1, DK_D), c2),
            pl.BlockSpec((1, DV_C), c2),
            pl.BlockSpec((1, DV_D), c2),
        ],
        out_specs=[
            pl.BlockSpec((bb, 8, DV_C), i3),
            pl.BlockSpec((bb, H_C, DK_C, DV_C), i4),
            pl.BlockSpec((bb, H_D, DK_D, DV_D), i4),
        ],
        out_shape=[
            jax.ShapeDtypeStruct((batch, 8, DV_C), F32),
            jax.ShapeDtypeStruct((batch, H_C, DK_C, DV_C), F32),
            jax.ShapeDtypeStruct((batch, H_D, DK_D, DV_D), F32),
        ],
        compiler_params=pltpu.CompilerParams(
            dimension_semantics=("arbitrary",), vmem_limit_bytes=VMEM_LIMIT),
        name="odd_decode",
    )(qc, kc, vc, g, qd, kd, vd, z, gla, ret, cc, ss, cnw, dnw)


def _pad_cols(w, width):
    return jnp.pad(w, ((0, 0), (0, width - w.shape[1])))


def _prep_even_w_in(w):
    u = w[:, 0:3072]
    beta = w[:, 3072:3080]
    a = w[:, 3080:3088]
    qb = w[:, 3088:3600].reshape(-1, H_B, DK_B)
    kb = w[:, 3600:4112].reshape(-1, H_B, DK_B)
    vb = w[:, 4112:5136]
    ig = w[:, 5136:5144]
    fg = w[:, 5144:5152]
    op = w[:, 5152:6176]
    z = w[:, 6176:8224]
    qk = jnp.concatenate([qb, kb], axis=-1).reshape(-1, H_B * 2 * DK_B)
    small = jnp.concatenate([_pad_cols(beta, 128), _pad_cols(a, 128), _pad_cols(ig, 128),
                             _pad_cols(fg, 128)], axis=1)
    return jnp.concatenate([u, small, qk, vb, op, z], axis=1).astype(BF16)


def _prep_odd_w_in(w):
    qc = w[:, 0:512]
    kc = w[:, 512:1024]
    vc = w[:, 1024:2048]
    glr = w[:, 2048:2064]
    qd = w[:, 2064:2576]
    kd = w[:, 2576:3088]
    vd = w[:, 3088:4112]
    z = w[:, 4112:6160]
    return jnp.concatenate([qc, kc, vc, _pad_cols(glr, 256), qd, kd, vd, z], axis=1).astype(BF16)


def _lane_row(vec, width=LANE):
    return jnp.pad(vec.astype(F32), (0, width - vec.shape[0]))


def _rotary_tables(pos):
    half = DK_D // 2
    inv = ROPE_BASE ** (-jnp.arange(half, dtype=F32) / half)
    ang = pos.astype(F32)[:, None] * inv[None, :]
    cos = jnp.cos(ang)
    sin = jnp.sin(ang)
    return jnp.concatenate([cos, cos], axis=-1), jnp.concatenate([-sin, sin], axis=-1)


def kernel(x_prompt, x_sample, c_prompt, c_sample, state_gdn, state_gdn_conv, state_mlstm_c,
           state_mlstm_n, state_mlstm_m, state_gla, state_ret, ada_w, ada_b, norm_w, ev_w_in,
           ev_w_out, gdn_conv_w, gdn_a_log, gdn_dt_bias, gdn_norm_w, mlstm_gate_b, mlstm_norm_w,
           od_w_in, od_w_out, gla_w2, gla_b2, gla_norm_w, ret_norm_w, final_norm_w):
    bp, tp, d = x_prompt.shape
    bs = x_sample.shape[0]
    n_p = bp * tp

    w_in_e = _prep_even_w_in(ev_w_in[0])
    w_in_o = _prep_odd_w_in(od_w_in[0])
    w_out_e = ev_w_out[0].astype(BF16)
    w_out_o = od_w_out[0].astype(BF16)
    sp = jnp.stack([_lane_row(gdn_dt_bias[0]), _lane_row(gdn_a_log[0]),
                    _lane_row(mlstm_gate_b[0, :H_B]), _lane_row(mlstm_gate_b[0, H_B:])]
                   + [jnp.zeros((LANE,), F32)] * 4)
    gnw = gdn_norm_w[0].reshape(1, DV_A)
    mnw = mlstm_norm_w[0].reshape(1, DV_B)
    cnw = gla_norm_w[0].reshape(1, DV_C)
    dnw = ret_norm_w[0].reshape(1, DV_D)
    w2 = jnp.pad(gla_w2[0], ((0, LANE - GLA_RANK), (0, 0))).astype(BF16)
    b2 = gla_b2[0].reshape(1, H_C * DK_C)

    mod = _modulation(jnp.concatenate([c_prompt, c_sample], axis=0), ada_w, ada_b)
    mod_p = [mod[l, :bp].reshape(bp, 1, 3 * d) for l in range(2)]
    mod_s = [mod[l, bp:].reshape(1, bs, 3 * d) for l in range(2)]

    xp = x_prompt.reshape(n_p, d)
    proj = _inproj(xp, mod_p[0], norm_w[0], w_in_e, 512, 2176)
    y, p_gdn, p_conv, p_mc, p_mn, p_mm = _even_prompt(proj, sp, gdn_conv_w[0], gnw, mnw, bp, tp)
    xp = _outproj(y, xp, mod_p[0], w_out_e, final_norm_w, 512, False)
    proj = _inproj(xp, mod_p[1], norm_w[1], w_in_o, 512, 1280)
    cc_p, ss_p = _rotary_tables(jnp.arange(tp))
    y, p_gla, p_ret = _odd_prompt(proj, w2, b2, cc_p, ss_p, cnw, dnw, bp, tp)
    y_prompt = _outproj(y, xp, mod_p[1], w_out_o, final_norm_w, 512, True).reshape(bp, tp, d)

    xs = x_sample.reshape(bs, d)
    proj = _inproj(xs, mod_s[0], norm_w[0], w_in_e, bs, 2176)
    mm_in = jnp.pad(state_mlstm_m[0], ((0, 0), (0, LANE - H_B)))
    y, s_conv, s_gdn, s_mc, s_mn, s_mm = _even_decode(
        proj[:, E_U:E_U + CONV_CH].reshape(bs, 24, LANE),
        proj[:, E_SM:E_SM + 512],
        proj[:, E_QK:E_QK + 1024].reshape(bs, H_B, LANE),
        proj[:, E_V:E_V + 1024].reshape(bs, H_B, DV_B),
        proj[:, E_OP:E_OP + 1024].reshape(bs, H_B, DV_B),
        proj[:, E_Z:E_Z + W_MIX].reshape(bs, 16, LANE),
        state_gdn_conv[0].reshape(bs, CONV_W - 1, 24, LANE),
        state_gdn[0], state_mlstm_c[0], state_mlstm_n[0], mm_in,
        sp, gdn_conv_w[0].reshape(CONV_W, 24, LANE), gnw, mnw)
    xs = _outproj(y.reshape(bs, W_MIX), xs, mod_s[0], w_out_e, final_norm_w, bs, False)
    proj = _inproj(xs, mod_s[1], norm_w[1], w_in_o, bs, 1280)
    g_s = _gla_gate(proj[:, O_GLR:O_GLR + LANE], w2, b2)
    cc_s, ss_s = _rotary_tables(PAST_LEN + jnp.arange(1))
    y, s_gla, s_ret = _odd_decode(
        proj[:, O_QC:O_QC + 512].reshape(bs, H_C, DK_C),
        proj[:, O_KC:O_KC + 512].reshape(bs, H_C, DK_C),
        proj[:, O_VC:O_VC + 1024].reshape(bs, H_C, DV_C),
        g_s.reshape(bs, H_C, DK_C),
        proj[:, O_QD:O_QD + 512].reshape(bs, H_D, DK_D),
        proj[:, O_KD:O_KD + 512].reshape(bs, H_D, DK_D),
        proj[:, O_VD:O_VD + 1024].reshape(bs, H_D, DV_D),
        proj[:, O_Z:O_Z + W_MIX].reshape(bs, 8, DV_C),
        state_gla[0], state_ret[0], cc_s, ss_s, cnw, dnw)
    y_sample = _outproj(y.reshape(bs, W_MIX), xs, mod_s[1], w_out_o, final_norm_w, bs, True)
    y_sample = y_sample.reshape(bs, 1, d)

    return (y_prompt, y_sample,
            p_gdn[None], p_conv[None], p_mc[None], p_mn[None], p_mm[None, :, 0, :H_B],
            p_gla[None], p_ret[None],
            s_gdn[None], s_conv.reshape(1, bs, CONV_W - 1, CONV_CH), s_mc[None], s_mn[None],
            s_mm[None, :, :H_B], s_gla[None], s_ret[None])
```

```python
import functools
import math

import jax
import jax.numpy as jnp
from jax import lax
from jax.experimental import pallas as pl
from jax.experimental.pallas import tpu as pltpu

F32 = jnp.float32
BF16 = jnp.bfloat16
HI = lax.Precision.HIGHEST

D_MODEL = 1024
W_MIX = 2048
H_A, DK_A, DV_A, CONV_W = 8, 128, 128, 4
CONV_CH = H_A * (2 * DK_A + DV_A)
H_B, DK_B, DV_B = 8, 64, 128
H_C, DK_C, DV_C, GLA_RANK, GLA_TAU = 4, 128, 256, 16, 16.0
H_D, DK_D, DV_D = 4, 128, 256
ROPE_BASE = 10000.0
PAST_LEN = 16384
EPS = 1e-6
LANE = 128

CHUNK = 64
CHUNK_C = 16

E_U, E_QK, E_V, E_OP, E_Z = 0, 3072, 4096, 5120, 6144
P_E = 8192
SMALL_E = 512
O_QC, O_KC, O_VC, O_QD, O_KD, O_VD, O_Z = 0, 512, 1024, 2048, 2560, 3072, 4096
P_O = 6144
SMALL_O = 128

VMEM_LIMIT = 56 * 1024 * 1024


def _sigmoid(x):
    return 1.0 / (1.0 + jnp.exp(-x))


def _softplus(x):
    return jnp.maximum(x, 0.0) + jnp.log(1.0 + jnp.exp(-jnp.abs(x)))


def _log_sigmoid(x):
    return -_softplus(-x)


def _silu(x):
    return x * _sigmoid(x)


def _rms_rows(x, w):
    return x * lax.rsqrt(jnp.mean(x * x, axis=-1, keepdims=True) + EPS) * w


def _l2n(x):
    return x * lax.rsqrt(jnp.sum(x * x, axis=-1, keepdims=True) + EPS)


def _dot(a, b, precision=None):
    return jnp.dot(a, b, preferred_element_type=F32, precision=precision)


def _dot_nt(a, b, precision=None):
    return lax.dot_general(a, b, (((1,), (1,)), ((), ())), preferred_element_type=F32,
                           precision=precision)


def _dot_tn(a, b, precision=None):
    return lax.dot_general(a, b, (((0,), (0,)), ((), ())), preferred_element_type=F32,
                           precision=precision)


def _eye(n):
    r = lax.broadcasted_iota(jnp.int32, (n, n), 0)
    c = lax.broadcasted_iota(jnp.int32, (n, n), 1)
    return (r == c).astype(F32)


def _transpose(x):
    return _dot_nt(_eye(x.shape[1]), x, precision=HI)


def _tri_masks(n):
    r = lax.broadcasted_iota(jnp.int32, (n, n), 0)
    c = lax.broadcasted_iota(jnp.int32, (n, n), 1)
    return r >= c, r > c


def _mod_kernel(c_ref, w_ref, b_ref, o_ref):
    cs = _silu(c_ref[...]).astype(BF16)
    o_ref[...] = _dot(cs, w_ref[...].astype(BF16)) + b_ref[...]


def _modulation(c_all, ada_w, ada_b):
    depth, d, d3 = ada_w.shape
    rows = c_all.shape[0]
    tn = 1024
    return pl.pallas_call(
        _mod_kernel,
        grid=(depth, d3 // tn),
        in_specs=[
            pl.BlockSpec((rows, d), lambda l, j: (0, 0)),
            pl.BlockSpec((None, d, tn), lambda l, j: (l, 0, j)),
            pl.BlockSpec((None, 1, tn), lambda l, j: (l, 0, j)),
        ],
        out_specs=pl.BlockSpec((None, rows, tn), lambda l, j: (l, 0, j)),
        out_shape=jax.ShapeDtypeStruct((depth, rows, d3), F32),
        compiler_params=pltpu.CompilerParams(
            dimension_semantics=("arbitrary", "arbitrary"), vmem_limit_bytes=VMEM_LIMIT),
        name="modulation",
    )(c_all, ada_w, ada_b.reshape(depth, 1, d3))


def _inproj_kernel(x_ref, mod_ref, nw_ref, w_ref, ws_ref, o_ref, os_ref, h_ref):
    @pl.when(pl.program_id(1) == 0)
    def _():
        x = x_ref[...]
        y = _rms_rows(x, nw_ref[...])
        shift = mod_ref[:, 0:D_MODEL]
        scale = mod_ref[:, D_MODEL:2 * D_MODEL]
        h_ref[...] = (y * (1.0 + scale) + shift).astype(BF16)
        os_ref[...] = _dot(h_ref[...], ws_ref[...])

    o_ref[...] = _dot(h_ref[...], w_ref[...]).astype(o_ref.dtype)


def _inproj(x, mod, norm_w, w, w_small, tm, tn):
    n, d = x.shape
    p = w.shape[1]
    ps = w_small.shape[1]
    g, r, _ = mod.shape
    tiles_per_group = (n // g) // tm
    return pl.pallas_call(
        _inproj_kernel,
        grid=(n // tm, p // tn),
        in_specs=[
            pl.BlockSpec((tm, d), lambda i, j: (i, 0)),
            pl.BlockSpec((None, r, 3 * d), lambda i, j: (i // tiles_per_group, 0, 0)),
            pl.BlockSpec((1, d), lambda i, j: (0, 0)),
            pl.BlockSpec((d, tn), lambda i, j: (0, j)),
            pl.BlockSpec((d, ps), lambda i, j: (0, 0)),
        ],
        out_specs=[pl.BlockSpec((tm, tn), lambda i, j: (i, j)),
                   pl.BlockSpec((tm, ps), lambda i, j: (i, 0))],
        out_shape=[jax.ShapeDtypeStruct((n, p), BF16), jax.ShapeDtypeStruct((n, ps), F32)],
        scratch_shapes=[pltpu.VMEM((tm, d), BF16)],
        compiler_params=pltpu.CompilerParams(
            dimension_semantics=("arbitrary", "arbitrary"), vmem_limit_bytes=VMEM_LIMIT),
        name="inproj",
    )(x, mod, norm_w.reshape(1, d), w, w_small)


def _outproj_kernel(y_ref, x_ref, mod_ref, w_ref, fw_ref, o_ref, *, final):
    acc = _dot(y_ref[...].astype(BF16), w_ref[...])
    gate = mod_ref[:, 2 * D_MODEL:3 * D_MODEL]
    xn = x_ref[...] + gate * acc
    if final:
        xn = _rms_rows(xn, fw_ref[...])
    o_ref[...] = xn


def _outproj(y, x, mod, w, final_w, tm, final):
    n, d = x.shape
    k = y.shape[1]
    g, r, _ = mod.shape
    tiles_per_group = (n // g) // tm
    return pl.pallas_call(
        functools.partial(_outproj_kernel, final=final),
        grid=(n // tm,),
        in_specs=[
            pl.BlockSpec((tm, k), lambda i: (i, 0)),
            pl.BlockSpec((tm, d), lambda i: (i, 0)),
            pl.BlockSpec((None, r, 3 * d), lambda i: (i // tiles_per_group, 0, 0)),
            pl.BlockSpec((k, d), lambda i: (0, 0)),
            pl.BlockSpec((1, d), lambda i: (0, 0)),
        ],
        out_specs=pl.BlockSpec((tm, d), lambda i: (i, 0)),
        out_shape=jax.ShapeDtypeStruct((n, d), F32),
        compiler_params=pltpu.CompilerParams(
            dimension_semantics=("arbitrary",), vmem_limit_bytes=VMEM_LIMIT),
        name="outproj",
    )(y, x, mod, w, final_w.reshape(1, d))


def _neumann_correction(a_list, n):
    ys = [-a for a in a_list]
    pbs = [a.astype(BF16) for a in a_list]
    for _ in range(int(math.log2(n)) - 1):
        for i in range(len(ys)):
            p = _dot(pbs[i], pbs[i])
            pbs[i] = p.astype(BF16)
            ys[i] = ys[i] + p + _dot(ys[i].astype(BF16), pbs[i])
    return ys


def _even_prompt_kernel(proj_ref, sm_ref, sp_ref, cw_ref, gnw_ref, mnw_ref,
                        y_ref, gdn_ref, conv_ref, mc_ref, mn_ref, mm_ref,
                        ubuf_ref, qkv_ref):
    L = CHUNK
    c = pl.program_id(1)

    @pl.when(c == 0)
    def _():
        gdn_ref[...] = jnp.zeros_like(gdn_ref)
        mc_ref[...] = jnp.zeros_like(mc_ref)
        mn_ref[...] = jnp.zeros_like(mn_ref)
        mm_ref[...] = jnp.zeros_like(mm_ref)
        ubuf_ref[0:8, :] = jnp.zeros((8, CONV_CH), F32)

    tri_incl, tri_strict = _tri_masks(L)
    tri_f = tri_incl.astype(F32)

    u = proj_ref[:, E_U:E_U + CONV_CH].astype(F32)
    ubuf_ref[8:8 + L, :] = u
    cw = cw_ref[...]
    uc = (cw[0:1] * ubuf_ref[5:5 + L, :] + cw[1:2] * ubuf_ref[6:6 + L, :]
          + cw[2:3] * ubuf_ref[7:7 + L, :] + cw[3:4] * u)
    ubuf_ref[0:8, :] = u[L - 8:L, :]
    conv_ref[...] = u[L - 3:L, :]
    qkv_ref[...] = _silu(uc)

    sm = sm_ref[...]
    head_lane = lax.broadcasted_iota(jnp.int32, (L, LANE), 1) < H_A
    beta = _sigmoid(sm[:, 0:128])
    g = -jnp.exp(sp_ref[1:2, :]) * _softplus(sm[:, 128:256] + sp_ref[0:1, :])
    ig = sm[:, 256:384] + sp_ref[2:3, :]
    lf = _log_sigmoid(sm[:, 384:512] + sp_ref[3:4, :])
    packed = jnp.where(head_lane, g, 0.0) + pltpu.roll(jnp.where(head_lane, lf, 0.0), H_A, 1)
    cum = _dot(tri_f, packed, precision=HI)
    cum_t = _transpose(cum + pltpu.roll(jnp.where(head_lane, ig, 0.0), H_A + H_B, 1))

    gnw = gnw_ref[...]
    mnw = mnw_ref[...]

    qkb, kf, decay, beta_c, gam, bh_c = [], [], [], [], [], []
    for h in range(H_A):
        q = _l2n(qkv_ref[:, h * DK_A:(h + 1) * DK_A]) * (DK_A ** -0.5)
        k = _l2n(qkv_ref[:, H_A * DK_A + h * DK_A:H_A * DK_A + (h + 1) * DK_A])
        kf.append(k)
        qkb.append(jnp.concatenate([q, k], axis=0).astype(BF16))
    qkk = [_dot_nt(qkb[h], qkb[h][L:2 * L]) for h in range(H_A)]
    a_list = []
    for h in range(H_A):
        bh_c.append(cum[:, h:h + 1])
        diff = bh_c[h] - cum_t[h:h + 1, :]
        decay.append(jnp.where(tri_incl, jnp.exp(jnp.where(tri_incl, diff, 0.0)), 0.0))
        beta_c.append(beta[:, h:h + 1])
        gam.append(jnp.exp(bh_c[h]))
        a_list.append(jnp.where(tri_strict, decay[h] * qkk[h][L:2 * L], 0.0) * beta_c[h])
    corr = _neumann_correction(a_list, L)

    for h in range(H_A):
        v = qkv_ref[:, 2 * H_A * DK_A + h * DV_A:2 * H_A * DK_A + (h + 1) * DV_A]
        s = gdn_ref[h]
        qks = _dot(qkb[h], s.astype(BF16))
        rhs = beta_c[h] * (v - gam[h] * qks[L:2 * L])
        uu = rhs + _dot(corr[h].astype(BF16), rhs.astype(BF16))
        ub = uu.astype(BF16)
        o = gam[h] * qks[0:L] + _dot((qkk[h][0:L] * decay[h]).astype(BF16), ub)
        last = bh_c[h][L - 1:L, :]
        kw = (kf[h] * jnp.exp(last - bh_c[h])).astype(BF16)
        gdn_ref[h] = jnp.exp(last) * s + _dot_tn(kw, ub)
        z = proj_ref[:, E_Z + h * DV_A:E_Z + (h + 1) * DV_A].astype(F32)
        y_ref[:, h * DV_A:(h + 1) * DV_A] = (_rms_rows(o, gnw) * _silu(z)).astype(BF16)

    qs, qbs, kbs = [], [], []
    for h in range(H_B):
        qk_in = proj_ref[:, E_QK + h * 128:E_QK + (h + 1) * 128]
        qbs.append(qk_in[:, 0:DK_B])
        qs.append(qbs[h].astype(F32))
        kbs.append((qk_in[:, DK_B:2 * DK_B].astype(F32) * (DK_B ** -0.5)).astype(BF16))
    qkm = [_dot_nt(qbs[h], kbs[h]) for h in range(H_B)]
    for h in range(H_B):
        q = qs[h]
        qb = qbs[h]
        k = kbs[h].astype(F32)
        vb = proj_ref[:, E_V + h * DV_B:E_V + (h + 1) * DV_B]
        b_c = cum[:, H_A + h:H_A + h + 1]
        b_r = cum_t[H_A + h:H_A + h + 1, :]
        ih_r = cum_t[H_A + H_B + h:H_A + H_B + h + 1, :]
        ih_c = ig[:, h:h + 1]
        m0 = mm_ref[0:1, h:h + 1]
        dmat = jnp.where(tri_incl, b_c - b_r + ih_r, -jnp.inf)
        w0 = b_c + m0
        m_t = jnp.maximum(w0, jnp.max(dmat, axis=-1, keepdims=True))
        p = jnp.exp(dmat - m_t) * qkm[h]
        s0 = jnp.exp(w0 - m_t)
        cmat = mc_ref[h]
        n = mn_ref[h:h + 1, :]
        num = s0 * _dot(qb, cmat.astype(BF16)) + _dot(p.astype(BF16), vb)
        den = s0 * jnp.sum(q * n, axis=-1, keepdims=True) + jnp.sum(p, axis=-1, keepdims=True)
        hh = num / jnp.maximum(jnp.abs(den), jnp.exp(-m_t))
        m_end = m_t[L - 1:L, :]
        b_last = b_c[L - 1:L, :]
        we = jnp.exp(b_last - b_c + ih_c - m_end)
        se = jnp.exp(b_last + m0 - m_end)
        kw = k * we
        mc_ref[h] = se * cmat + _dot_tn(kw.astype(BF16), vb)
        mn_ref[h:h + 1, :] = se * n + jnp.sum(kw, axis=0, keepdims=True)
        mm_ref[0:1, h:h + 1] = m_end
        o_pre = proj_ref[:, E_OP + h * DV_B:E_OP + (h + 1) * DV_B].astype(F32)
        z = proj_ref[:, E_Z + H_A * DV_A + h * DV_B:
                     E_Z + H_A * DV_A + (h + 1) * DV_B].astype(F32)
        y_ref[:, H_A * DV_A + h * DV_B:H_A * DV_A + (h + 1) * DV_B] = (
            _sigmoid(o_pre) * _rms_rows(hh, mnw) * _silu(z)).astype(BF16)


def _even_prompt(proj, small, sp, conv_w, gnw, mnw, batch, seq):
    nc = seq // CHUNK
    row = lambda b, c: (b * nc + c, 0)
    const2 = lambda b, c: (0, 0)
    return pl.pallas_call(
        _even_prompt_kernel,
        grid=(batch, nc),
        in_specs=[
            pl.BlockSpec((CHUNK, P_E), row),
            pl.BlockSpec((CHUNK, SMALL_E), row),
            pl.BlockSpec((8, LANE), const2),
            pl.BlockSpec((CONV_W, CONV_CH), const2),
            pl.BlockSpec((1, DV_A), const2),
            pl.BlockSpec((1, DV_B), const2),
        ],
        out_specs=[
            pl.BlockSpec((CHUNK, W_MIX), row),
            pl.BlockSpec((None, H_A, DK_A, DV_A), lambda b, c: (b, 0, 0, 0)),
            pl.BlockSpec((None, CONV_W - 1, CONV_CH), lambda b, c: (b, 0, 0)),
            pl.BlockSpec((None, H_B, DK_B, DV_B), lambda b, c: (b, 0, 0, 0)),
            pl.BlockSpec((None, H_B, DK_B), lambda b, c: (b, 0, 0)),
            pl.BlockSpec((None, 1, LANE), lambda b, c: (b, 0, 0)),
        ],
        out_shape=[
            jax.ShapeDtypeStruct((batch * seq, W_MIX), BF16),
            jax.ShapeDtypeStruct((batch, H_A, DK_A, DV_A), F32),
            jax.ShapeDtypeStruct((batch, CONV_W - 1, CONV_CH), F32),
            jax.ShapeDtypeStruct((batch, H_B, DK_B, DV_B), F32),
            jax.ShapeDtypeStruct((batch, H_B, DK_B), F32),
            jax.ShapeDtypeStruct((batch, 1, LANE), F32),
        ],
        scratch_shapes=[pltpu.VMEM((8 + CHUNK, CONV_CH), F32), pltpu.VMEM((CHUNK, CONV_CH), F32)],
        compiler_params=pltpu.CompilerParams(
            dimension_semantics=("arbitrary", "arbitrary"), vmem_limit_bytes=VMEM_LIMIT),
        name="even_prompt",
    )(proj, small, sp, conv_w, gnw, mnw)


def _ret_log_gamma(h):
    return math.log(1.0 - 2.0 ** (-5.0 - h))


def _odd_prompt_kernel(proj_ref, glr_ref, w2_ref, b2_ref, cc_ref, ss_ref, cnw_ref, dnw_ref,
                       y_ref, gla_ref, ret_ref, st_ref, b_ref, oc_ref):
    L = CHUNK
    LC = CHUNK_C
    c = pl.program_id(1)
    nc = pl.num_programs(1)

    @pl.when(c == 0)
    def _():
        st_ref[...] = jnp.zeros_like(st_ref)
        ret_ref[...] = jnp.zeros_like(ret_ref)

    tri_incl, _ = _tri_masks(L)
    r64 = lax.broadcasted_iota(jnp.int32, (L, L), 0)
    c64 = lax.broadcasted_iota(jnp.int32, (L, L), 1)
    sub = int(math.log2(LC))
    blocktri = jnp.logical_and(r64 >= c64, (r64 >> sub) == (c64 >> sub)).astype(F32)
    tri16, _ = _tri_masks(LC)
    lane16 = lax.broadcasted_iota(jnp.int32, (LC, LC), 1)

    glr = glr_ref[...].astype(BF16)
    gc = _log_sigmoid(_dot(glr, w2_ref[...]) + b2_ref[...]) / GLA_TAU
    b_ref[...] = _dot(blocktri, gc, precision=HI)

    cnw = cnw_ref[...]
    dnw = dnw_ref[...]

    for h in range(H_C):
        for j in range(L // LC):
            rows = slice(j * LC, (j + 1) * LC)
            q = proj_ref[rows, O_QC + h * DK_C:O_QC + (h + 1) * DK_C].astype(F32) * (DK_C ** -0.5)
            k = proj_ref[rows, O_KC + h * DK_C:O_KC + (h + 1) * DK_C].astype(F32)
            v = proj_ref[rows, O_VC + h * DV_C:O_VC + (h + 1) * DV_C]
            b = b_ref[rows, h * DK_C:(h + 1) * DK_C]
            att = jnp.zeros((LC, LC), F32)
            for s_i in range(LC):
                w = q * k[s_i:s_i + 1, :] * jnp.exp(jnp.minimum(b - b[s_i:s_i + 1, :], 0.0))
                col = jnp.sum(w, axis=-1, keepdims=True)
                att = jnp.where(lane16 == s_i, col, att)
            att = jnp.where(tri16, att, 0.0)
            st = st_ref[h]
            o = _dot_nt((q * jnp.exp(b)).astype(BF16), st.astype(BF16)) + _dot(att.astype(BF16), v)
            oc_ref[rows, :] = o
            bl = b[LC - 1:LC, :]
            kw = (k * jnp.exp(bl - b)).astype(BF16)
            st_ref[h] = st * jnp.exp(bl) + _dot_tn(v, kw)
        z = proj_ref[:, O_Z + h * DV_C:O_Z + (h + 1) * DV_C].astype(F32)
        y_ref[:, h * DV_C:(h + 1) * DV_C] = (_rms_rows(oc_ref[...], cnw) * _silu(z)).astype(BF16)

    @pl.when(c == nc - 1)
    def _():
        for h in range(H_C):
            st = st_ref[h]
            gla_ref[h, :, 0:128] = _transpose(st[0:128, :])
            gla_ref[h, :, 128:256] = _transpose(st[128:256, :])

    cc = cc_ref[...]
    ss = ss_ref[...]
    rel = (r64 - c64).astype(F32)
    tcol = lax.broadcasted_iota(jnp.int32, (L, 1), 0).astype(F32)
    for h in range(H_D):
        lg = _ret_log_gamma(h)
        dm = jnp.where(tri_incl, jnp.exp(lg * jnp.maximum(rel, 0.0)), 0.0)
        inner = jnp.exp(lg * (tcol + 1.0))
        end = jnp.exp(lg * (L - 1.0 - tcol))
        q = proj_ref[:, O_QD + h * DK_D:O_QD + (h + 1) * DK_D].astype(F32)
        k = proj_ref[:, O_KD + h * DK_D:O_KD + (h + 1) * DK_D].astype(F32)
        q = q * cc + pltpu.roll(q, DK_D // 2, 1) * ss
        k = (k * cc + pltpu.roll(k, DK_D // 2, 1) * ss) * (DK_D ** -0.5)
        v = proj_ref[:, O_VD + h * DV_D:O_VD + (h + 1) * DV_D]
        qb = q.astype(BF16)
        kb = k.astype(BF16)
        s = ret_ref[h]
        att = _dot_nt(qb, kb) * dm
        o = _dot(qb, s.astype(BF16)) * inner + _dot(att.astype(BF16), v)
        ret_ref[h] = math.exp(lg * L) * s + _dot_tn((k * end).astype(BF16), v)
        z = proj_ref[:, O_Z + H_C * DV_C + h * DV_D:
                     O_Z + H_C * DV_C + (h + 1) * DV_D].astype(F32)
        y_ref[:, H_C * DV_C + h * DV_D:H_C * DV_C + (h + 1) * DV_D] = (
            _rms_rows(o, dnw) * _silu(z)).astype(BF16)


def _odd_prompt(proj, glr, w2, b2, cc, ss, cnw, dnw, batch, seq):
    nc = seq // CHUNK
    row = lambda b, c: (b * nc + c, 0)
    const2 = lambda b, c: (0, 0)
    return pl.pallas_call(
        _odd_prompt_kernel,
        grid=(batch, nc),
        in_specs=[
            pl.BlockSpec((CHUNK, P_O), row),
            pl.BlockSpec((CHUNK, SMALL_O), row),
            pl.BlockSpec((LANE, H_C * DK_C), const2),
            pl.BlockSpec((1, H_C * DK_C), const2),
            pl.BlockSpec((CHUNK, DK_D), lambda b, c: (c, 0)),
            pl.BlockSpec((CHUNK, DK_D), lambda b, c: (c, 0)),
            pl.BlockSpec((1, DV_C), const2),
            pl.BlockSpec((1, DV_D), const2),
        ],
        out_specs=[
            pl.BlockSpec((CHUNK, W_MIX), row),
            pl.BlockSpec((None, H_C, DK_C, DV_C), lambda b, c: (b, 0, 0, 0)),
            pl.BlockSpec((None, H_D, DK_D, DV_D), lambda b, c: (b, 0, 0, 0)),
        ],
        out_shape=[
            jax.ShapeDtypeStruct((batch * seq, W_MIX), BF16),
            jax.ShapeDtypeStruct((batch, H_C, DK_C, DV_C), F32),
            jax.ShapeDtypeStruct((batch, H_D, DK_D, DV_D), F32),
        ],
        scratch_shapes=[pltpu.VMEM((H_C, DV_C, DK_C), F32), pltpu.VMEM((CHUNK, H_C * DK_C), F32),
                        pltpu.VMEM((CHUNK, DV_C), F32)],
        compiler_params=pltpu.CompilerParams(
            dimension_semantics=("arbitrary", "arbitrary"), vmem_limit_bytes=VMEM_LIMIT),
        name="odd_prompt",
    )(proj, glr, w2, b2, cc, ss, cnw, dnw)


DEC_BB = 8


def _group_sum_matrix(groups):
    r = lax.broadcasted_iota(jnp.int32, (groups, 8 * groups), 0)
    c = lax.broadcasted_iota(jnp.int32, (groups, 8 * groups), 1)
    return ((c >> 3) == r).astype(F32)


def _sublane_group_sums(parts, gsum):
    folded = [p.reshape(p.shape[0] // 8, 8, p.shape[1]).sum(axis=0) for p in parts]
    return _dot(gsum, jnp.concatenate(folded, axis=0), precision=HI)


def _even_decode_kernel(u_ref, sm_ref, qk_ref, v_ref, op_ref, z_ref,
                        conv_ref, gdn_ref, mc_ref, mn_ref, mm_ref,
                        sp_ref, cw_ref, gnw_ref, mnw_ref,
                        y_ref, conv_o, gdn_o, mc_o, mn_o, mm_o):
    gnw = gnw_ref[...]
    mnw = mnw_ref[...]
    gsum = _group_sum_matrix(H_A)

    def body(b, carry):
        sm = sm_ref[pl.ds(b, 1), :]
        beta = _sigmoid(sm[:, 0:128])
        gam = jnp.exp(-jnp.exp(sp_ref[1:2, :]) * _softplus(sm[:, 128:256] + sp_ref[0:1, :]))
        ig = sm[:, 256:384] + sp_ref[2:3, :]
        lf = _log_sigmoid(sm[:, 384:512] + sp_ref[3:4, :])
        m0 = mm_ref[pl.ds(b, 1), :]
        w0 = lf + m0
        m_t = jnp.maximum(w0, ig)
        s0 = jnp.exp(w0 - m_t)
        we = jnp.exp(ig - m_t)
        floor = jnp.exp(-m_t)
        mm_o[pl.ds(b, 1), :] = m_t

        u = u_ref[b]
        cs = conv_ref[b]
        cw = cw_ref[...]
        uc = cw[0] * cs[0] + cw[1] * cs[1] + cw[2] * cs[2] + cw[3] * u
        conv_o[b, 0] = cs[1]
        conv_o[b, 1] = cs[2]
        conv_o[b, 2] = u
        act = _silu(uc)
        q = _l2n(act[0:8]) * (DK_A ** -0.5)
        k = _l2n(act[8:16])
        v = act[16:24]
        qt = _transpose(q)
        kt = _transpose(k)
        gcols = _transpose(jnp.concatenate(
            [beta, gam, s0, we, floor, jnp.zeros((3, LANE), F32)], axis=0))[0:8, :]
        beta_c, gam_c, s0_c, we_c, floor_c = (gcols[:, i:i + 1] for i in range(5))
        z = z_ref[b]
        s_old = [gdn_ref[b, h] for h in range(H_A)]
        ks = _sublane_group_sums([kt[:, h:h + 1] * s_old[h] for h in range(H_A)], gsum)
        uu = beta_c * (v - gam_c * ks)
        parts = []
        for h in range(H_A):
            sn = gam[:, h:h + 1] * s_old[h] + kt[:, h:h + 1] * uu[h:h + 1, :]
            gdn_o[b, h] = sn
            parts.append(qt[:, h:h + 1] * sn)
        o = _sublane_group_sums(parts, gsum)
        y_ref[b, 0:H_A, :] = _rms_rows(o, gnw) * _silu(z[0:H_A, :])

        qk = qk_ref[b]
        qkt = _transpose(qk)
        q_rows = qk[:, 0:DK_B]
        k_rows = qk[:, DK_B:2 * DK_B] * (DK_B ** -0.5)
        wv = we_c * v_ref[b]
        nn = s0_c * mn_ref[b] + we_c * k_rows
        mn_o[b] = nn
        parts = []
        for h in range(H_B):
            kc = qkt[DK_B:2 * DK_B, h:h + 1] * (DK_B ** -0.5)
            cn = s0[:, h:h + 1] * mc_ref[b, h] + kc * wv[h:h + 1, :]
            mc_o[b, h] = cn
            parts.append(qkt[0:DK_B, h:h + 1] * cn)
        num = _sublane_group_sums(parts, gsum)
        den = jnp.sum(q_rows * nn, axis=-1, keepdims=True)
        hh = num / jnp.maximum(jnp.abs(den), floor_c)
        y_ref[b, H_A:H_A + H_B, :] = (
            _sigmoid(op_ref[b]) * _rms_rows(hh, mnw) * _silu(z[H_A:H_A + H_B, :]))
        return carry

    lax.fori_loop(0, DEC_BB, body, 0, unroll=2)


def _even_decode(u, sm, qk, v, op, z, conv, gdn, mc, mn, mm, sp, cw, gnw, mnw):
    batch = u.shape[0]
    bb = DEC_BB
    i3 = lambda i: (i, 0, 0)
    i4 = lambda i: (i, 0, 0, 0)
    c2 = lambda i: (0, 0)
    return pl.pallas_call(
        _even_decode_kernel,
        grid=(batch // bb,),
        in_specs=[
            pl.BlockSpec((bb, 24, LANE), i3),
            pl.BlockSpec((bb, 512), lambda i: (i, 0)),
            pl.BlockSpec((bb, H_B, LANE), i3),
            pl.BlockSpec((bb, H_B, DV_B), i3),
            pl.BlockSpec((bb, H_B, DV_B), i3),
            pl.BlockSpec((bb, 16, LANE), i3),
            pl.BlockSpec((bb, CONV_W - 1, 24, LANE), i4),
            pl.BlockSpec((bb, H_A, DK_A, DV_A), i4),
            pl.BlockSpec((bb, H_B, DK_B, DV_B), i4),
            pl.BlockSpec((bb, H_B, DK_B), i3),
            pl.BlockSpec((bb, LANE), lambda i: (i, 0)),
            pl.BlockSpec((8, LANE), c2),
            pl.BlockSpec((CONV_W, 24, LANE), lambda i: (0, 0, 0)),
            pl.BlockSpec((1, DV_A), c2),
            pl.BlockSpec((1, DV_B), c2),
        ],
        out_specs=[
            pl.BlockSpec((bb, 16, LANE), i3),
            pl.BlockSpec((bb, CONV_W - 1, 24, LANE), i4),
            pl.BlockSpec((bb, H_A, DK_A, DV_A), i4),
            pl.BlockSpec((bb, H_B, DK_B, DV_B), i4),
            pl.BlockSpec((bb, H_B, DK_B), i3),
            pl.BlockSpec((bb, LANE), lambda i: (i, 0)),
        ],
        out_shape=[
            jax.ShapeDtypeStruct((batch, 16, LANE), F32),
            jax.ShapeDtypeStruct((batch, CONV_W - 1, 24, LANE), F32),
            jax.ShapeDtypeStruct((batch, H_A, DK_A, DV_A), F32),
            jax.ShapeDtypeStruct((batch, H_B, DK_B, DV_B), F32),
            jax.ShapeDtypeStruct((batch, H_B, DK_B), F32),
            jax.ShapeDtypeStruct((batch, LANE), F32),
        ],
        compiler_params=pltpu.CompilerParams(
            dimension_semantics=("arbitrary",), vmem_limit_bytes=VMEM_LIMIT),
        name="even_decode",
    )(u, sm, qk, v, op, z, conv, gdn, mc, mn, mm, sp, cw, gnw, mnw)


def _gla_gate_kernel(glr_ref, w2_ref, b2_ref, o_ref):
    o_ref[...] = _log_sigmoid(_dot(glr_ref[...].astype(BF16), w2_ref[...]) + b2_ref[...]) / GLA_TAU


def _gla_gate(glr, w2, b2):
    n = glr.shape[0]
    return pl.pallas_call(
        _gla_gate_kernel,
        out_shape=jax.ShapeDtypeStruct((n, H_C * DK_C), F32),
        name="gla_gate",
    )(glr, w2, b2)


def _odd_decode_kernel(qc_ref, kc_ref, vc_ref, g_ref, qd_ref, kd_ref, vd_ref, z_ref,
                       gla_ref, ret_ref, cc_ref, ss_ref, cnw_ref, dnw_ref,
                       y_ref, gla_o, ret_o):
    head_row = lax.broadcasted_iota(jnp.int32, (H_C + H_D, DV_C), 0)
    nw = jnp.where(head_row < H_C, cnw_ref[...], dnw_ref[...])
    cc = cc_ref[...]
    ss = ss_ref[...]
    gsum = _group_sum_matrix(H_C + H_D)

    def body(b, carry):
        q = qc_ref[b] * (DK_C ** -0.5)
        k = kc_ref[b]
        eg = jnp.exp(g_ref[b])
        t1 = _transpose(jnp.concatenate([q, k], axis=0))
        t2 = _transpose(jnp.concatenate([eg, eg], axis=0))
        v = vc_ref[b]
        parts = []
        for h in range(H_C):
            sn = t2[:, h:h + 1] * gla_ref[b, h] + t1[:, H_C + h:H_C + h + 1] * v[h:h + 1, :]
            gla_o[b, h] = sn
            parts.append(t1[:, h:h + 1] * sn)
        qd = qd_ref[b]
        kd = kd_ref[b]
        qd = qd * cc + pltpu.roll(qd, DK_D // 2, 1) * ss
        kd = (kd * cc + pltpu.roll(kd, DK_D // 2, 1) * ss) * (DK_D ** -0.5)
        t3 = _transpose(jnp.concatenate([qd, kd], axis=0))
        vd = vd_ref[b]
        for h in range(H_D):
            gamma = math.exp(_ret_log_gamma(h))
            sn = gamma * ret_ref[b, h] + t3[:, H_D + h:H_D + h + 1] * vd[h:h + 1, :]
            ret_o[b, h] = sn
            parts.append(t3[:, h:h + 1] * sn)
        o = _sublane_group_sums(parts, gsum)
        y_ref[b] = _rms_rows(o, nw) * _silu(z_ref[b])
        return carry

    lax.fori_loop(0, DEC_BB, body, 0, unroll=2)


def _odd_decode(qc, kc, vc, g, qd, kd, vd, z, gla, ret, cc, ss, cnw, dnw):
    batch = qc.shape[0]
    bb = DEC_BB
    i3 = lambda i: (i, 0, 0)
    i4 = lambda i: (i, 0, 0, 0)
    c2 = lambda i: (0, 0)
    return pl.pallas_call(
        _odd_decode_kernel,
        grid=(batch // bb,),
        in_specs=[
            pl.BlockSpec((bb, H_C, DK_C), i3),
            pl.BlockSpec((bb, H_C, DK_C), i3),
            pl.BlockSpec((bb, H_C, DV_C), i3),
            pl.BlockSpec((bb, H_C, DK_C), i3),
            pl.BlockSpec((bb, H_D, DK_D), i3),
            pl.BlockSpec((bb, H_D, DK_D), i3),
            pl.BlockSpec((bb, H_D, DV_D), i3),
            pl.BlockSpec((bb, 8, DV_C), i3),
            pl.BlockSpec((bb, H_C, DK_C, DV_C), i4),
            pl.BlockSpec((bb, H_D, DK_D, DV_D), i4),
            pl.BlockSpec((1, DK_D), c2),
            pl.BlockSpec((1, DK_D), c2),
            pl.BlockSpec((1, DV_C), c2),
            pl.BlockSpec((1, DV_D), c2),
        ],
        out_specs=[
            pl.BlockSpec((bb, 8, DV_C), i3),
            pl.BlockSpec((bb, H_C, DK_C, DV_C), i4),
            pl.BlockSpec((bb, H_D, DK_D, DV_D), i4),
        ],
        out_shape=[
            jax.ShapeDtypeStruct((batch, 8, DV_C), F32),
            jax.ShapeDtypeStruct((batch, H_C, DK_C, DV_C), F32),
            jax.ShapeDtypeStruct((batch, H_D, DK_D, DV_D), F32),
        ],
        compiler_params=pltpu.CompilerParams(
            dimension_semantics=("arbitrary",), vmem_limit_bytes=VMEM_LIMIT),
        name="odd_decode",
    )(qc, kc, vc, g, qd, kd, vd, z, gla, ret, cc, ss, cnw, dnw)


def _pad_cols(w, width):
    return jnp.pad(w, ((0, 0), (0, width - w.shape[1])))


def _prep_even_w_in(w):
    u = w[:, 0:3072]
    beta = w[:, 3072:3080]
    a = w[:, 3080:3088]
    qb = w[:, 3088:3600].reshape(-1, H_B, DK_B)
    kb = w[:, 3600:4112].reshape(-1, H_B, DK_B)
    vb = w[:, 4112:5136]
    ig = w[:, 5136:5144]
    fg = w[:, 5144:5152]
    op = w[:, 5152:6176]
    z = w[:, 6176:8224]
    qk = jnp.concatenate([qb, kb], axis=-1).reshape(-1, H_B * 2 * DK_B)
    small = jnp.concatenate([_pad_cols(beta, 128), _pad_cols(a, 128), _pad_cols(ig, 128),
                             _pad_cols(fg, 128)], axis=1)
    return jnp.concatenate([u, qk, vb, op, z], axis=1).astype(BF16), small.astype(BF16)


def _prep_odd_w_in(w):
    qc = w[:, 0:512]
    kc = w[:, 512:1024]
    vc = w[:, 1024:2048]
    glr = w[:, 2048:2064]
    qd = w[:, 2064:2576]
    kd = w[:, 2576:3088]
    vd = w[:, 3088:4112]
    z = w[:, 4112:6160]
    return (jnp.concatenate([qc, kc, vc, qd, kd, vd, z], axis=1).astype(BF16),
            _pad_cols(glr, SMALL_O).astype(BF16))


def _lane_row(vec, width=LANE):
    return jnp.pad(vec.astype(F32), (0, width - vec.shape[0]))


def _rotary_tables(pos):
    half = DK_D // 2
    inv = ROPE_BASE ** (-jnp.arange(half, dtype=F32) / half)
    ang = pos.astype(F32)[:, None] * inv[None, :]
    cos = jnp.cos(ang)
    sin = jnp.sin(ang)
    return jnp.concatenate([cos, cos], axis=-1), jnp.concatenate([-sin, sin], axis=-1)


def kernel(x_prompt, x_sample, c_prompt, c_sample, state_gdn, state_gdn_conv, state_mlstm_c,
           state_mlstm_n, state_mlstm_m, state_gla, state_ret, ada_w, ada_b, norm_w, ev_w_in,
           ev_w_out, gdn_conv_w, gdn_a_log, gdn_dt_bias, gdn_norm_w, mlstm_gate_b, mlstm_norm_w,
           od_w_in, od_w_out, gla_w2, gla_b2, gla_norm_w, ret_norm_w, final_norm_w):
    bp, tp, d = x_prompt.shape
    bs = x_sample.shape[0]
    n_p = bp * tp

    w_in_e, w_sm_e = _prep_even_w_in(ev_w_in[0])
    w_in_o, w_sm_o = _prep_odd_w_in(od_w_in[0])
    w_out_e = ev_w_out[0].astype(BF16)
    w_out_o = od_w_out[0].astype(BF16)
    sp = jnp.stack([_lane_row(gdn_dt_bias[0]), _lane_row(gdn_a_log[0]),
                    _lane_row(mlstm_gate_b[0, :H_B]), _lane_row(mlstm_gate_b[0, H_B:])]
                   + [jnp.zeros((LANE,), F32)] * 4)
    gnw = gdn_norm_w[0].reshape(1, DV_A)
    mnw = mlstm_norm_w[0].reshape(1, DV_B)
    cnw = gla_norm_w[0].reshape(1, DV_C)
    dnw = ret_norm_w[0].reshape(1, DV_D)
    w2 = jnp.pad(gla_w2[0], ((0, LANE - GLA_RANK), (0, 0))).astype(BF16)
    b2 = gla_b2[0].reshape(1, H_C * DK_C)

    mod = _modulation(jnp.concatenate([c_prompt, c_sample], axis=0), ada_w, ada_b)
    mod_p = [mod[l, :bp].reshape(bp, 1, 3 * d) for l in range(2)]
    mod_s = [mod[l, bp:].reshape(1, bs, 3 * d) for l in range(2)]

    xp = x_prompt.reshape(n_p, d)
    proj, small = _inproj(xp, mod_p[0], norm_w[0], w_in_e, w_sm_e, 1024, 2048)
    y, p_gdn, p_conv, p_mc, p_mn, p_mm = _even_prompt(proj, small, sp, gdn_conv_w[0], gnw, mnw,
                                                      bp, tp)
    xp = _outproj(y, xp, mod_p[0], w_out_e, final_norm_w, 512, False)
    proj, small = _inproj(xp, mod_p[1], norm_w[1], w_in_o, w_sm_o, 1024, 2048)
    cc_p, ss_p = _rotary_tables(jnp.arange(tp))
    y, p_gla, p_ret = _odd_prompt(proj, small, w2, b2, cc_p, ss_p, cnw, dnw, bp, tp)
    y_prompt = _outproj(y, xp, mod_p[1], w_out_o, final_norm_w, 512, True).reshape(bp, tp, d)

    xs = x_sample.reshape(bs, d)
    proj, small = _inproj(xs, mod_s[0], norm_w[0], w_in_e, w_sm_e, bs, 2048)
    proj = proj.astype(F32)
    mm_in = jnp.pad(state_mlstm_m[0], ((0, 0), (0, LANE - H_B)))
    y, s_conv, s_gdn, s_mc, s_mn, s_mm = _even_decode(
        proj[:, E_U:E_U + CONV_CH].reshape(bs, 24, LANE),
        small,
        proj[:, E_QK:E_QK + 1024].reshape(bs, H_B, LANE),
        proj[:, E_V:E_V + 1024].reshape(bs, H_B, DV_B),
        proj[:, E_OP:E_OP + 1024].reshape(bs, H_B, DV_B),
        proj[:, E_Z:E_Z + W_MIX].reshape(bs, 16, LANE),
        state_gdn_conv[0].reshape(bs, CONV_W - 1, 24, LANE),
        state_gdn[0], state_mlstm_c[0], state_mlstm_n[0], mm_in,
        sp, gdn_conv_w[0].reshape(CONV_W, 24, LANE), gnw, mnw)
    xs = _outproj(y.reshape(bs, W_MIX), xs, mod_s[0], w_out_e, final_norm_w, bs, False)
    proj, small = _inproj(xs, mod_s[1], norm_w[1], w_in_o, w_sm_o, bs, 2048)
    proj = proj.astype(F32)
    g_s = _gla_gate(small, w2, b2)
    cc_s, ss_s = _rotary_tables(PAST_LEN + jnp.arange(1))
    y, s_gla, s_ret = _odd_decode(
        proj[:, O_QC:O_QC + 512].reshape(bs, H_C, DK_C),
        proj[:, O_KC:O_KC + 512].reshape(bs, H_C, DK_C),
        proj[:, O_VC:O_VC + 1024].reshape(bs, H_C, DV_C),
        g_s.reshape(bs, H_C, DK_C),
        proj[:, O_QD:O_QD + 512].reshape(bs, H_D, DK_D),
        proj[:, O_KD:O_KD + 512].reshape(bs, H_D, DK_D),
        proj[:, O_VD:O_VD + 1024].reshape(bs, H_D, DV_D),
        proj[:, O_Z:O_Z + W_MIX].reshape(bs, 8, DV_C),
        state_gla[0], state_ret[0], cc_s, ss_s, cnw, dnw)
    y_sample = _outproj(y.reshape(bs, W_MIX), xs, mod_s[1], w_out_o, final_norm_w, bs, True)
    y_sample = y_sample.reshape(bs, 1, d)

    return (y_prompt, y_sample,
            p_gdn[None], p_conv[None], p_mc[None], p_mn[None], p_mm[None, :, 0, :H_B],
            p_gla[None], p_ret[None],
            s_gdn[None], s_conv.reshape(1, bs, CONV_W - 1, CONV_CH), s_mc[None], s_mn[None],
            s_mm[None, :, :H_B], s_gla[None], s_ret[None])
```

```python
import functools
import math

import jax
import jax.numpy as jnp
import numpy as np
from jax import lax
from jax.experimental import pallas as pl
from jax.experimental.pallas import tpu as pltpu

F32 = jnp.float32
BF16 = jnp.bfloat16
HI = lax.Precision.HIGHEST

D_MODEL = 1024
W_MIX = 2048
H_A, DK_A, DV_A, CONV_W = 8, 128, 128, 4
CONV_CH = H_A * (2 * DK_A + DV_A)
H_B, DK_B, DV_B = 8, 64, 128
H_C, DK_C, DV_C, GLA_RANK, GLA_TAU = 4, 128, 256, 16, 16.0
H_D, DK_D, DV_D = 4, 128, 256
ROPE_BASE = 10000.0
PAST_LEN = 16384
EPS = 1e-6
LANE = 128

CHUNK = 64

E_U, E_QK, E_V, E_OP, E_Z = 0, 3072, 4096, 5120, 6144
P_E = 8192
SMALL_E = 512
O_QC, O_KC, O_VC, O_QD, O_KD, O_VD, O_Z = 0, 512, 1024, 2048, 2560, 3072, 4096
P_O = 6144
SMALL_O = 128

VMEM_LIMIT = 56 * 1024 * 1024


def _sigmoid(x):
    return 0.5 * jnp.tanh(0.5 * x) + 0.5


def _softplus(x):
    return jnp.maximum(x, 0.0) + jnp.log(1.0 + jnp.exp(-jnp.abs(x)))


def _log_sigmoid(x):
    return -_softplus(-x)


def _silu(x):
    half = 0.5 * x
    return half + half * jnp.tanh(half)


def _rms_rows(x, w):
    return x * lax.rsqrt(jnp.mean(x * x, axis=-1, keepdims=True) + EPS) * w


def _l2n(x):
    return x * lax.rsqrt(jnp.sum(x * x, axis=-1, keepdims=True) + EPS)


def _dot(a, b, precision=None):
    return jnp.dot(a, b, preferred_element_type=F32, precision=precision)


def _dot_nt(a, b, precision=None):
    return lax.dot_general(a, b, (((1,), (1,)), ((), ())), preferred_element_type=F32,
                           precision=precision)


def _dot_tn(a, b, precision=None):
    return lax.dot_general(a, b, (((0,), (0,)), ((), ())), preferred_element_type=F32,
                           precision=precision)


def _eye(n):
    r = lax.broadcasted_iota(jnp.int32, (n, n), 0)
    c = lax.broadcasted_iota(jnp.int32, (n, n), 1)
    return (r == c).astype(F32)


def _transpose(x):
    return _dot_nt(_eye(x.shape[1]), x, precision=HI)


def _tri_masks(n):
    r = lax.broadcasted_iota(jnp.int32, (n, n), 0)
    c = lax.broadcasted_iota(jnp.int32, (n, n), 1)
    return r >= c, r > c


def _mod_kernel(c_ref, w_ref, b_ref, o_ref):
    cs = _silu(c_ref[...]).astype(BF16)
    o_ref[...] = _dot(cs, w_ref[...].astype(BF16)) + b_ref[...]


def _modulation(c_all, ada_w, ada_b):
    depth, d, d3 = ada_w.shape
    rows = c_all.shape[0]
    tn = 1024
    return pl.pallas_call(
        _mod_kernel,
        grid=(depth, d3 // tn),
        in_specs=[
            pl.BlockSpec((rows, d), lambda l, j: (0, 0)),
            pl.BlockSpec((None, d, tn), lambda l, j: (l, 0, j)),
            pl.BlockSpec((None, 1, tn), lambda l, j: (l, 0, j)),
        ],
        out_specs=pl.BlockSpec((None, rows, tn), lambda l, j: (l, 0, j)),
        out_shape=jax.ShapeDtypeStruct((depth, rows, d3), F32),
        compiler_params=pltpu.CompilerParams(
            dimension_semantics=("arbitrary", "arbitrary"), vmem_limit_bytes=VMEM_LIMIT),
        name="modulation",
    )(c_all, ada_w, ada_b.reshape(depth, 1, d3))


def _inproj_kernel(x_ref, mod_ref, nw_ref, w_ref, ws_ref, o_ref, os_ref, h_ref):
    @pl.when(pl.program_id(1) == 0)
    def _():
        x = x_ref[...]
        y = _rms_rows(x, nw_ref[...])
        shift = mod_ref[:, 0:D_MODEL]
        scale = mod_ref[:, D_MODEL:2 * D_MODEL]
        h_ref[...] = (y * (1.0 + scale) + shift).astype(BF16)
        os_ref[...] = _dot(h_ref[...], ws_ref[...])

    o_ref[...] = _dot(h_ref[...], w_ref[...]).astype(o_ref.dtype)


def _inproj(x, mod, norm_w, w, w_small, tm, tn):
    n, d = x.shape
    p = w.shape[1]
    ps = w_small.shape[1]
    g, r, _ = mod.shape
    tiles_per_group = (n // g) // tm
    return pl.pallas_call(
        _inproj_kernel,
        grid=(n // tm, p // tn),
        in_specs=[
            pl.BlockSpec((tm, d), lambda i, j: (i, 0)),
            pl.BlockSpec((None, r, 3 * d), lambda i, j: (i // tiles_per_group, 0, 0)),
            pl.BlockSpec((1, d), lambda i, j: (0, 0)),
            pl.BlockSpec((d, tn), lambda i, j: (0, j)),
            pl.BlockSpec((d, ps), lambda i, j: (0, 0)),
        ],
        out_specs=[pl.BlockSpec((tm, tn), lambda i, j: (i, j)),
                   pl.BlockSpec((tm, ps), lambda i, j: (i, 0))],
        out_shape=[jax.ShapeDtypeStruct((n, p), BF16), jax.ShapeDtypeStruct((n, ps), F32)],
        scratch_shapes=[pltpu.VMEM((tm, d), BF16)],
        compiler_params=pltpu.CompilerParams(
            dimension_semantics=("arbitrary", "arbitrary"), vmem_limit_bytes=VMEM_LIMIT),
        name="inproj",
    )(x, mod, norm_w.reshape(1, d), w, w_small)


def _outproj_kernel(y_ref, x_ref, mod_ref, w_ref, fw_ref, o_ref, *, final):
    acc = _dot(y_ref[...].astype(BF16), w_ref[...])
    gate = mod_ref[:, 2 * D_MODEL:3 * D_MODEL]
    xn = x_ref[...] + gate * acc
    if final:
        xn = _rms_rows(xn, fw_ref[...])
    o_ref[...] = xn


def _outproj(y, x, mod, w, final_w, tm, final):
    n, d = x.shape
    k = y.shape[1]
    g, r, _ = mod.shape
    tiles_per_group = (n // g) // tm
    return pl.pallas_call(
        functools.partial(_outproj_kernel, final=final),
        grid=(n // tm,),
        in_specs=[
            pl.BlockSpec((tm, k), lambda i: (i, 0)),
            pl.BlockSpec((tm, d), lambda i: (i, 0)),
            pl.BlockSpec((None, r, 3 * d), lambda i: (i // tiles_per_group, 0, 0)),
            pl.BlockSpec((k, d), lambda i: (0, 0)),
            pl.BlockSpec((1, d), lambda i: (0, 0)),
        ],
        out_specs=pl.BlockSpec((tm, d), lambda i: (i, 0)),
        out_shape=jax.ShapeDtypeStruct((n, d), F32),
        compiler_params=pltpu.CompilerParams(
            dimension_semantics=("arbitrary",), vmem_limit_bytes=VMEM_LIMIT),
        name="outproj",
    )(y, x, mod, w, final_w.reshape(1, d))


def _neumann_correction(a_list, n):
    ys = [-a for a in a_list]
    pbs = [a.astype(BF16) for a in a_list]
    for _ in range(int(math.log2(n)) - 1):
        for i in range(len(ys)):
            p = _dot(pbs[i], pbs[i])
            pbs[i] = p.astype(BF16)
            ys[i] = ys[i] + p + _dot(ys[i].astype(BF16), pbs[i])
    idx = range(len(ys))
    a_h = [a.astype(BF16) for a in a_list]
    a_l = [(a_list[i] - a_h[i].astype(F32)).astype(BF16) for i in idx]
    y_h = [y.astype(BF16) for y in ys]
    y_l = [(ys[i] - y_h[i].astype(F32)).astype(BF16) for i in idx]
    ay = [_dot(a_h[i], y_h[i]) + _dot(a_h[i], y_l[i]) + _dot(a_l[i], y_h[i]) for i in idx]
    e = [-(a_list[i] + ys[i] + ay[i]) for i in idx]
    return [ys[i] + e[i] + _dot(y_h[i], e[i].astype(BF16)) for i in idx]


def _even_prompt_kernel(proj_ref, sm_ref, sp_ref, cw_ref, gnw_ref, mnw_ref,
                        y_ref, gdn_ref, conv_ref, mc_ref, mn_ref, mm_ref,
                        ubuf_ref, qkv_ref):
    L = CHUNK
    c = pl.program_id(1)

    @pl.when(c == 0)
    def _():
        gdn_ref[...] = jnp.zeros_like(gdn_ref)
        mc_ref[...] = jnp.zeros_like(mc_ref)
        mn_ref[...] = jnp.zeros_like(mn_ref)
        mm_ref[...] = jnp.zeros_like(mm_ref)
        ubuf_ref[0:8, :] = jnp.zeros((8, CONV_CH), F32)

    tri_incl, tri_strict = _tri_masks(L)
    tri_f = tri_incl.astype(F32)

    u = proj_ref[:, E_U:E_U + CONV_CH].astype(F32)
    ubuf_ref[8:8 + L, :] = u
    cw = cw_ref[...]
    uc = (cw[0:1] * ubuf_ref[5:5 + L, :] + cw[1:2] * ubuf_ref[6:6 + L, :]
          + cw[2:3] * ubuf_ref[7:7 + L, :] + cw[3:4] * u)
    ubuf_ref[0:8, :] = u[L - 8:L, :]
    conv_ref[...] = u[L - 3:L, :]
    qkv_ref[...] = _silu(uc)

    sm = sm_ref[...]
    head_lane = lax.broadcasted_iota(jnp.int32, (L, LANE), 1) < H_A
    beta = _sigmoid(sm[:, 0:128])
    g = -jnp.exp(sp_ref[1:2, :]) * _softplus(sm[:, 128:256] + sp_ref[0:1, :])
    ig = sm[:, 256:384] + sp_ref[2:3, :]
    lf = _log_sigmoid(sm[:, 384:512] + sp_ref[3:4, :])
    packed = jnp.where(head_lane, g, 0.0) + pltpu.roll(jnp.where(head_lane, lf, 0.0), H_A, 1)
    cum = _dot(tri_f, packed, precision=HI)
    cum_t = _transpose(cum + pltpu.roll(jnp.where(head_lane, ig, 0.0), H_A + H_B, 1))

    gnw = gnw_ref[...]
    mnw = mnw_ref[...]

    qkb, kf, decay, beta_c, gam, bh_c = [], [], [], [], [], []
    for h in range(H_A):
        q = _l2n(qkv_ref[:, h * DK_A:(h + 1) * DK_A]) * (DK_A ** -0.5)
        k = _l2n(qkv_ref[:, H_A * DK_A + h * DK_A:H_A * DK_A + (h + 1) * DK_A])
        kf.append(k)
        qkb.append(jnp.concatenate([q, k], axis=0).astype(BF16))
    qkk = [_dot_nt(qkb[h], qkb[h][L:2 * L]) for h in range(H_A)]
    a_list = []
    for h in range(H_A):
        bh_c.append(cum[:, h:h + 1])
        diff = bh_c[h] - cum_t[h:h + 1, :]
        decay.append(jnp.where(tri_incl, jnp.exp(jnp.where(tri_incl, diff, 0.0)), 0.0))
        beta_c.append(beta[:, h:h + 1])
        gam.append(jnp.exp(bh_c[h]))
        a_list.append(jnp.where(tri_strict, decay[h] * qkk[h][L:2 * L], 0.0) * beta_c[h])
    corr = _neumann_correction(a_list, L)

    heads = range(H_A)
    s_old = [gdn_ref[h] for h in heads]
    qks = [_dot(qkb[h], s_old[h].astype(BF16)) for h in heads]
    rhs = [beta_c[h] * (qkv_ref[:, 2 * H_A * DK_A + h * DV_A:2 * H_A * DK_A + (h + 1) * DV_A]
                        - gam[h] * qks[h][L:2 * L]) for h in heads]
    ub = [(rhs[h] + _dot(corr[h].astype(BF16), rhs[h].astype(BF16))).astype(BF16) for h in heads]
    o = [gam[h] * qks[h][0:L] + _dot((qkk[h][0:L] * decay[h]).astype(BF16), ub[h])
         for h in heads]
    for h in heads:
        last = bh_c[h][L - 1:L, :]
        kw = (kf[h] * jnp.exp(last - bh_c[h])).astype(BF16)
        gdn_ref[h] = jnp.exp(last) * s_old[h] + _dot_tn(kw, ub[h])
    for h in heads:
        z = proj_ref[:, E_Z + h * DV_A:E_Z + (h + 1) * DV_A].astype(F32)
        y_ref[:, h * DV_A:(h + 1) * DV_A] = (_rms_rows(o[h], gnw) * _silu(z)).astype(BF16)

    heads = range(H_B)
    qs, qbs, kbs, vbs = [], [], [], []
    for h in heads:
        qk_in = proj_ref[:, E_QK + h * 128:E_QK + (h + 1) * 128]
        qbs.append(qk_in[:, 0:DK_B])
        qs.append(qbs[h].astype(F32))
        kbs.append((qk_in[:, DK_B:2 * DK_B].astype(F32) * (DK_B ** -0.5)).astype(BF16))
        vbs.append(proj_ref[:, E_V + h * DV_B:E_V + (h + 1) * DV_B])
    qkm = [_dot_nt(qbs[h], kbs[h]) for h in heads]
    b_c = [cum[:, H_A + h:H_A + h + 1] for h in heads]
    m0 = [mm_ref[0:1, h:h + 1] for h in heads]
    dmat = [jnp.where(tri_incl, b_c[h] - cum_t[H_A + h:H_A + h + 1, :]
                      + cum_t[H_A + H_B + h:H_A + H_B + h + 1, :], -jnp.inf) for h in heads]
    w0 = [b_c[h] + m0[h] for h in heads]
    m_t = [jnp.maximum(w0[h], jnp.max(dmat[h], axis=-1, keepdims=True)) for h in heads]
    p = [jnp.exp(dmat[h] - m_t[h]) * qkm[h] for h in heads]
    s0 = [jnp.exp(w0[h] - m_t[h]) for h in heads]
    c_old = [mc_ref[h] for h in heads]
    n_old = [mn_ref[h:h + 1, :] for h in heads]
    num = [s0[h] * _dot(qbs[h], c_old[h].astype(BF16)) + _dot(p[h].astype(BF16), vbs[h])
           for h in heads]
    den = [s0[h] * jnp.sum(qs[h] * n_old[h], axis=-1, keepdims=True)
           + jnp.sum(p[h], axis=-1, keepdims=True) for h in heads]
    hh = [num[h] / jnp.maximum(jnp.abs(den[h]), jnp.exp(-m_t[h])) for h in heads]
    for h in heads:
        m_end = m_t[h][L - 1:L, :]
        b_last = b_c[h][L - 1:L, :]
        we = jnp.exp(b_last - b_c[h] + ig[:, h:h + 1] - m_end)
        se = jnp.exp(b_last + m0[h] - m_end)
        kw = kbs[h].astype(F32) * we
        mc_ref[h] = se * c_old[h] + _dot_tn(kw.astype(BF16), vbs[h])
        mn_ref[h:h + 1, :] = se * n_old[h] + jnp.sum(kw, axis=0, keepdims=True)
        mm_ref[0:1, h:h + 1] = m_end
    for h in heads:
        o_pre = proj_ref[:, E_OP + h * DV_B:E_OP + (h + 1) * DV_B].astype(F32)
        z = proj_ref[:, E_Z + H_A * DV_A + h * DV_B:
                     E_Z + H_A * DV_A + (h + 1) * DV_B].astype(F32)
        y_ref[:, H_A * DV_A + h * DV_B:H_A * DV_A + (h + 1) * DV_B] = (
            _sigmoid(o_pre) * _rms_rows(hh[h], mnw) * _silu(z)).astype(BF16)


def _even_prompt(proj, small, sp, conv_w, gnw, mnw, batch, seq):
    nc = seq // CHUNK
    row = lambda b, c: (b * nc + c, 0)
    const2 = lambda b, c: (0, 0)
    return pl.pallas_call(
        _even_prompt_kernel,
        grid=(batch, nc),
        in_specs=[
            pl.BlockSpec((CHUNK, P_E), row),
            pl.BlockSpec((CHUNK, SMALL_E), row),
            pl.BlockSpec((8, LANE), const2),
            pl.BlockSpec((CONV_W, CONV_CH), const2),
            pl.BlockSpec((1, DV_A), const2),
            pl.BlockSpec((1, DV_B), const2),
        ],
        out_specs=[
            pl.BlockSpec((CHUNK, W_MIX), row),
            pl.BlockSpec((None, H_A, DK_A, DV_A), lambda b, c: (b, 0, 0, 0)),
            pl.BlockSpec((None, CONV_W - 1, CONV_CH), lambda b, c: (b, 0, 0)),
            pl.BlockSpec((None, H_B, DK_B, DV_B), lambda b, c: (b, 0, 0, 0)),
            pl.BlockSpec((None, H_B, DK_B), lambda b, c: (b, 0, 0)),
            pl.BlockSpec((None, 1, LANE), lambda b, c: (b, 0, 0)),
        ],
        out_shape=[
            jax.ShapeDtypeStruct((batch * seq, W_MIX), BF16),
            jax.ShapeDtypeStruct((batch, H_A, DK_A, DV_A), F32),
            jax.ShapeDtypeStruct((batch, CONV_W - 1, CONV_CH), F32),
            jax.ShapeDtypeStruct((batch, H_B, DK_B, DV_B), F32),
            jax.ShapeDtypeStruct((batch, H_B, DK_B), F32),
            jax.ShapeDtypeStruct((batch, 1, LANE), F32),
        ],
        scratch_shapes=[pltpu.VMEM((8 + CHUNK, CONV_CH), F32), pltpu.VMEM((CHUNK, CONV_CH), F32)],
        compiler_params=pltpu.CompilerParams(
            dimension_semantics=("arbitrary", "arbitrary"), vmem_limit_bytes=VMEM_LIMIT),
        name="even_prompt",
    )(proj, small, sp, conv_w, gnw, mnw)


def _ret_log_gamma(h):
    return math.log(1.0 - 2.0 ** (-5.0 - h))


GLA_LEVELS = int(math.log2(CHUNK))
GLA_MM_LEVELS = (1, 2)
LOG2E = math.log2(math.e)


def _gla_chunk_tables():
    n = CHUNK
    t = np.arange(n)[:, None]
    i = np.arange(n)[None, :]
    blocks = [(i <= t)]
    for lvl in GLA_MM_LEVELS:
        hs = 1 << lvl
        p = (t & ~(2 * hs - 1)) + hs - 1
        upper = (t & hs) != 0
        blocks.append(np.where(upper, (i > p) & (i <= t), (i > t) & (i <= p)))
    sums = np.concatenate(blocks, axis=0).astype(np.float32)
    s = np.arange(n)[None, :]
    level = np.full((n, n), GLA_LEVELS + 1, np.int32)
    level[np.arange(n), np.arange(n)] = GLA_LEVELS
    for lvl in range(GLA_LEVELS):
        hs = 1 << lvl
        same = (t >> (lvl + 1)) == (s >> (lvl + 1))
        level[same & ((t & hs) != 0) & ((s & hs) == 0)] = lvl
    return jnp.asarray(sums), jnp.asarray(level)


def _odd_prompt_kernel(proj_ref, glr_ref, w2_ref, b2_ref, sums_ref, level_ref, cc_ref, ss_ref,
                       cnw_ref, dnw_ref, y_ref, gla_ref, ret_ref, st_ref, x_ref):
    L = CHUNK
    c = pl.program_id(1)
    nc = pl.num_programs(1)

    @pl.when(c == 0)
    def _():
        st_ref[...] = jnp.zeros_like(st_ref)
        ret_ref[...] = jnp.zeros_like(ret_ref)

    tri_incl, _ = _tri_masks(L)
    r64 = lax.broadcasted_iota(jnp.int32, (L, L), 0)
    c64 = lax.broadcasted_iota(jnp.int32, (L, L), 1)
    level = level_ref[...]

    glr = glr_ref[...].astype(BF16)
    gc2 = _log_sigmoid(_dot(glr, w2_ref[...]) + b2_ref[...]) * (LOG2E / GLA_TAU)
    x_ref[...] = _dot(sums_ref[...], gc2, precision=HI)
    odd_row = (lax.broadcasted_iota(jnp.int32, (L, DK_C), 0) & 1) == 1

    def level_exponent(lvl, h):
        cols = slice(h * DK_C, (h + 1) * DK_C)
        if lvl == 0:
            return jnp.where(odd_row, gc2[:, cols], 0.0)
        if lvl in GLA_MM_LEVELS:
            r0 = (1 + GLA_MM_LEVELS.index(lvl)) * L
            return x_ref[r0:r0 + L, cols]
        hs = 1 << lvl
        pieces = []
        for lo in range(0, L, 2 * hs):
            ref = x_ref[lo + hs - 1:lo + hs, cols]
            pieces.append(ref - x_ref[lo:lo + hs, cols])
            pieces.append(x_ref[lo + hs:lo + 2 * hs, cols] - ref)
        return jnp.concatenate(pieces, axis=0)

    cnw = cnw_ref[...]
    dnw = dnw_ref[...]

    heads = range(H_C)
    q = [proj_ref[:, O_QC + h * DK_C:O_QC + (h + 1) * DK_C].astype(F32) * (DK_C ** -0.5)
         for h in heads]
    k = [proj_ref[:, O_KC + h * DK_C:O_KC + (h + 1) * DK_C].astype(F32) for h in heads]
    v = [proj_ref[:, O_VC + h * DV_C:O_VC + (h + 1) * DV_C] for h in heads]
    att = [jnp.where(level == GLA_LEVELS, _dot_nt(q[h].astype(BF16), k[h].astype(BF16)), 0.0)
           for h in heads]
    for lvl in range(GLA_LEVELS):
        for h in heads:
            e = jnp.exp2(level_exponent(lvl, h))
            m = _dot_nt((q[h] * e).astype(BF16), (k[h] * e).astype(BF16))
            att[h] = jnp.where(level == lvl, m, att[h])
    b2 = [x_ref[0:L, h * DK_C:(h + 1) * DK_C] for h in heads]
    st_old = [st_ref[h] for h in heads]
    o = [_dot_nt((q[h] * jnp.exp2(b2[h])).astype(BF16), st_old[h].astype(BF16))
         + _dot(att[h].astype(BF16), v[h]) for h in heads]
    for h in heads:
        bl = b2[h][L - 1:L, :]
        kw = (k[h] * jnp.exp2(bl - b2[h])).astype(BF16)
        st_ref[h] = st_old[h] * jnp.exp2(bl) + _dot_tn(v[h], kw)
    for h in heads:
        z = proj_ref[:, O_Z + h * DV_C:O_Z + (h + 1) * DV_C].astype(F32)
        y_ref[:, h * DV_C:(h + 1) * DV_C] = (_rms_rows(o[h], cnw) * _silu(z)).astype(BF16)

    @pl.when(c == nc - 1)
    def _():
        for h in range(H_C):
            st = st_ref[h]
            gla_ref[h, :, 0:128] = _transpose(st[0:128, :])
            gla_ref[h, :, 128:256] = _transpose(st[128:256, :])

    cc = cc_ref[...]
    ss = ss_ref[...]
    rel = (r64 - c64).astype(F32)
    tcol = lax.broadcasted_iota(jnp.int32, (L, 1), 0).astype(F32)
    heads = range(H_D)
    lg = [_ret_log_gamma(h) for h in heads]
    qb, kf, kb, vd = [], [], [], []
    for h in heads:
        qd = proj_ref[:, O_QD + h * DK_D:O_QD + (h + 1) * DK_D].astype(F32)
        kd = proj_ref[:, O_KD + h * DK_D:O_KD + (h + 1) * DK_D].astype(F32)
        qb.append((qd * cc + pltpu.roll(qd, DK_D // 2, 1) * ss).astype(BF16))
        kf.append((kd * cc + pltpu.roll(kd, DK_D // 2, 1) * ss) * (DK_D ** -0.5))
        kb.append(kf[h].astype(BF16))
        vd.append(proj_ref[:, O_VD + h * DV_D:O_VD + (h + 1) * DV_D])
    s_old = [ret_ref[h] for h in heads]
    attd = [_dot_nt(qb[h], kb[h])
            * jnp.where(tri_incl, jnp.exp(lg[h] * jnp.maximum(rel, 0.0)), 0.0) for h in heads]
    od = [_dot(qb[h], s_old[h].astype(BF16)) * jnp.exp(lg[h] * (tcol + 1.0))
          + _dot(attd[h].astype(BF16), vd[h]) for h in heads]
    for h in heads:
        end = jnp.exp(lg[h] * (L - 1.0 - tcol))
        ret_ref[h] = math.exp(lg[h] * L) * s_old[h] + _dot_tn((kf[h] * end).astype(BF16), vd[h])
    for h in heads:
        z = proj_ref[:, O_Z + H_C * DV_C + h * DV_D:
                     O_Z + H_C * DV_C + (h + 1) * DV_D].astype(F32)
        y_ref[:, H_C * DV_C + h * DV_D:H_C * DV_C + (h + 1) * DV_D] = (
            _rms_rows(od[h], dnw) * _silu(z)).astype(BF16)


def _odd_prompt(proj, glr, w2, b2, cc, ss, cnw, dnw, batch, seq):
    nc = seq // CHUNK
    row = lambda b, c: (b * nc + c, 0)
    const2 = lambda b, c: (0, 0)
    sums, level = _gla_chunk_tables()
    n_sums = sums.shape[0]
    return pl.pallas_call(
        _odd_prompt_kernel,
        grid=(batch, nc),
        in_specs=[
            pl.BlockSpec((CHUNK, P_O), row),
            pl.BlockSpec((CHUNK, SMALL_O), row),
            pl.BlockSpec((LANE, H_C * DK_C), const2),
            pl.BlockSpec((1, H_C * DK_C), const2),
            pl.BlockSpec((n_sums, CHUNK), const2),
            pl.BlockSpec((CHUNK, CHUNK), const2),
            pl.BlockSpec((CHUNK, DK_D), lambda b, c: (c, 0)),
            pl.BlockSpec((CHUNK, DK_D), lambda b, c: (c, 0)),
            pl.BlockSpec((1, DV_C), const2),
            pl.BlockSpec((1, DV_D), const2),
        ],
        out_specs=[
            pl.BlockSpec((CHUNK, W_MIX), row),
            pl.BlockSpec((None, H_C, DK_C, DV_C), lambda b, c: (b, 0, 0, 0)),
            pl.BlockSpec((None, H_D, DK_D, DV_D), lambda b, c: (b, 0, 0, 0)),
        ],
        out_shape=[
            jax.ShapeDtypeStruct((batch * seq, W_MIX), BF16),
            jax.ShapeDtypeStruct((batch, H_C, DK_C, DV_C), F32),
            jax.ShapeDtypeStruct((batch, H_D, DK_D, DV_D), F32),
        ],
        scratch_shapes=[pltpu.VMEM((H_C, DV_C, DK_C), F32),
                        pltpu.VMEM((n_sums, H_C * DK_C), F32)],
        compiler_params=pltpu.CompilerParams(
            dimension_semantics=("arbitrary", "arbitrary"), vmem_limit_bytes=VMEM_LIMIT),
        name="odd_prompt",
    )(proj, glr, w2, b2, sums, level, cc, ss, cnw, dnw)


DEC_BB = 8


def _group_sum_matrix(groups):
    r = lax.broadcasted_iota(jnp.int32, (groups, 8 * groups), 0)
    c = lax.broadcasted_iota(jnp.int32, (groups, 8 * groups), 1)
    return ((c >> 3) == r).astype(F32)


def _sublane_group_sums(parts, gsum):
    folded = [p.reshape(p.shape[0] // 8, 8, p.shape[1]).sum(axis=0) for p in parts]
    return _dot(gsum, jnp.concatenate(folded, axis=0), precision=HI)


def _even_decode_kernel(u_ref, sm_ref, qk_ref, v_ref, op_ref, z_ref,
                        conv_ref, gdn_ref, mc_ref, mn_ref, mm_ref,
                        sp_ref, cw_ref, gnw_ref, mnw_ref,
                        y_ref, conv_o, gdn_o, mc_o, mn_o, mm_o):
    gnw = gnw_ref[...]
    mnw = mnw_ref[...]
    gsum = _group_sum_matrix(H_A)

    def body(b, carry):
        sm = sm_ref[pl.ds(b, 1), :]
        beta = _sigmoid(sm[:, 0:128])
        gam = jnp.exp(-jnp.exp(sp_ref[1:2, :]) * _softplus(sm[:, 128:256] + sp_ref[0:1, :]))
        ig = sm[:, 256:384] + sp_ref[2:3, :]
        lf = _log_sigmoid(sm[:, 384:512] + sp_ref[3:4, :])
        m0 = mm_ref[pl.ds(b, 1), :]
        w0 = lf + m0
        m_t = jnp.maximum(w0, ig)
        s0 = jnp.exp(w0 - m_t)
        we = jnp.exp(ig - m_t)
        floor = jnp.exp(-m_t)
        mm_o[pl.ds(b, 1), :] = m_t

        u = u_ref[b]
        cs = conv_ref[b]
        cw = cw_ref[...]
        uc = cw[0] * cs[0] + cw[1] * cs[1] + cw[2] * cs[2] + cw[3] * u
        conv_o[b, 0] = cs[1]
        conv_o[b, 1] = cs[2]
        conv_o[b, 2] = u
        act = _silu(uc)
        q = _l2n(act[0:8]) * (DK_A ** -0.5)
        k = _l2n(act[8:16])
        v = act[16:24]
        qt = _transpose(q)
        kt = _transpose(k)
        gcols = _transpose(jnp.concatenate(
            [beta, gam, s0, we, floor, jnp.zeros((3, LANE), F32)], axis=0))[0:8, :]
        beta_c, gam_c, s0_c, we_c, floor_c = (gcols[:, i:i + 1] for i in range(5))
        z = z_ref[b]
        s_old = [gdn_ref[b, h] for h in range(H_A)]
        ks = _sublane_group_sums([kt[:, h:h + 1] * s_old[h] for h in range(H_A)], gsum)
        uu = beta_c * (v - gam_c * ks)
        parts = []
        for h in range(H_A):
            sn = gam[:, h:h + 1] * s_old[h] + kt[:, h:h + 1] * uu[h:h + 1, :]
            gdn_o[b, h] = sn
            parts.append(qt[:, h:h + 1] * sn)
        o = _sublane_group_sums(parts, gsum)
        y_ref[b, 0:H_A, :] = _rms_rows(o, gnw) * _silu(z[0:H_A, :])

        qk = qk_ref[b]
        qkt = _transpose(qk)
        q_rows = qk[:, 0:DK_B]
        k_rows = qk[:, DK_B:2 * DK_B] * (DK_B ** -0.5)
        wv = we_c * v_ref[b]
        nn = s0_c * mn_ref[b] + we_c * k_rows
        mn_o[b] = nn
        parts = []
        for h in range(H_B):
            kc = qkt[DK_B:2 * DK_B, h:h + 1] * (DK_B ** -0.5)
            cn = s0[:, h:h + 1] * mc_ref[b, h] + kc * wv[h:h + 1, :]
            mc_o[b, h] = cn
            parts.append(qkt[0:DK_B, h:h + 1] * cn)
        num = _sublane_group_sums(parts, gsum)
        den = jnp.sum(q_rows * nn, axis=-1, keepdims=True)
        hh = num / jnp.maximum(jnp.abs(den), floor_c)
        y_ref[b, H_A:H_A + H_B, :] = (
            _sigmoid(op_ref[b]) * _rms_rows(hh, mnw) * _silu(z[H_A:H_A + H_B, :]))
        return carry

    lax.fori_loop(0, DEC_BB, body, 0, unroll=2)


def _even_decode(u, sm, qk, v, op, z, conv, gdn, mc, mn, mm, sp, cw, gnw, mnw):
    batch = u.shape[0]
    bb = DEC_BB
    i3 = lambda i: (i, 0, 0)
    i4 = lambda i: (i, 0, 0, 0)
    c2 = lambda i: (0, 0)
    return pl.pallas_call(
        _even_decode_kernel,
        grid=(batch // bb,),
        in_specs=[
            pl.BlockSpec((bb, 24, LANE), i3),
            pl.BlockSpec((bb, 512), lambda i: (i, 0)),
            pl.BlockSpec((bb, H_B, LANE), i3),
            pl.BlockSpec((bb, H_B, DV_B), i3),
            pl.BlockSpec((bb, H_B, DV_B), i3),
            pl.BlockSpec((bb, 16, LANE), i3),
            pl.BlockSpec((bb, CONV_W - 1, 24, LANE), i4),
            pl.BlockSpec((bb, H_A, DK_A, DV_A), i4),
            pl.BlockSpec((bb, H_B, DK_B, DV_B), i4),
            pl.BlockSpec((bb, H_B, DK_B), i3),
            pl.BlockSpec((bb, LANE), lambda i: (i, 0)),
            pl.BlockSpec((8, LANE), c2),
            pl.BlockSpec((CONV_W, 24, LANE), lambda i: (0, 0, 0)),
            pl.BlockSpec((1, DV_A), c2),
            pl.BlockSpec((1, DV_B), c2),
        ],
        out_specs=[
            pl.BlockSpec((bb, 16, LANE), i3),
            pl.BlockSpec((bb, CONV_W - 1, 24, LANE), i4),
            pl.BlockSpec((bb, H_A, DK_A, DV_A), i4),
            pl.BlockSpec((bb, H_B, DK_B, DV_B), i4),
            pl.BlockSpec((bb, H_B, DK_B), i3),
            pl.BlockSpec((bb, LANE), lambda i: (i, 0)),
        ],
        out_shape=[
            jax.ShapeDtypeStruct((batch, 16, LANE), F32),
            jax.ShapeDtypeStruct((batch, CONV_W - 1, 24, LANE), F32),
            jax.ShapeDtypeStruct((batch, H_A, DK_A, DV_A), F32),
            jax.ShapeDtypeStruct((batch, H_B, DK_B, DV_B), F32),
            jax.ShapeDtypeStruct((batch, H_B, DK_B), F32),
            jax.ShapeDtypeStruct((batch, LANE), F32),
        ],
        compiler_params=pltpu.CompilerParams(
            dimension_semantics=("arbitrary",), vmem_limit_bytes=VMEM_LIMIT),
        name="even_decode",
    )(u, sm, qk, v, op, z, conv, gdn, mc, mn, mm, sp, cw, gnw, mnw)


def _gla_gate_kernel(glr_ref, w2_ref, b2_ref, o_ref):
    o_ref[...] = _log_sigmoid(_dot(glr_ref[...].astype(BF16), w2_ref[...]) + b2_ref[...]) / GLA_TAU


def _gla_gate(glr, w2, b2):
    n = glr.shape[0]
    return pl.pallas_call(
        _gla_gate_kernel,
        out_shape=jax.ShapeDtypeStruct((n, H_C * DK_C), F32),
        name="gla_gate",
    )(glr, w2, b2)


def _odd_decode_kernel(qc_ref, kc_ref, vc_ref, g_ref, qd_ref, kd_ref, vd_ref, z_ref,
                       gla_ref, ret_ref, cc_ref, ss_ref, cnw_ref, dnw_ref,
                       y_ref, gla_o, ret_o):
    head_row = lax.broadcasted_iota(jnp.int32, (H_C + H_D, DV_C), 0)
    nw = jnp.where(head_row < H_C, cnw_ref[...], dnw_ref[...])
    cc = cc_ref[...]
    ss = ss_ref[...]
    gsum = _group_sum_matrix(H_C + H_D)

    def body(b, carry):
        q = qc_ref[b] * (DK_C ** -0.5)
        k = kc_ref[b]
        eg = jnp.exp(g_ref[b])
        t1 = _transpose(jnp.concatenate([q, k], axis=0))
        t2 = _transpose(jnp.concatenate([eg, eg], axis=0))
        v = vc_ref[b]
        parts = []
        for h in range(H_C):
            sn = t2[:, h:h + 1] * gla_ref[b, h] + t1[:, H_C + h:H_C + h + 1] * v[h:h + 1, :]
            gla_o[b, h] = sn
            parts.append(t1[:, h:h + 1] * sn)
        qd = qd_ref[b]
        kd = kd_ref[b]
        qd = qd * cc + pltpu.roll(qd, DK_D // 2, 1) * ss
        kd = (kd * cc + pltpu.roll(kd, DK_D // 2, 1) * ss) * (DK_D ** -0.5)
        t3 = _transpose(jnp.concatenate([qd, kd], axis=0))
        vd = vd_ref[b]
        for h in range(H_D):
            gamma = math.exp(_ret_log_gamma(h))
            sn = gamma * ret_ref[b, h] + t3[:, H_D + h:H_D + h + 1] * vd[h:h + 1, :]
            ret_o[b, h] = sn
            parts.append(t3[:, h:h + 1] * sn)
        o = _sublane_group_sums(parts, gsum)
        y_ref[b] = _rms_rows(o, nw) * _silu(z_ref[b])
        return carry

    lax.fori_loop(0, DEC_BB, body, 0, unroll=2)


def _odd_decode(qc, kc, vc, g, qd, kd, vd, z, gla, ret, cc, ss, cnw, dnw):
    batch = qc.shape[0]
    bb = DEC_BB
    i3 = lambda i: (i, 0, 0)
    i4 = lambda i: (i, 0, 0, 0)
    c2 = lambda i: (0, 0)
    return pl.pallas_call(
        _odd_decode_kernel,
        grid=(batch // bb,),
        in_specs=[
            pl.BlockSpec((bb, H_C, DK_C), i3),
            pl.BlockSpec((bb, H_C, DK_C), i3),
            pl.BlockSpec((bb, H_C, DV_C), i3),
            pl.BlockSpec((bb, H_C, DK_C), i3),
            pl.BlockSpec((bb, H_D, DK_D), i3),
            pl.BlockSpec((bb, H_D, DK_D), i3),
            pl.BlockSpec((bb, H_D, DV_D), i3),
            pl.BlockSpec((bb, 8, DV_C), i3),
            pl.BlockSpec((bb, H_C, DK_C, DV_C), i4),
            pl.BlockSpec((bb, H_D, DK_D, DV_D), i4),
            pl.BlockSpec((1, DK_D), c2),
            pl.BlockSpec((1, DK_D), c2),
            pl.BlockSpec((1, DV_C), c2),
            pl.BlockSpec((1, DV_D), c2),
        ],
        out_specs=[
            pl.BlockSpec((bb, 8, DV_C), i3),
            pl.BlockSpec((bb, H_C, DK_C, DV_C), i4),
            pl.BlockSpec((bb, H_D, DK_D, DV_D), i4),
        ],
        out_shape=[
            jax.ShapeDtypeStruct((batch, 8, DV_C), F32),
            jax.ShapeDtypeStruct((batch, H_C, DK_C, DV_C), F32),
            jax.ShapeDtypeStruct((batch, H_D, DK_D, DV_D), F32),
        ],
        compiler_params=pltpu.CompilerParams(
            dimension_semantics=("arbitrary",), vmem_limit_bytes=VMEM_LIMIT),
        name="odd_decode",
    )(qc, kc, vc, g, qd, kd, vd, z, gla, ret, cc, ss, cnw, dnw)


def _pad_cols(w, width):
    return jnp.pad(w, ((0, 0), (0, width - w.shape[1])))


def _prep_even_w_in(w):
    u = w[:, 0:3072]
    beta = w[:, 3072:3080]
    a = w[:, 3080:3088]
    qb = w[:, 3088:3600].reshape(-1, H_B, DK_B)
    kb = w[:, 3600:4112].reshape(-1, H_B, DK_B)
    vb = w[:, 4112:5136]
    ig = w[:, 5136:5144]
    fg = w[:, 5144:5152]
    op = w[:, 5152:6176]
    z = w[:, 6176:8224]
    qk = jnp.concatenate([qb, kb], axis=-1).reshape(-1, H_B * 2 * DK_B)
    small = jnp.concatenate([_pad_cols(beta, 128), _pad_cols(a, 128), _pad_cols(ig, 128),
                             _pad_cols(fg, 128)], axis=1)
    return jnp.concatenate([u, qk, vb, op, z], axis=1).astype(BF16), small.astype(BF16)


def _prep_odd_w_in(w):
    qc = w[:, 0:512]
    kc = w[:, 512:1024]
    vc = w[:, 1024:2048]
    glr = w[:, 2048:2064]
    qd = w[:, 2064:2576]
    kd = w[:, 2576:3088]
    vd = w[:, 3088:4112]
    z = w[:, 4112:6160]
    return (jnp.concatenate([qc, kc, vc, qd, kd, vd, z], axis=1).astype(BF16),
            _pad_cols(glr, SMALL_O).astype(BF16))


def _lane_row(vec, width=LANE):
    return jnp.pad(vec.astype(F32), (0, width - vec.shape[0]))


def _rotary_tables(pos):
    half = DK_D // 2
    inv = ROPE_BASE ** (-jnp.arange(half, dtype=F32) / half)
    ang = pos.astype(F32)[:, None] * inv[None, :]
    cos = jnp.cos(ang)
    sin = jnp.sin(ang)
    return jnp.concatenate([cos, cos], axis=-1), jnp.concatenate([-sin, sin], axis=-1)


def kernel(x_prompt, x_sample, c_prompt, c_sample, state_gdn, state_gdn_conv, state_mlstm_c,
           state_mlstm_n, state_mlstm_m, state_gla, state_ret, ada_w, ada_b, norm_w, ev_w_in,
           ev_w_out, gdn_conv_w, gdn_a_log, gdn_dt_bias, gdn_norm_w, mlstm_gate_b, mlstm_norm_w,
           od_w_in, od_w_out, gla_w2, gla_b2, gla_norm_w, ret_norm_w, final_norm_w):
    bp, tp, d = x_prompt.shape
    bs = x_sample.shape[0]
    n_p = bp * tp

    w_in_e, w_sm_e = _prep_even_w_in(ev_w_in[0])
    w_in_o, w_sm_o = _prep_odd_w_in(od_w_in[0])
    w_out_e = ev_w_out[0].astype(BF16)
    w_out_o = od_w_out[0].astype(BF16)
    sp = jnp.stack([_lane_row(gdn_dt_bias[0]), _lane_row(gdn_a_log[0]),
                    _lane_row(mlstm_gate_b[0, :H_B]), _lane_row(mlstm_gate_b[0, H_B:])]
                   + [jnp.zeros((LANE,), F32)] * 4)
    gnw = gdn_norm_w[0].reshape(1, DV_A)
    mnw = mlstm_norm_w[0].reshape(1, DV_B)
    cnw = gla_norm_w[0].reshape(1, DV_C)
    dnw = ret_norm_w[0].reshape(1, DV_D)
    w2 = jnp.pad(gla_w2[0], ((0, LANE - GLA_RANK), (0, 0))).astype(BF16)
    b2 = gla_b2[0].reshape(1, H_C * DK_C)

    mod = _modulation(jnp.concatenate([c_prompt, c_sample], axis=0), ada_w, ada_b)
    mod_p = [mod[l, :bp].reshape(bp, 1, 3 * d) for l in range(2)]
    mod_s = [mod[l, bp:].reshape(1, bs, 3 * d) for l in range(2)]

    xp = x_prompt.reshape(n_p, d)
    proj, small = _inproj(xp, mod_p[0], norm_w[0], w_in_e, w_sm_e, 1024, 2048)
    y, p_gdn, p_conv, p_mc, p_mn, p_mm = _even_prompt(proj, small, sp, gdn_conv_w[0], gnw, mnw,
                                                      bp, tp)
    xp = _outproj(y, xp, mod_p[0], w_out_e, final_norm_w, 512, False)
    proj, small = _inproj(xp, mod_p[1], norm_w[1], w_in_o, w_sm_o, 1024, 2048)
    cc_p, ss_p = _rotary_tables(jnp.arange(tp))
    y, p_gla, p_ret = _odd_prompt(proj, small, w2, b2, cc_p, ss_p, cnw, dnw, bp, tp)
    y_prompt = _outproj(y, xp, mod_p[1], w_out_o, final_norm_w, 512, True).reshape(bp, tp, d)

    xs = x_sample.reshape(bs, d)
    proj, small = _inproj(xs, mod_s[0], norm_w[0], w_in_e, w_sm_e, bs, 2048)
    proj = proj.astype(F32)
    mm_in = jnp.pad(state_mlstm_m[0], ((0, 0), (0, LANE - H_B)))
    y, s_conv, s_gdn, s_mc, s_mn, s_mm = _even_decode(
        proj[:, E_U:E_U + CONV_CH].reshape(bs, 24, LANE),
        small,
        proj[:, E_QK:E_QK + 1024].reshape(bs, H_B, LANE),
        proj[:, E_V:E_V + 1024].reshape(bs, H_B, DV_B),
        proj[:, E_OP:E_OP + 1024].reshape(bs, H_B, DV_B),
        proj[:, E_Z:E_Z + W_MIX].reshape(bs, 16, LANE),
        state_gdn_conv[0].reshape(bs, CONV_W - 1, 24, LANE),
        state_gdn[0], state_mlstm_c[0], state_mlstm_n[0], mm_in,
        sp, gdn_conv_w[0].reshape(CONV_W, 24, LANE), gnw, mnw)
    xs = _outproj(y.reshape(bs, W_MIX), xs, mod_s[0], w_out_e, final_norm_w, bs, False)
    proj, small = _inproj(xs, mod_s[1], norm_w[1], w_in_o, w_sm_o, bs, 2048)
    proj = proj.astype(F32)
    g_s = _gla_gate(small, w2, b2)
    cc_s, ss_s = _rotary_tables(PAST_LEN + jnp.arange(1))
    y, s_gla, s_ret = _odd_decode(
        proj[:, O_QC:O_QC + 512].reshape(bs, H_C, DK_C),
        proj[:, O_KC:O_KC + 512].reshape(bs, H_C, DK_C),
        proj[:, O_VC:O_VC + 1024].reshape(bs, H_C, DV_C),
        g_s.reshape(bs, H_C, DK_C),
        proj[:, O_QD:O_QD + 512].reshape(bs, H_D, DK_D),
        proj[:, O_KD:O_KD + 512].reshape(bs, H_D, DK_D),
        proj[:, O_VD:O_VD + 1024].reshape(bs, H_D, DV_D),
        proj[:, O_Z:O_Z + W_MIX].reshape(bs, 8, DV_C),
        state_gla[0], state_ret[0], cc_s, ss_s, cnw, dnw)
    y_sample = _outproj(y.reshape(bs, W_MIX), xs, mod_s[1], w_out_o, final_norm_w, bs, True)
    y_sample = y_sample.reshape(bs, 1, d)

    return (y_prompt, y_sample,
            p_gdn[None], p_conv[None], p_mc[None], p_mn[None], p_mm[None, :, 0, :H_B],
            p_gla[None], p_ret[None],
            s_gdn[None], s_conv.reshape(1, bs, CONV_W - 1, CONV_CH), s_mc[None], s_mn[None],
            s_mm[None, :, :H_B], s_gla[None], s_ret[None])
```

```python
import functools
import itertools
import math

import jax
import jax.numpy as jnp
import numpy as np
from jax import lax
from jax.experimental import pallas as pl
from jax.experimental.pallas import tpu as pltpu

F32 = jnp.float32
BF16 = jnp.bfloat16
HI = lax.Precision.HIGHEST

D_MODEL = 1024
W_MIX = 2048
H_A, DK_A, DV_A, CONV_W = 8, 128, 128, 4
CONV_CH = H_A * (2 * DK_A + DV_A)
H_B, DK_B, DV_B = 8, 64, 128
H_C, DK_C, DV_C, GLA_RANK, GLA_TAU = 4, 128, 256, 16, 16.0
H_D, DK_D, DV_D = 4, 128, 256
ROPE_BASE = 10000.0
PAST_LEN = 16384
EPS = 1e-6
LANE = 128

CHUNK = 64
SEQS_PER_STEP = 1
CHUNKS_PER_STEP = 4

E_U, E_QK, E_V, E_OP, E_Z = 0, 3072, 4096, 5120, 6144
P_E = 8192
SMALL_E = 512
O_QC, O_KC, O_VC, O_QD, O_KD, O_VD, O_Z = 0, 512, 1024, 2048, 2560, 3072, 4096
P_O = 6144
SMALL_O = 128

VMEM_LIMIT = 56 * 1024 * 1024


def _sigmoid(x):
    return 0.5 * jnp.tanh(0.5 * x) + 0.5


def _softplus(x):
    return jnp.maximum(x, 0.0) + jnp.log(1.0 + jnp.exp(-jnp.abs(x)))


def _log_sigmoid(x):
    return -_softplus(-x)


def _silu(x):
    half = 0.5 * x
    return half + half * jnp.tanh(half)


def _rms_rows(x, w):
    return x * lax.rsqrt(jnp.mean(x * x, axis=-1, keepdims=True) + EPS) * w


def _l2n(x):
    return x * lax.rsqrt(jnp.sum(x * x, axis=-1, keepdims=True) + EPS)


def _dot(a, b, precision=None):
    return jnp.dot(a, b, preferred_element_type=F32, precision=precision)


def _dot_nt(a, b, precision=None):
    return lax.dot_general(a, b, (((1,), (1,)), ((), ())), preferred_element_type=F32,
                           precision=precision)


def _dot_tn(a, b, precision=None):
    return lax.dot_general(a, b, (((0,), (0,)), ((), ())), preferred_element_type=F32,
                           precision=precision)


def _eye(n):
    r = lax.broadcasted_iota(jnp.int32, (n, n), 0)
    c = lax.broadcasted_iota(jnp.int32, (n, n), 1)
    return (r == c).astype(F32)


def _transpose(x):
    return _dot_nt(_eye(x.shape[1]), x, precision=HI)


def _tri_masks(n):
    r = lax.broadcasted_iota(jnp.int32, (n, n), 0)
    c = lax.broadcasted_iota(jnp.int32, (n, n), 1)
    return r >= c, r > c


def _mod_kernel(c_ref, w_ref, b_ref, o_ref):
    cs = _silu(c_ref[...]).astype(BF16)
    o_ref[...] = _dot(cs, w_ref[...].astype(BF16)) + b_ref[...]


def _modulation(c_all, ada_w, ada_b):
    depth, d, d3 = ada_w.shape
    rows = c_all.shape[0]
    tn = 1024
    return pl.pallas_call(
        _mod_kernel,
        grid=(depth, d3 // tn),
        in_specs=[
            pl.BlockSpec((rows, d), lambda l, j: (0, 0)),
            pl.BlockSpec((None, d, tn), lambda l, j: (l, 0, j)),
            pl.BlockSpec((None, 1, tn), lambda l, j: (l, 0, j)),
        ],
        out_specs=pl.BlockSpec((None, rows, tn), lambda l, j: (l, 0, j)),
        out_shape=jax.ShapeDtypeStruct((depth, rows, d3), F32),
        compiler_params=pltpu.CompilerParams(
            dimension_semantics=("arbitrary", "arbitrary"), vmem_limit_bytes=VMEM_LIMIT),
        name="modulation",
    )(c_all, ada_w, ada_b.reshape(depth, 1, d3))


def _inproj_kernel(x_ref, mod_ref, nw_ref, w_ref, ws_ref, o_ref, os_ref, h_ref):
    @pl.when(pl.program_id(1) == 0)
    def _():
        x = x_ref[...]
        y = _rms_rows(x, nw_ref[...])
        shift = mod_ref[:, 0:D_MODEL]
        scale = mod_ref[:, D_MODEL:2 * D_MODEL]
        h_ref[...] = (y * (1.0 + scale) + shift).astype(BF16)
        os_ref[...] = _dot(h_ref[...], ws_ref[...])

    o_ref[...] = _dot(h_ref[...], w_ref[...]).astype(o_ref.dtype)


def _inproj(x, mod, norm_w, w, w_small, tm, tn):
    n, d = x.shape
    p = w.shape[1]
    ps = w_small.shape[1]
    g, r, _ = mod.shape
    tiles_per_group = (n // g) // tm
    return pl.pallas_call(
        _inproj_kernel,
        grid=(n // tm, p // tn),
        in_specs=[
            pl.BlockSpec((tm, d), lambda i, j: (i, 0)),
            pl.BlockSpec((None, r, 3 * d), lambda i, j: (i // tiles_per_group, 0, 0)),
            pl.BlockSpec((1, d), lambda i, j: (0, 0)),
            pl.BlockSpec((d, tn), lambda i, j: (0, j)),
            pl.BlockSpec((d, ps), lambda i, j: (0, 0)),
        ],
        out_specs=[pl.BlockSpec((tm, tn), lambda i, j: (i, j)),
                   pl.BlockSpec((tm, ps), lambda i, j: (i, 0))],
        out_shape=[jax.ShapeDtypeStruct((n, p), BF16), jax.ShapeDtypeStruct((n, ps), F32)],
        scratch_shapes=[pltpu.VMEM((tm, d), BF16)],
        compiler_params=pltpu.CompilerParams(
            dimension_semantics=("arbitrary", "arbitrary"), vmem_limit_bytes=VMEM_LIMIT),
        name="inproj",
    )(x, mod, norm_w.reshape(1, d), w, w_small)


def _outproj_kernel(y_ref, x_ref, mod_ref, w_ref, fw_ref, o_ref, *, final):
    acc = _dot(y_ref[...].astype(BF16), w_ref[...])
    gate = mod_ref[:, 2 * D_MODEL:3 * D_MODEL]
    xn = x_ref[...] + gate * acc
    if final:
        xn = _rms_rows(xn, fw_ref[...])
    o_ref[...] = xn


def _outproj(y, x, mod, w, final_w, tm, final):
    n, d = x.shape
    k = y.shape[1]
    g, r, _ = mod.shape
    tiles_per_group = (n // g) // tm
    return pl.pallas_call(
        functools.partial(_outproj_kernel, final=final),
        grid=(n // tm,),
        in_specs=[
            pl.BlockSpec((tm, k), lambda i: (i, 0)),
            pl.BlockSpec((tm, d), lambda i: (i, 0)),
            pl.BlockSpec((None, r, 3 * d), lambda i: (i // tiles_per_group, 0, 0)),
            pl.BlockSpec((k, d), lambda i: (0, 0)),
            pl.BlockSpec((1, d), lambda i: (0, 0)),
        ],
        out_specs=pl.BlockSpec((tm, d), lambda i: (i, 0)),
        out_shape=jax.ShapeDtypeStruct((n, d), F32),
        compiler_params=pltpu.CompilerParams(
            dimension_semantics=("arbitrary",), vmem_limit_bytes=VMEM_LIMIT),
        name="outproj",
    )(y, x, mod, w, final_w.reshape(1, d))


def _neumann_correction(a_list, n):
    ys = [-a for a in a_list]
    pbs = [a.astype(BF16) for a in a_list]
    for _ in range(int(math.log2(n)) - 1):
        for i in range(len(ys)):
            p = _dot(pbs[i], pbs[i])
            pbs[i] = p.astype(BF16)
            ys[i] = ys[i] + p + _dot(ys[i].astype(BF16), pbs[i])
        yield
    idx = range(len(ys))
    a_h = [a.astype(BF16) for a in a_list]
    a_l = [(a_list[i] - a_h[i].astype(F32)).astype(BF16) for i in idx]
    y_h = [y.astype(BF16) for y in ys]
    y_l = [(ys[i] - y_h[i].astype(F32)).astype(BF16) for i in idx]
    ay = [_dot(a_h[i], y_h[i]) + _dot(a_h[i], y_l[i]) + _dot(a_l[i], y_h[i]) for i in idx]
    e = [-(a_list[i] + ys[i] + ay[i]) for i in idx]
    return [ys[i] + e[i] + _dot(y_h[i], e[i].astype(BF16)) for i in idx]


def _run_interleaved(stage_generators):
    for _ in itertools.zip_longest(*stage_generators):
        pass


def _conv_shift_matrix():
    n = CHUNK
    r = np.arange((CONV_W - 1) * n)[:, None]
    col = np.arange(2 * n)[None, :]
    return jnp.asarray(col == n + (r % n) - (r // n + 1), BF16)


def _even_prompt_kernel(proj_ref, sm_ref, sp_ref, cw_ref, shift_ref, gnw_ref, mnw_ref,
                        y_ref, gdn_ref, conv_ref, mc_ref, mn_ref, mm_ref,
                        uprev_ref, qkv_ref):
    for j in range(CHUNKS_PER_STEP):
        rows = pl.ds(j * CHUNK, CHUNK)
        _run_interleaved([
            _even_prompt_sequence(j == 0, proj_ref.at[i, rows], sm_ref.at[i, rows], sp_ref, cw_ref,
                                  shift_ref, gnw_ref, mnw_ref, y_ref.at[i, rows], gdn_ref.at[i],
                                  conv_ref.at[i], mc_ref.at[i], mn_ref.at[i], mm_ref.at[i],
                                  uprev_ref.at[i], qkv_ref.at[i])
            for i in range(SEQS_PER_STEP)])


def _even_prompt_sequence(first_in_step, proj_ref, sm_ref, sp_ref, cw_ref, shift_ref, gnw_ref,
                          mnw_ref, y_ref, gdn_ref, conv_ref, mc_ref, mn_ref, mm_ref,
                          uprev_ref, qkv_ref):
    L = CHUNK

    if first_in_step:
        @pl.when(pl.program_id(1) == 0)
        def _():
            gdn_ref[...] = jnp.zeros_like(gdn_ref)
            mc_ref[...] = jnp.zeros_like(mc_ref)
            mn_ref[...] = jnp.zeros_like(mn_ref)
            mm_ref[...] = jnp.zeros_like(mm_ref)
            uprev_ref[...] = jnp.zeros_like(uprev_ref)

    tri_incl, tri_strict = _tri_masks(L)
    tri_f = tri_incl.astype(F32)

    ub = proj_ref[:, E_U:E_U + CONV_CH]
    shifted = _dot(shift_ref[...], jnp.concatenate([uprev_ref[...], ub], axis=0))
    uprev_ref[...] = ub
    u = ub.astype(F32)
    cw = cw_ref[...]
    uc = (cw[0:1] * shifted[2 * L:3 * L] + cw[1:2] * shifted[L:2 * L]
          + cw[2:3] * shifted[0:L] + cw[3:4] * u)
    conv_ref[...] = u[L - 3:L, :]
    qkv_ref[...] = _silu(uc)
    yield

    sm = sm_ref[...]
    head_lane = lax.broadcasted_iota(jnp.int32, (L, LANE), 1) < H_A
    beta = _sigmoid(sm[:, 0:128])
    g = -jnp.exp(sp_ref[1:2, :]) * _softplus(sm[:, 128:256] + sp_ref[0:1, :])
    ig = sm[:, 256:384] + sp_ref[2:3, :]
    lf = _log_sigmoid(sm[:, 384:512] + sp_ref[3:4, :])
    packed = jnp.where(head_lane, g, 0.0) + pltpu.roll(jnp.where(head_lane, lf, 0.0), H_A, 1)
    cum = _dot(tri_f, packed, precision=HI)
    cum_t = _transpose(cum + pltpu.roll(jnp.where(head_lane, ig, 0.0), H_A + H_B, 1))

    gnw = gnw_ref[...]
    mnw = mnw_ref[...]
    yield

    qkb, kf, decay, beta_c, gam, bh_c = [], [], [], [], [], []
    for h in range(H_A):
        q = _l2n(qkv_ref[:, h * DK_A:(h + 1) * DK_A]) * (DK_A ** -0.5)
        k = _l2n(qkv_ref[:, H_A * DK_A + h * DK_A:H_A * DK_A + (h + 1) * DK_A])
        kf.append(k)
        qkb.append(jnp.concatenate([q, k], axis=0).astype(BF16))
    yield
    qkk = [_dot_nt(qkb[h], qkb[h][L:2 * L]) for h in range(H_A)]
    yield
    a_list = []
    for h in range(H_A):
        bh_c.append(cum[:, h:h + 1])
        diff = bh_c[h] - cum_t[h:h + 1, :]
        decay.append(jnp.where(tri_incl, jnp.exp(jnp.where(tri_incl, diff, 0.0)), 0.0))
        beta_c.append(beta[:, h:h + 1])
        gam.append(jnp.exp(bh_c[h]))
        a_list.append(jnp.where(tri_strict, decay[h] * qkk[h][L:2 * L], 0.0) * beta_c[h])
    yield
    corr = yield from _neumann_correction(a_list, L)
    yield

    heads = range(H_A)
    s_old = [gdn_ref[h] for h in heads]
    qks = [_dot(qkb[h], s_old[h].astype(BF16)) for h in heads]
    yield
    rhs = [beta_c[h] * (qkv_ref[:, 2 * H_A * DK_A + h * DV_A:2 * H_A * DK_A + (h + 1) * DV_A]
                        - gam[h] * qks[h][L:2 * L]) for h in heads]
    ub = [(rhs[h] + _dot(corr[h].astype(BF16), rhs[h].astype(BF16))).astype(BF16) for h in heads]
    yield
    o = [gam[h] * qks[h][0:L] + _dot((qkk[h][0:L] * decay[h]).astype(BF16), ub[h])
         for h in heads]
    yield
    for h in heads:
        last = bh_c[h][L - 1:L, :]
        kw = (kf[h] * jnp.exp(last - bh_c[h])).astype(BF16)
        gdn_ref[h] = jnp.exp(last) * s_old[h] + _dot_tn(kw, ub[h])
    yield
    for h in heads:
        z = proj_ref[:, E_Z + h * DV_A:E_Z + (h + 1) * DV_A].astype(F32)
        y_ref[:, h * DV_A:(h + 1) * DV_A] = (_rms_rows(o[h], gnw) * _silu(z)).astype(BF16)
    yield

    heads = range(H_B)
    qs, qbs, kbs, vbs = [], [], [], []
    for h in heads:
        qk_in = proj_ref[:, E_QK + h * 128:E_QK + (h + 1) * 128]
        qbs.append(qk_in[:, 0:DK_B])
        qs.append(qbs[h].astype(F32))
        kbs.append((qk_in[:, DK_B:2 * DK_B].astype(F32) * (DK_B ** -0.5)).astype(BF16))
        vbs.append(proj_ref[:, E_V + h * DV_B:E_V + (h + 1) * DV_B])
    qkm = [_dot_nt(qbs[h], kbs[h]) for h in heads]
    yield
    b_c = [cum[:, H_A + h:H_A + h + 1] for h in heads]
    m0 = [mm_ref[0:1, h:h + 1] for h in heads]
    dmat = [jnp.where(tri_incl, b_c[h] - cum_t[H_A + h:H_A + h + 1, :]
                      + cum_t[H_A + H_B + h:H_A + H_B + h + 1, :], -jnp.inf) for h in heads]
    w0 = [b_c[h] + m0[h] for h in heads]
    m_t = [jnp.maximum(w0[h], jnp.max(dmat[h], axis=-1, keepdims=True)) for h in heads]
    p = [jnp.exp(dmat[h] - m_t[h]) * qkm[h] for h in heads]
    s0 = [jnp.exp(w0[h] - m_t[h]) for h in heads]
    yield
    c_old = [mc_ref[h] for h in heads]
    n_old = [mn_ref[h:h + 1, :] for h in heads]
    num = [s0[h] * _dot(qbs[h], c_old[h].astype(BF16)) + _dot(p[h].astype(BF16), vbs[h])
           for h in heads]
    den = [s0[h] * jnp.sum(qs[h] * n_old[h], axis=-1, keepdims=True)
           + jnp.sum(p[h], axis=-1, keepdims=True) for h in heads]
    hh = [num[h] / jnp.maximum(jnp.abs(den[h]), jnp.exp(-m_t[h])) for h in heads]
    yield
    for h in heads:
        m_end = m_t[h][L - 1:L, :]
        b_last = b_c[h][L - 1:L, :]
        we = jnp.exp(b_last - b_c[h] + ig[:, h:h + 1] - m_end)
        se = jnp.exp(b_last + m0[h] - m_end)
        kw = kbs[h].astype(F32) * we
        mc_ref[h] = se * c_old[h] + _dot_tn(kw.astype(BF16), vbs[h])
        mn_ref[h:h + 1, :] = se * n_old[h] + jnp.sum(kw, axis=0, keepdims=True)
        mm_ref[0:1, h:h + 1] = m_end
    yield
    for h in heads:
        o_pre = proj_ref[:, E_OP + h * DV_B:E_OP + (h + 1) * DV_B].astype(F32)
        z = proj_ref[:, E_Z + H_A * DV_A + h * DV_B:
                     E_Z + H_A * DV_A + (h + 1) * DV_B].astype(F32)
        y_ref[:, H_A * DV_A + h * DV_B:H_A * DV_A + (h + 1) * DV_B] = (
            _sigmoid(o_pre) * _rms_rows(hh[h], mnw) * _silu(z)).astype(BF16)


def _even_prompt(proj, small, sp, conv_w, gnw, mnw, batch, seq):
    step = CHUNK * CHUNKS_PER_STEP
    nc = seq // step
    nb = SEQS_PER_STEP
    row = lambda b, c: (b, c, 0)
    const2 = lambda b, c: (0, 0)
    st3 = lambda b, c: (b, 0, 0)
    st4 = lambda b, c: (b, 0, 0, 0)
    return pl.pallas_call(
        _even_prompt_kernel,
        grid=(batch // nb, nc),
        in_specs=[
            pl.BlockSpec((nb, step, P_E), row),
            pl.BlockSpec((nb, step, SMALL_E), row),
            pl.BlockSpec((8, LANE), const2),
            pl.BlockSpec((CONV_W, CONV_CH), const2),
            pl.BlockSpec(((CONV_W - 1) * CHUNK, 2 * CHUNK), const2),
            pl.BlockSpec((1, DV_A), const2),
            pl.BlockSpec((1, DV_B), const2),
        ],
        out_specs=[
            pl.BlockSpec((nb, step, W_MIX), row),
            pl.BlockSpec((nb, H_A, DK_A, DV_A), st4),
            pl.BlockSpec((nb, CONV_W - 1, CONV_CH), st3),
            pl.BlockSpec((nb, H_B, DK_B, DV_B), st4),
            pl.BlockSpec((nb, H_B, DK_B), st3),
            pl.BlockSpec((nb, 1, LANE), st3),
        ],
        out_shape=[
            jax.ShapeDtypeStruct((batch, seq, W_MIX), BF16),
            jax.ShapeDtypeStruct((batch, H_A, DK_A, DV_A), F32),
            jax.ShapeDtypeStruct((batch, CONV_W - 1, CONV_CH), F32),
            jax.ShapeDtypeStruct((batch, H_B, DK_B, DV_B), F32),
            jax.ShapeDtypeStruct((batch, H_B, DK_B), F32),
            jax.ShapeDtypeStruct((batch, 1, LANE), F32),
        ],
        scratch_shapes=[pltpu.VMEM((nb, CHUNK, CONV_CH), BF16),
                        pltpu.VMEM((nb, CHUNK, CONV_CH), F32)],
        compiler_params=pltpu.CompilerParams(
            dimension_semantics=("arbitrary", "arbitrary"), vmem_limit_bytes=VMEM_LIMIT),
        name="even_prompt",
    )(proj, small, sp, conv_w, _conv_shift_matrix(), gnw, mnw)


def _ret_log_gamma(h):
    return math.log(1.0 - 2.0 ** (-5.0 - h))


GLA_LEVELS = int(math.log2(CHUNK))
GLA_MM_LEVELS = (1, 2)
LOG2E = math.log2(math.e)


def _gla_chunk_tables():
    n = CHUNK
    t = np.arange(n)[:, None]
    i = np.arange(n)[None, :]
    blocks = [(i <= t)]
    for lvl in GLA_MM_LEVELS:
        hs = 1 << lvl
        p = (t & ~(2 * hs - 1)) + hs - 1
        upper = (t & hs) != 0
        blocks.append(np.where(upper, (i > p) & (i <= t), (i > t) & (i <= p)))
    sums = np.concatenate(blocks, axis=0).astype(np.float32)
    s = np.arange(n)[None, :]
    level = np.full((n, n), GLA_LEVELS + 1, np.int32)
    level[np.arange(n), np.arange(n)] = GLA_LEVELS
    for lvl in range(GLA_LEVELS):
        hs = 1 << lvl
        same = (t >> (lvl + 1)) == (s >> (lvl + 1))
        level[same & ((t & hs) != 0) & ((s & hs) == 0)] = lvl
    return jnp.asarray(sums), jnp.asarray(level)


def _odd_prompt_kernel(proj_ref, glr_ref, w2_ref, b2_ref, sums_ref, level_ref, cc_ref, ss_ref,
                       cnw_ref, dnw_ref, y_ref, gla_ref, ret_ref, st_ref, x_ref):
    for j in range(CHUNKS_PER_STEP):
        rows = pl.ds(j * CHUNK, CHUNK)
        _odd_prompt_chunk(j == 0, j == CHUNKS_PER_STEP - 1, proj_ref.at[rows], glr_ref.at[rows],
                          w2_ref, b2_ref, sums_ref, level_ref, cc_ref.at[rows], ss_ref.at[rows],
                          cnw_ref, dnw_ref, y_ref.at[rows], gla_ref, ret_ref, st_ref, x_ref)


def _odd_prompt_chunk(first_in_step, last_in_step, proj_ref, glr_ref, w2_ref, b2_ref, sums_ref,
                      level_ref, cc_ref, ss_ref, cnw_ref, dnw_ref, y_ref, gla_ref, ret_ref,
                      st_ref, x_ref):
    L = CHUNK
    c = pl.program_id(1)
    nc = pl.num_programs(1)

    if first_in_step:
        @pl.when(c == 0)
        def _():
            st_ref[...] = jnp.zeros_like(st_ref)
            ret_ref[...] = jnp.zeros_like(ret_ref)

    tri_incl, _ = _tri_masks(L)
    r64 = lax.broadcasted_iota(jnp.int32, (L, L), 0)
    c64 = lax.broadcasted_iota(jnp.int32, (L, L), 1)
    level = level_ref[...]

    glr = glr_ref[...].astype(BF16)
    gc2 = _log_sigmoid(_dot(glr, w2_ref[...]) + b2_ref[...]) * (LOG2E / GLA_TAU)
    x_ref[...] = _dot(sums_ref[...], gc2, precision=HI)
    odd_row = (lax.broadcasted_iota(jnp.int32, (L, DK_C), 0) & 1) == 1

    def level_exponent(lvl, h):
        cols = slice(h * DK_C, (h + 1) * DK_C)
        if lvl == 0:
            return jnp.where(odd_row, gc2[:, cols], 0.0)
        if lvl in GLA_MM_LEVELS:
            r0 = (1 + GLA_MM_LEVELS.index(lvl)) * L
            return x_ref[r0:r0 + L, cols]
        hs = 1 << lvl
        pieces = []
        for lo in range(0, L, 2 * hs):
            ref = x_ref[lo + hs - 1:lo + hs, cols]
            pieces.append(ref - x_ref[lo:lo + hs, cols])
            pieces.append(x_ref[lo + hs:lo + 2 * hs, cols] - ref)
        return jnp.concatenate(pieces, axis=0)

    cnw = cnw_ref[...]
    dnw = dnw_ref[...]

    heads = range(H_C)
    q = [proj_ref[:, O_QC + h * DK_C:O_QC + (h + 1) * DK_C].astype(F32) * (DK_C ** -0.5)
         for h in heads]
    k = [proj_ref[:, O_KC + h * DK_C:O_KC + (h + 1) * DK_C].astype(F32) for h in heads]
    v = [proj_ref[:, O_VC + h * DV_C:O_VC + (h + 1) * DV_C] for h in heads]
    att = [jnp.where(level == GLA_LEVELS, _dot_nt(q[h].astype(BF16), k[h].astype(BF16)), 0.0)
           for h in heads]
    for lvl in range(GLA_LEVELS):
        for h in heads:
            e = jnp.exp2(level_exponent(lvl, h))
            m = _dot_nt((q[h] * e).astype(BF16), (k[h] * e).astype(BF16))
            att[h] = jnp.where(level == lvl, m, att[h])
    b2 = [x_ref[0:L, h * DK_C:(h + 1) * DK_C] for h in heads]
    st_old = [st_ref[h] for h in heads]
    o = [_dot_nt((q[h] * jnp.exp2(b2[h])).astype(BF16), st_old[h].astype(BF16))
         + _dot(att[h].astype(BF16), v[h]) for h in heads]
    for h in heads:
        bl = b2[h][L - 1:L, :]
        kw = (k[h] * jnp.exp2(bl - b2[h])).astype(BF16)
        st_ref[h] = st_old[h] * jnp.exp2(bl) + _dot_tn(v[h], kw)
    for h in heads:
        z = proj_ref[:, O_Z + h * DV_C:O_Z + (h + 1) * DV_C].astype(F32)
        y_ref[:, h * DV_C:(h + 1) * DV_C] = (_rms_rows(o[h], cnw) * _silu(z)).astype(BF16)

    if last_in_step:
        @pl.when(c == nc - 1)
        def _():
            for h in range(H_C):
                st = st_ref[h]
                gla_ref[h, :, 0:128] = _transpose(st[0:128, :])
                gla_ref[h, :, 128:256] = _transpose(st[128:256, :])

    cc = cc_ref[...]
    ss = ss_ref[...]
    rel = (r64 - c64).astype(F32)
    tcol = lax.broadcasted_iota(jnp.int32, (L, 1), 0).astype(F32)
    heads = range(H_D)
    lg = [_ret_log_gamma(h) for h in heads]
    qb, kf, kb, vd = [], [], [], []
    for h in heads:
        qd = proj_ref[:, O_QD + h * DK_D:O_QD + (h + 1) * DK_D].astype(F32)
        kd = proj_ref[:, O_KD + h * DK_D:O_KD + (h + 1) * DK_D].astype(F32)
        qb.append((qd * cc + pltpu.roll(qd, DK_D // 2, 1) * ss).astype(BF16))
        kf.append((kd * cc + pltpu.roll(kd, DK_D // 2, 1) * ss) * (DK_D ** -0.5))
        kb.append(kf[h].astype(BF16))
        vd.append(proj_ref[:, O_VD + h * DV_D:O_VD + (h + 1) * DV_D])
    s_old = [ret_ref[h] for h in heads]
    attd = [_dot_nt(qb[h], kb[h])
            * jnp.where(tri_incl, jnp.exp(lg[h] * jnp.maximum(rel, 0.0)), 0.0) for h in heads]
    od = [_dot(qb[h], s_old[h].astype(BF16)) * jnp.exp(lg[h] * (tcol + 1.0))
          + _dot(attd[h].astype(BF16), vd[h]) for h in heads]
    for h in heads:
        end = jnp.exp(lg[h] * (L - 1.0 - tcol))
        ret_ref[h] = math.exp(lg[h] * L) * s_old[h] + _dot_tn((kf[h] * end).astype(BF16), vd[h])
    for h in heads:
        z = proj_ref[:, O_Z + H_C * DV_C + h * DV_D:
                     O_Z + H_C * DV_C + (h + 1) * DV_D].astype(F32)
        y_ref[:, H_C * DV_C + h * DV_D:H_C * DV_C + (h + 1) * DV_D] = (
            _rms_rows(od[h], dnw) * _silu(z)).astype(BF16)


def _odd_prompt(proj, glr, w2, b2, cc, ss, cnw, dnw, batch, seq):
    step = CHUNK * CHUNKS_PER_STEP
    nc = seq // step
    row = lambda b, c: (b * nc + c, 0)
    const2 = lambda b, c: (0, 0)
    sums, level = _gla_chunk_tables()
    n_sums = sums.shape[0]
    return pl.pallas_call(
        _odd_prompt_kernel,
        grid=(batch, nc),
        in_specs=[
            pl.BlockSpec((step, P_O), row),
            pl.BlockSpec((step, SMALL_O), row),
            pl.BlockSpec((LANE, H_C * DK_C), const2),
            pl.BlockSpec((1, H_C * DK_C), const2),
            pl.BlockSpec((n_sums, CHUNK), const2),
            pl.BlockSpec((CHUNK, CHUNK), const2),
            pl.BlockSpec((step, DK_D), lambda b, c: (c, 0)),
            pl.BlockSpec((step, DK_D), lambda b, c: (c, 0)),
            pl.BlockSpec((1, DV_C), const2),
            pl.BlockSpec((1, DV_D), const2),
        ],
        out_specs=[
            pl.BlockSpec((step, W_MIX), row),
            pl.BlockSpec((None, H_C, DK_C, DV_C), lambda b, c: (b, 0, 0, 0)),
            pl.BlockSpec((None, H_D, DK_D, DV_D), lambda b, c: (b, 0, 0, 0)),
        ],
        out_shape=[
            jax.ShapeDtypeStruct((batch * seq, W_MIX), BF16),
            jax.ShapeDtypeStruct((batch, H_C, DK_C, DV_C), F32),
            jax.ShapeDtypeStruct((batch, H_D, DK_D, DV_D), F32),
        ],
        scratch_shapes=[pltpu.VMEM((H_C, DV_C, DK_C), F32),
                        pltpu.VMEM((n_sums, H_C * DK_C), F32)],
        compiler_params=pltpu.CompilerParams(
            dimension_semantics=("arbitrary", "arbitrary"), vmem_limit_bytes=VMEM_LIMIT),
        name="odd_prompt",
    )(proj, glr, w2, b2, sums, level, cc, ss, cnw, dnw)


DEC_BB = 8


def _group_sum_matrix(groups):
    r = lax.broadcasted_iota(jnp.int32, (groups, 8 * groups), 0)
    c = lax.broadcasted_iota(jnp.int32, (groups, 8 * groups), 1)
    return ((c >> 3) == r).astype(F32)


def _sublane_group_sums(parts, gsum):
    folded = [p.reshape(p.shape[0] // 8, 8, p.shape[1]).sum(axis=0) for p in parts]
    return _dot(gsum, jnp.concatenate(folded, axis=0), precision=HI)


def _even_decode_kernel(u_ref, sm_ref, qk_ref, v_ref, op_ref, z_ref,
                        conv_ref, gdn_ref, mc_ref, mn_ref, mm_ref,
                        sp_ref, cw_ref, gnw_ref, mnw_ref,
                        y_ref, conv_o, gdn_o, mc_o, mn_o, mm_o):
    gnw = gnw_ref[...]
    mnw = mnw_ref[...]
    gsum = _group_sum_matrix(H_A)

    def body(b, carry):
        sm = sm_ref[pl.ds(b, 1), :]
        beta = _sigmoid(sm[:, 0:128])
        gam = jnp.exp(-jnp.exp(sp_ref[1:2, :]) * _softplus(sm[:, 128:256] + sp_ref[0:1, :]))
        ig = sm[:, 256:384] + sp_ref[2:3, :]
        lf = _log_sigmoid(sm[:, 384:512] + sp_ref[3:4, :])
        m0 = mm_ref[pl.ds(b, 1), :]
        w0 = lf + m0
        m_t = jnp.maximum(w0, ig)
        s0 = jnp.exp(w0 - m_t)
        we = jnp.exp(ig - m_t)
        floor = jnp.exp(-m_t)
        mm_o[pl.ds(b, 1), :] = m_t

        u = u_ref[b]
        cs = conv_ref[b]
        cw = cw_ref[...]
        uc = cw[0] * cs[0] + cw[1] * cs[1] + cw[2] * cs[2] + cw[3] * u
        conv_o[b, 0] = cs[1]
        conv_o[b, 1] = cs[2]
        conv_o[b, 2] = u
        act = _silu(uc)
        q = _l2n(act[0:8]) * (DK_A ** -0.5)
        k = _l2n(act[8:16])
        v = act[16:24]
        qt = _transpose(q)
        kt = _transpose(k)
        gcols = _transpose(jnp.concatenate(
            [beta, gam, s0, we, floor, jnp.zeros((3, LANE), F32)], axis=0))[0:8, :]
        beta_c, gam_c, s0_c, we_c, floor_c = (gcols[:, i:i + 1] for i in range(5))
        z = z_ref[b]
        s_old = [gdn_ref[b, h] for h in range(H_A)]
        ks = _sublane_group_sums([kt[:, h:h + 1] * s_old[h] for h in range(H_A)], gsum)
        uu = beta_c * (v - gam_c * ks)
        parts = []
        for h in range(H_A):
            sn = gam[:, h:h + 1] * s_old[h] + kt[:, h:h + 1] * uu[h:h + 1, :]
            gdn_o[b, h] = sn
            parts.append(qt[:, h:h + 1] * sn)
        o = _sublane_group_sums(parts, gsum)
        y_ref[b, 0:H_A, :] = _rms_rows(o, gnw) * _silu(z[0:H_A, :])

        qk = qk_ref[b]
        qkt = _transpose(qk)
        q_rows = qk[:, 0:DK_B]
        k_rows = qk[:, DK_B:2 * DK_B] * (DK_B ** -0.5)
        wv = we_c * v_ref[b]
        nn = s0_c * mn_ref[b] + we_c * k_rows
        mn_o[b] = nn
        parts = []
        for h in range(H_B):
            kc = qkt[DK_B:2 * DK_B, h:h + 1] * (DK_B ** -0.5)
            cn = s0[:, h:h + 1] * mc_ref[b, h] + kc * wv[h:h + 1, :]
            mc_o[b, h] = cn
            parts.append(qkt[0:DK_B, h:h + 1] * cn)
        num = _sublane_group_sums(parts, gsum)
        den = jnp.sum(q_rows * nn, axis=-1, keepdims=True)
        hh = num / jnp.maximum(jnp.abs(den), floor_c)
        y_ref[b, H_A:H_A + H_B, :] = (
            _sigmoid(op_ref[b]) * _rms_rows(hh, mnw) * _silu(z[H_A:H_A + H_B, :]))
        return carry

    lax.fori_loop(0, DEC_BB, body, 0, unroll=2)


def _even_decode(u, sm, qk, v, op, z, conv, gdn, mc, mn, mm, sp, cw, gnw, mnw):
    batch = u.shape[0]
    bb = DEC_BB
    i3 = lambda i: (i, 0, 0)
    i4 = lambda i: (i, 0, 0, 0)
    c2 = lambda i: (0, 0)
    return pl.pallas_call(
        _even_decode_kernel,
        grid=(batch // bb,),
        in_specs=[
            pl.BlockSpec((bb, 24, LANE), i3),
            pl.BlockSpec((bb, 512), lambda i: (i, 0)),
            pl.BlockSpec((bb, H_B, LANE), i3),
            pl.BlockSpec((bb, H_B, DV_B), i3),
            pl.BlockSpec((bb, H_B, DV_B), i3),
            pl.BlockSpec((bb, 16, LANE), i3),
            pl.BlockSpec((bb, CONV_W - 1, 24, LANE), i4),
            pl.BlockSpec((bb, H_A, DK_A, DV_A), i4),
            pl.BlockSpec((bb, H_B, DK_B, DV_B), i4),
            pl.BlockSpec((bb, H_B, DK_B), i3),
            pl.BlockSpec((bb, LANE), lambda i: (i, 0)),
            pl.BlockSpec((8, LANE), c2),
            pl.BlockSpec((CONV_W, 24, LANE), lambda i: (0, 0, 0)),
            pl.BlockSpec((1, DV_A), c2),
            pl.BlockSpec((1, DV_B), c2),
        ],
        out_specs=[
            pl.BlockSpec((bb, 16, LANE), i3),
            pl.BlockSpec((bb, CONV_W - 1, 24, LANE), i4),
            pl.BlockSpec((bb, H_A, DK_A, DV_A), i4),
            pl.BlockSpec((bb, H_B, DK_B, DV_B), i4),
            pl.BlockSpec((bb, H_B, DK_B), i3),
            pl.BlockSpec((bb, LANE), lambda i: (i, 0)),
        ],
        out_shape=[
            jax.ShapeDtypeStruct((batch, 16, LANE), F32),
            jax.ShapeDtypeStruct((batch, CONV_W - 1, 24, LANE), F32),
            jax.ShapeDtypeStruct((batch, H_A, DK_A, DV_A), F32),
            jax.ShapeDtypeStruct((batch, H_B, DK_B, DV_B), F32),
            jax.ShapeDtypeStruct((batch, H_B, DK_B), F32),
            jax.ShapeDtypeStruct((batch, LANE), F32),
        ],
        compiler_params=pltpu.CompilerParams(
            dimension_semantics=("arbitrary",), vmem_limit_bytes=VMEM_LIMIT),
        name="even_decode",
    )(u, sm, qk, v, op, z, conv, gdn, mc, mn, mm, sp, cw, gnw, mnw)


def _gla_gate_kernel(glr_ref, w2_ref, b2_ref, o_ref):
    o_ref[...] = _log_sigmoid(_dot(glr_ref[...].astype(BF16), w2_ref[...]) + b2_ref[...]) / GLA_TAU


def _gla_gate(glr, w2, b2):
    n = glr.shape[0]
    return pl.pallas_call(
        _gla_gate_kernel,
        out_shape=jax.ShapeDtypeStruct((n, H_C * DK_C), F32),
        name="gla_gate",
    )(glr, w2, b2)


def _odd_decode_kernel(qc_ref, kc_ref, vc_ref, g_ref, qd_ref, kd_ref, vd_ref, z_ref,
                       gla_ref, ret_ref, cc_ref, ss_ref, cnw_ref, dnw_ref,
                       y_ref, gla_o, ret_o):
    head_row = lax.broadcasted_iota(jnp.int32, (H_C + H_D, DV_C), 0)
    nw = jnp.where(head_row < H_C, cnw_ref[...], dnw_ref[...])
    cc = cc_ref[...]
    ss = ss_ref[...]
    gsum = _group_sum_matrix(H_C + H_D)

    def body(b, carry):
        q = qc_ref[b] * (DK_C ** -0.5)
        k = kc_ref[b]
        eg = jnp.exp(g_ref[b])
        t1 = _transpose(jnp.concatenate([q, k], axis=0))
        t2 = _transpose(jnp.concatenate([eg, eg], axis=0))
        v = vc_ref[b]
        parts = []
        for h in range(H_C):
            sn = t2[:, h:h + 1] * gla_ref[b, h] + t1[:, H_C + h:H_C + h + 1] * v[h:h + 1, :]
            gla_o[b, h] = sn
            parts.append(t1[:, h:h + 1] * sn)
        qd = qd_ref[b]
        kd = kd_ref[b]
        qd = qd * cc + pltpu.roll(qd, DK_D // 2, 1) * ss
        kd = (kd * cc + pltpu.roll(kd, DK_D // 2, 1) * ss) * (DK_D ** -0.5)
        t3 = _transpose(jnp.concatenate([qd, kd], axis=0))
        vd = vd_ref[b]
        for h in range(H_D):
            gamma = math.exp(_ret_log_gamma(h))
            sn = gamma * ret_ref[b, h] + t3[:, H_D + h:H_D + h + 1] * vd[h:h + 1, :]
            ret_o[b, h] = sn
            parts.append(t3[:, h:h + 1] * sn)
        o = _sublane_group_sums(parts, gsum)
        y_ref[b] = _rms_rows(o, nw) * _silu(z_ref[b])
        return carry

    lax.fori_loop(0, DEC_BB, body, 0, unroll=2)


def _odd_decode(qc, kc, vc, g, qd, kd, vd, z, gla, ret, cc, ss, cnw, dnw):
    batch = qc.shape[0]
    bb = DEC_BB
    i3 = lambda i: (i, 0, 0)
    i4 = lambda i: (i, 0, 0, 0)
    c2 = lambda i: (0, 0)
    return pl.pallas_call(
        _odd_decode_kernel,
        grid=(batch // bb,),
        in_specs=[
            pl.BlockSpec((bb, H_C, DK_C), i3),
            pl.BlockSpec((bb, H_C, DK_C), i3),
            pl.BlockSpec((bb, H_C, DV_C), i3),
            pl.BlockSpec((bb, H_C, DK_C), i3),
            pl.BlockSpec((bb, H_D, DK_D), i3),
            pl.BlockSpec((bb, H_D, DK_D), i3),
            pl.BlockSpec((bb, H_D, DV_D), i3),
            pl.BlockSpec((bb, 8, DV_C), i3),
            pl.BlockSpec((bb, H_C, DK_C, DV_C), i4),
            pl.BlockSpec((bb, H_D, DK_D, DV_D), i4),
            pl.BlockSpec((1, DK_D), c2),
            pl.BlockSpec((1, DK_D), c2),
            pl.BlockSpec((1, DV_C), c2),
            pl.BlockSpec((1, DV_D), c2),
        ],
        out_specs=[
            pl.BlockSpec((bb, 8, DV_C), i3),
            pl.BlockSpec((bb, H_C, DK_C, DV_C), i4),
            pl.BlockSpec((bb, H_D, DK_D, DV_D), i4),
        ],
        out_shape=[
            jax.ShapeDtypeStruct((batch, 8, DV_C), F32),
            jax.ShapeDtypeStruct((batch, H_C, DK_C, DV_C), F32),
            jax.ShapeDtypeStruct((batch, H_D, DK_D, DV_D), F32),
        ],
        compiler_params=pltpu.CompilerParams(
            dimension_semantics=("arbitrary",), vmem_limit_bytes=VMEM_LIMIT),
        name="odd_decode",
    )(qc, kc, vc, g, qd, kd, vd, z, gla, ret, cc, ss, cnw, dnw)


def _pad_cols(w, width):
    return jnp.pad(w, ((0, 0), (0, width - w.shape[1])))


def _prep_even_w_in(w):
    u = w[:, 0:3072]
    beta = w[:, 3072:3080]
    a = w[:, 3080:3088]
    qb = w[:, 3088:3600].reshape(-1, H_B, DK_B)
    kb = w[:, 3600:4112].reshape(-1, H_B, DK_B)
    vb = w[:, 4112:5136]
    ig = w[:, 5136:5144]
    fg = w[:, 5144:5152]
    op = w[:, 5152:6176]
    z = w[:, 6176:8224]
    qk = jnp.concatenate([qb, kb], axis=-1).reshape(-1, H_B * 2 * DK_B)
    small = jnp.concatenate([_pad_cols(beta, 128), _pad_cols(a, 128), _pad_cols(ig, 128),
                             _pad_cols(fg, 128)], axis=1)
    return jnp.concatenate([u, qk, vb, op, z], axis=1).astype(BF16), small.astype(BF16)


def _prep_odd_w_in(w):
    qc = w[:, 0:512]
    kc = w[:, 512:1024]
    vc = w[:, 1024:2048]
    glr = w[:, 2048:2064]
    qd = w[:, 2064:2576]
    kd = w[:, 2576:3088]
    vd = w[:, 3088:4112]
    z = w[:, 4112:6160]
    return (jnp.concatenate([qc, kc, vc, qd, kd, vd, z], axis=1).astype(BF16),
            _pad_cols(glr, SMALL_O).astype(BF16))


def _lane_row(vec, width=LANE):
    return jnp.pad(vec.astype(F32), (0, width - vec.shape[0]))


def _rotary_tables(pos):
    half = DK_D // 2
    inv = ROPE_BASE ** (-jnp.arange(half, dtype=F32) / half)
    ang = pos.astype(F32)[:, None] * inv[None, :]
    cos = jnp.cos(ang)
    sin = jnp.sin(ang)
    return jnp.concatenate([cos, cos], axis=-1), jnp.concatenate([-sin, sin], axis=-1)


def kernel(x_prompt, x_sample, c_prompt, c_sample, state_gdn, state_gdn_conv, state_mlstm_c,
           state_mlstm_n, state_mlstm_m, state_gla, state_ret, ada_w, ada_b, norm_w, ev_w_in,
           ev_w_out, gdn_conv_w, gdn_a_log, gdn_dt_bias, gdn_norm_w, mlstm_gate_b, mlstm_norm_w,
           od_w_in, od_w_out, gla_w2, gla_b2, gla_norm_w, ret_norm_w, final_norm_w):
    bp, tp, d = x_prompt.shape
    bs = x_sample.shape[0]
    n_p = bp * tp

    w_in_e, w_sm_e = _prep_even_w_in(ev_w_in[0])
    w_in_o, w_sm_o = _prep_odd_w_in(od_w_in[0])
    w_out_e = ev_w_out[0].astype(BF16)
    w_out_o = od_w_out[0].astype(BF16)
    sp = jnp.stack([_lane_row(gdn_dt_bias[0]), _lane_row(gdn_a_log[0]),
                    _lane_row(mlstm_gate_b[0, :H_B]), _lane_row(mlstm_gate_b[0, H_B:])]
                   + [jnp.zeros((LANE,), F32)] * 4)
    gnw = gdn_norm_w[0].reshape(1, DV_A)
    mnw = mlstm_norm_w[0].reshape(1, DV_B)
    cnw = gla_norm_w[0].reshape(1, DV_C)
    dnw = ret_norm_w[0].reshape(1, DV_D)
    w2 = jnp.pad(gla_w2[0], ((0, LANE - GLA_RANK), (0, 0))).astype(BF16)
    b2 = gla_b2[0].reshape(1, H_C * DK_C)

    mod = _modulation(jnp.concatenate([c_prompt, c_sample], axis=0), ada_w, ada_b)
    mod_p = [mod[l, :bp].reshape(bp, 1, 3 * d) for l in range(2)]
    mod_s = [mod[l, bp:].reshape(1, bs, 3 * d) for l in range(2)]

    xp = x_prompt.reshape(n_p, d)
    proj, small = _inproj(xp, mod_p[0], norm_w[0], w_in_e, w_sm_e, 1024, 2048)
    y, p_gdn, p_conv, p_mc, p_mn, p_mm = _even_prompt(
        proj.reshape(bp, tp, P_E), small.reshape(bp, tp, SMALL_E), sp, gdn_conv_w[0], gnw, mnw,
        bp, tp)
    xp = _outproj(y.reshape(n_p, W_MIX), xp, mod_p[0], w_out_e, final_norm_w, 512, False)
    proj, small = _inproj(xp, mod_p[1], norm_w[1], w_in_o, w_sm_o, 1024, 2048)
    cc_p, ss_p = _rotary_tables(jnp.arange(tp))
    y, p_gla, p_ret = _odd_prompt(proj, small, w2, b2, cc_p, ss_p, cnw, dnw, bp, tp)
    y_prompt = _outproj(y, xp, mod_p[1], w_out_o, final_norm_w, 512, True).reshape(bp, tp, d)

    xs = x_sample.reshape(bs, d)
    proj, small = _inproj(xs, mod_s[0], norm_w[0], w_in_e, w_sm_e, bs, 2048)
    proj = proj.astype(F32)
    mm_in = jnp.pad(state_mlstm_m[0], ((0, 0), (0, LANE - H_B)))
    y, s_conv, s_gdn, s_mc, s_mn, s_mm = _even_decode(
        proj[:, E_U:E_U + CONV_CH].reshape(bs, 24, LANE),
        small,
        proj[:, E_QK:E_QK + 1024].reshape(bs, H_B, LANE),
        proj[:, E_V:E_V + 1024].reshape(bs, H_B, DV_B),
        proj[:, E_OP:E_OP + 1024].reshape(bs, H_B, DV_B),
        proj[:, E_Z:E_Z + W_MIX].reshape(bs, 16, LANE),
        state_gdn_conv[0].reshape(bs, CONV_W - 1, 24, LANE),
        state_gdn[0], state_mlstm_c[0], state_mlstm_n[0], mm_in,
        sp, gdn_conv_w[0].reshape(CONV_W, 24, LANE), gnw, mnw)
    xs = _outproj(y.reshape(bs, W_MIX), xs, mod_s[0], w_out_e, final_norm_w, bs, False)
    proj, small = _inproj(xs, mod_s[1], norm_w[1], w_in_o, w_sm_o, bs, 2048)
    proj = proj.astype(F32)
    g_s = _gla_gate(small, w2, b2)
    cc_s, ss_s = _rotary_tables(PAST_LEN + jnp.arange(1))
    y, s_gla, s_ret = _odd_decode(
        proj[:, O_QC:O_QC + 512].reshape(bs, H_C, DK_C),
        proj[:, O_KC:O_KC + 512].reshape(bs, H_C, DK_C),
        proj[:, O_VC:O_VC + 1024].reshape(bs, H_C, DV_C),
        g_s.reshape(bs, H_C, DK_C),
        proj[:, O_QD:O_QD + 512].reshape(bs, H_D, DK_D),
        proj[:, O_KD:O_KD + 512].reshape(bs, H_D, DK_D),
        proj[:, O_VD:O_VD + 1024].reshape(bs, H_D, DV_D),
        proj[:, O_Z:O_Z + W_MIX].reshape(bs, 8, DV_C),
        state_gla[0], state_ret[0], cc_s, ss_s, cnw, dnw)
    y_sample = _outproj(y.reshape(bs, W_MIX), xs, mod_s[1], w_out_o, final_norm_w, bs, True)
    y_sample = y_sample.reshape(bs, 1, d)

    return (y_prompt, y_sample,
            p_gdn[None], p_conv[None], p_mc[None], p_mn[None], p_mm[None, :, 0, :H_B],
            p_gla[None], p_ret[None],
            s_gdn[None], s_conv.reshape(1, bs, CONV_W - 1, CONV_CH), s_mc[None], s_mn[None],
            s_mm[None, :, :H_B], s_gla[None], s_ret[None])
```

```python
import functools
import itertools
import math

import jax
import jax.numpy as jnp
import numpy as np
from jax import lax
from jax.experimental import pallas as pl
from jax.experimental.pallas import tpu as pltpu

F32 = jnp.float32
BF16 = jnp.bfloat16
HI = lax.Precision.HIGHEST

D_MODEL = 1024
W_MIX = 2048
H_A, DK_A, DV_A, CONV_W = 8, 128, 128, 4
CONV_CH = H_A * (2 * DK_A + DV_A)
H_B, DK_B, DV_B = 8, 64, 128
H_C, DK_C, DV_C, GLA_RANK, GLA_TAU = 4, 128, 256, 16, 16.0
H_D, DK_D, DV_D = 4, 128, 256
ROPE_BASE = 10000.0
PAST_LEN = 16384
EPS = 1e-6
LANE = 128

CHUNK = 64
SEQS_PER_STEP = 1
CHUNKS_PER_STEP = 4

E_U, E_QK, E_V, E_OP, E_Z = 0, 3072, 4096, 5120, 6144
P_E = 8192
SMALL_E = 512
O_QC, O_KC, O_VC, O_QD, O_KD, O_VD, O_Z = 0, 512, 1024, 2048, 2560, 3072, 4096
P_O = 6144
SMALL_O = 128

VMEM_LIMIT = 56 * 1024 * 1024


def _sigmoid(x):
    return 0.5 * jnp.tanh(0.5 * x) + 0.5


def _softplus(x):
    return jnp.maximum(x, 0.0) + jnp.log(1.0 + jnp.exp(-jnp.abs(x)))


def _log_sigmoid(x):
    return -_softplus(-x)


def _silu(x):
    half = 0.5 * x
    return half + half * jnp.tanh(half)


def _rms_rows(x, w):
    return x * lax.rsqrt(jnp.mean(x * x, axis=-1, keepdims=True) + EPS) * w


def _l2n(x):
    return x * lax.rsqrt(jnp.sum(x * x, axis=-1, keepdims=True) + EPS)


def _dot(a, b, precision=None):
    return jnp.dot(a, b, preferred_element_type=F32, precision=precision)


def _dot_nt(a, b, precision=None):
    return lax.dot_general(a, b, (((1,), (1,)), ((), ())), preferred_element_type=F32,
                           precision=precision)


def _dot_tn(a, b, precision=None):
    return lax.dot_general(a, b, (((0,), (0,)), ((), ())), preferred_element_type=F32,
                           precision=precision)


def _eye(n):
    r = lax.broadcasted_iota(jnp.int32, (n, n), 0)
    c = lax.broadcasted_iota(jnp.int32, (n, n), 1)
    return (r == c).astype(F32)


def _transpose(x):
    return _dot_nt(_eye(x.shape[1]), x, precision=HI)


def _tri_masks(n):
    r = lax.broadcasted_iota(jnp.int32, (n, n), 0)
    c = lax.broadcasted_iota(jnp.int32, (n, n), 1)
    return r >= c, r > c


def _mod_kernel(c_ref, w_ref, b_ref, o_ref):
    cs = _silu(c_ref[...]).astype(BF16)
    o_ref[...] = _dot(cs, w_ref[...].astype(BF16)) + b_ref[...]


def _modulation(c_all, ada_w, ada_b):
    depth, d, d3 = ada_w.shape
    rows = c_all.shape[0]
    tn = 1024
    return pl.pallas_call(
        _mod_kernel,
        grid=(depth, d3 // tn),
        in_specs=[
            pl.BlockSpec((rows, d), lambda l, j: (0, 0)),
            pl.BlockSpec((None, d, tn), lambda l, j: (l, 0, j)),
            pl.BlockSpec((None, 1, tn), lambda l, j: (l, 0, j)),
        ],
        out_specs=pl.BlockSpec((None, rows, tn), lambda l, j: (l, 0, j)),
        out_shape=jax.ShapeDtypeStruct((depth, rows, d3), F32),
        compiler_params=pltpu.CompilerParams(
            dimension_semantics=("arbitrary", "arbitrary"), vmem_limit_bytes=VMEM_LIMIT),
        name="modulation",
    )(c_all, ada_w, ada_b.reshape(depth, 1, d3))


def _inproj_kernel(x_ref, mod_ref, nw_ref, w_ref, ws_ref, o_ref, os_ref, h_ref):
    @pl.when(pl.program_id(1) == 0)
    def _():
        x = x_ref[...]
        y = _rms_rows(x, nw_ref[...])
        shift = mod_ref[:, 0:D_MODEL]
        scale = mod_ref[:, D_MODEL:2 * D_MODEL]
        h_ref[...] = (y * (1.0 + scale) + shift).astype(BF16)
        os_ref[...] = _dot(h_ref[...], ws_ref[...])

    o_ref[...] = _dot(h_ref[...], w_ref[...]).astype(o_ref.dtype)


def _inproj(x, mod, norm_w, w, w_small, tm, tn):
    n, d = x.shape
    p = w.shape[1]
    ps = w_small.shape[1]
    g, r, _ = mod.shape
    tiles_per_group = (n // g) // tm
    return pl.pallas_call(
        _inproj_kernel,
        grid=(n // tm, p // tn),
        in_specs=[
            pl.BlockSpec((tm, d), lambda i, j: (i, 0)),
            pl.BlockSpec((None, r, 3 * d), lambda i, j: (i // tiles_per_group, 0, 0)),
            pl.BlockSpec((1, d), lambda i, j: (0, 0)),
            pl.BlockSpec((d, tn), lambda i, j: (0, j)),
            pl.BlockSpec((d, ps), lambda i, j: (0, 0)),
        ],
        out_specs=[pl.BlockSpec((tm, tn), lambda i, j: (i, j)),
                   pl.BlockSpec((tm, ps), lambda i, j: (i, 0))],
        out_shape=[jax.ShapeDtypeStruct((n, p), BF16), jax.ShapeDtypeStruct((n, ps), F32)],
        scratch_shapes=[pltpu.VMEM((tm, d), BF16)],
        compiler_params=pltpu.CompilerParams(
            dimension_semantics=("arbitrary", "arbitrary"), vmem_limit_bytes=VMEM_LIMIT),
        name="inproj",
    )(x, mod, norm_w.reshape(1, d), w, w_small)


def _outproj_kernel(y_ref, x_ref, mod_ref, w_ref, fw_ref, o_ref, *, final):
    acc = _dot(y_ref[...].astype(BF16), w_ref[...])
    gate = mod_ref[:, 2 * D_MODEL:3 * D_MODEL]
    xn = x_ref[...] + gate * acc
    if final:
        xn = _rms_rows(xn, fw_ref[...])
    o_ref[...] = xn


def _outproj(y, x, mod, w, final_w, tm, final):
    n, d = x.shape
    k = y.shape[1]
    g, r, _ = mod.shape
    tiles_per_group = (n // g) // tm
    return pl.pallas_call(
        functools.partial(_outproj_kernel, final=final),
        grid=(n // tm,),
        in_specs=[
            pl.BlockSpec((tm, k), lambda i: (i, 0)),
            pl.BlockSpec((tm, d), lambda i: (i, 0)),
            pl.BlockSpec((None, r, 3 * d), lambda i: (i // tiles_per_group, 0, 0)),
            pl.BlockSpec((k, d), lambda i: (0, 0)),
            pl.BlockSpec((1, d), lambda i: (0, 0)),
        ],
        out_specs=pl.BlockSpec((tm, d), lambda i: (i, 0)),
        out_shape=jax.ShapeDtypeStruct((n, d), F32),
        compiler_params=pltpu.CompilerParams(
            dimension_semantics=("arbitrary",), vmem_limit_bytes=VMEM_LIMIT),
        name="outproj",
    )(y, x, mod, w, final_w.reshape(1, d))


def _neumann_correction(a_list, n):
    ys = [-a for a in a_list]
    pbs = [a.astype(BF16) for a in a_list]
    for _ in range(int(math.log2(n)) - 1):
        for i in range(len(ys)):
            p = _dot(pbs[i], pbs[i])
            pbs[i] = p.astype(BF16)
            ys[i] = ys[i] + p + _dot(ys[i].astype(BF16), pbs[i])
        yield
    idx = range(len(ys))
    a_h = [a.astype(BF16) for a in a_list]
    a_l = [(a_list[i] - a_h[i].astype(F32)).astype(BF16) for i in idx]
    y_h = [y.astype(BF16) for y in ys]
    y_l = [(ys[i] - y_h[i].astype(F32)).astype(BF16) for i in idx]
    ay = [_dot(a_h[i], y_h[i]) + _dot(a_h[i], y_l[i]) + _dot(a_l[i], y_h[i]) for i in idx]
    e = [-(a_list[i] + ys[i] + ay[i]) for i in idx]
    return [ys[i] + e[i] + _dot(y_h[i], e[i].astype(BF16)) for i in idx]


def _run_interleaved(stage_generators):
    for _ in itertools.zip_longest(*stage_generators):
        pass


def _conv_shift_matrix():
    n = CHUNK
    r = np.arange((CONV_W - 1) * n)[:, None]
    col = np.arange(2 * n)[None, :]
    return jnp.asarray(col == n + (r % n) - (r // n + 1), BF16)


def _even_prompt_kernel(proj_ref, sm_ref, sp_ref, cw_ref, shift_ref, gnw_ref, mnw_ref,
                        y_ref, gdn_ref, conv_ref, mc_ref, mn_ref, mm_ref,
                        uprev_ref, qkv_ref):
    for j in range(CHUNKS_PER_STEP):
        rows = pl.ds(j * CHUNK, CHUNK)
        _run_interleaved([
            _even_prompt_sequence(j == 0, proj_ref.at[i, rows], sm_ref.at[i, rows], sp_ref, cw_ref,
                                  shift_ref, gnw_ref, mnw_ref, y_ref.at[i, rows], gdn_ref.at[i],
                                  conv_ref.at[i], mc_ref.at[i], mn_ref.at[i], mm_ref.at[i],
                                  uprev_ref.at[i], qkv_ref.at[i])
            for i in range(SEQS_PER_STEP)])


def _even_prompt_sequence(first_in_step, proj_ref, sm_ref, sp_ref, cw_ref, shift_ref, gnw_ref,
                          mnw_ref, y_ref, gdn_ref, conv_ref, mc_ref, mn_ref, mm_ref,
                          uprev_ref, qkv_ref):
    L = CHUNK

    if first_in_step:
        @pl.when(pl.program_id(1) == 0)
        def _():
            gdn_ref[...] = jnp.zeros_like(gdn_ref)
            mc_ref[...] = jnp.zeros_like(mc_ref)
            mn_ref[...] = jnp.zeros_like(mn_ref)
            mm_ref[...] = jnp.zeros_like(mm_ref)
            uprev_ref[...] = jnp.zeros_like(uprev_ref)

    tri_incl, tri_strict = _tri_masks(L)
    tri_f = tri_incl.astype(F32)

    ub = proj_ref[:, E_U:E_U + CONV_CH]
    shifted = _dot(shift_ref[...], jnp.concatenate([uprev_ref[...], ub], axis=0))
    uprev_ref[...] = ub
    u = ub.astype(F32)
    cw = cw_ref[...]
    uc = (cw[0:1] * shifted[2 * L:3 * L] + cw[1:2] * shifted[L:2 * L]
          + cw[2:3] * shifted[0:L] + cw[3:4] * u)
    conv_ref[...] = u[L - 3:L, :]
    qkv_ref[...] = _silu(uc)
    yield

    sm = sm_ref[...]
    head_lane = lax.broadcasted_iota(jnp.int32, (L, LANE), 1) < H_A
    beta = _sigmoid(sm[:, 0:128])
    g = -jnp.exp(sp_ref[1:2, :]) * _softplus(sm[:, 128:256] + sp_ref[0:1, :])
    ig = sm[:, 256:384] + sp_ref[2:3, :]
    lf = _log_sigmoid(sm[:, 384:512] + sp_ref[3:4, :])
    packed = jnp.where(head_lane, g, 0.0) + pltpu.roll(jnp.where(head_lane, lf, 0.0), H_A, 1)
    cum = _dot(tri_f, packed, precision=HI)
    cum_t = _transpose(cum + pltpu.roll(jnp.where(head_lane, ig, 0.0), H_A + H_B, 1))

    gnw = gnw_ref[...]
    mnw = mnw_ref[...]
    yield

    qkb, kf, decay, beta_c, gam, bh_c = [], [], [], [], [], []
    for h in range(H_A):
        q = _l2n(qkv_ref[:, h * DK_A:(h + 1) * DK_A]) * (DK_A ** -0.5)
        k = _l2n(qkv_ref[:, H_A * DK_A + h * DK_A:H_A * DK_A + (h + 1) * DK_A])
        kf.append(k)
        qkb.append(jnp.concatenate([q, k], axis=0).astype(BF16))
    yield
    qkk = [_dot_nt(qkb[h], qkb[h][L:2 * L]) for h in range(H_A)]
    yield
    a_list = []
    for h in range(H_A):
        bh_c.append(cum[:, h:h + 1])
        diff = bh_c[h] - cum_t[h:h + 1, :]
        decay.append(jnp.where(tri_incl, jnp.exp(jnp.where(tri_incl, diff, 0.0)), 0.0))
        beta_c.append(beta[:, h:h + 1])
        gam.append(jnp.exp(bh_c[h]))
        a_list.append(jnp.where(tri_strict, decay[h] * qkk[h][L:2 * L], 0.0) * beta_c[h])
    yield
    corr = yield from _neumann_correction(a_list, L)
    yield

    heads = range(H_A)
    s_old = [gdn_ref[h] for h in heads]
    qks = [_dot(qkb[h], s_old[h].astype(BF16)) for h in heads]
    yield
    rhs = [beta_c[h] * (qkv_ref[:, 2 * H_A * DK_A + h * DV_A:2 * H_A * DK_A + (h + 1) * DV_A]
                        - gam[h] * qks[h][L:2 * L]) for h in heads]
    ub = [(rhs[h] + _dot(corr[h].astype(BF16), rhs[h].astype(BF16))).astype(BF16) for h in heads]
    yield
    o = [gam[h] * qks[h][0:L] + _dot((qkk[h][0:L] * decay[h]).astype(BF16), ub[h])
         for h in heads]
    yield
    for h in heads:
        last = bh_c[h][L - 1:L, :]
        kw = (kf[h] * jnp.exp(last - bh_c[h])).astype(BF16)
        gdn_ref[h] = jnp.exp(last) * s_old[h] + _dot_tn(kw, ub[h])
    yield
    for h in heads:
        z = proj_ref[:, E_Z + h * DV_A:E_Z + (h + 1) * DV_A].astype(F32)
        y_ref[:, h * DV_A:(h + 1) * DV_A] = (_rms_rows(o[h], gnw) * _silu(z)).astype(BF16)
    yield

    heads = range(H_B)
    qs, qbs, kbs, vbs = [], [], [], []
    for h in heads:
        qk_in = proj_ref[:, E_QK + h * 128:E_QK + (h + 1) * 128]
        qbs.append(qk_in[:, 0:DK_B])
        qs.append(qbs[h].astype(F32))
        kbs.append((qk_in[:, DK_B:2 * DK_B].astype(F32) * (DK_B ** -0.5)).astype(BF16))
        vbs.append(proj_ref[:, E_V + h * DV_B:E_V + (h + 1) * DV_B])
    qkm = [_dot_nt(qbs[h], kbs[h]) for h in heads]
    yield
    b_c = [cum[:, H_A + h:H_A + h + 1] for h in heads]
    m0 = [mm_ref[0:1, h:h + 1] for h in heads]
    dmat = [jnp.where(tri_incl, b_c[h] - cum_t[H_A + h:H_A + h + 1, :]
                      + cum_t[H_A + H_B + h:H_A + H_B + h + 1, :], -jnp.inf) for h in heads]
    w0 = [b_c[h] + m0[h] for h in heads]
    m_t = [jnp.maximum(w0[h], jnp.max(dmat[h], axis=-1, keepdims=True)) for h in heads]
    p = [jnp.exp(dmat[h] - m_t[h]) * qkm[h] for h in heads]
    s0 = [jnp.exp(w0[h] - m_t[h]) for h in heads]
    yield
    c_old = [mc_ref[h] for h in heads]
    n_old = [mn_ref[h:h + 1, :] for h in heads]
    num = [s0[h] * _dot(qbs[h], c_old[h].astype(BF16)) + _dot(p[h].astype(BF16), vbs[h])
           for h in heads]
    den = [s0[h] * jnp.sum(qs[h] * n_old[h], axis=-1, keepdims=True)
           + jnp.sum(p[h], axis=-1, keepdims=True) for h in heads]
    hh = [num[h] / jnp.maximum(jnp.abs(den[h]), jnp.exp(-m_t[h])) for h in heads]
    yield
    for h in heads:
        m_end = m_t[h][L - 1:L, :]
        b_last = b_c[h][L - 1:L, :]
        we = jnp.exp(b_last - b_c[h] + ig[:, h:h + 1] - m_end)
        se = jnp.exp(b_last + m0[h] - m_end)
        kw = kbs[h].astype(F32) * we
        mc_ref[h] = se * c_old[h] + _dot_tn(kw.astype(BF16), vbs[h])
        mn_ref[h:h + 1, :] = se * n_old[h] + jnp.sum(kw, axis=0, keepdims=True)
        mm_ref[0:1, h:h + 1] = m_end
    yield
    for h in heads:
        o_pre = proj_ref[:, E_OP + h * DV_B:E_OP + (h + 1) * DV_B].astype(F32)
        z = proj_ref[:, E_Z + H_A * DV_A + h * DV_B:
                     E_Z + H_A * DV_A + (h + 1) * DV_B].astype(F32)
        y_ref[:, H_A * DV_A + h * DV_B:H_A * DV_A + (h + 1) * DV_B] = (
            _sigmoid(o_pre) * _rms_rows(hh[h], mnw) * _silu(z)).astype(BF16)


def _even_prompt(proj, small, sp, conv_w, gnw, mnw, batch, seq):
    step = CHUNK * CHUNKS_PER_STEP
    nc = seq // step
    nb = SEQS_PER_STEP
    row = lambda b, c: (b, c, 0)
    const2 = lambda b, c: (0, 0)
    st3 = lambda b, c: (b, 0, 0)
    st4 = lambda b, c: (b, 0, 0, 0)
    return pl.pallas_call(
        _even_prompt_kernel,
        grid=(batch // nb, nc),
        in_specs=[
            pl.BlockSpec((nb, step, P_E), row),
            pl.BlockSpec((nb, step, SMALL_E), row),
            pl.BlockSpec((8, LANE), const2),
            pl.BlockSpec((CONV_W, CONV_CH), const2),
            pl.BlockSpec(((CONV_W - 1) * CHUNK, 2 * CHUNK), const2),
            pl.BlockSpec((1, DV_A), const2),
            pl.BlockSpec((1, DV_B), const2),
        ],
        out_specs=[
            pl.BlockSpec((nb, step, W_MIX), row),
            pl.BlockSpec((nb, H_A, DK_A, DV_A), st4),
            pl.BlockSpec((nb, CONV_W - 1, CONV_CH), st3),
            pl.BlockSpec((nb, H_B, DK_B, DV_B), st4),
            pl.BlockSpec((nb, H_B, DK_B), st3),
            pl.BlockSpec((nb, 1, LANE), st3),
        ],
        out_shape=[
            jax.ShapeDtypeStruct((batch, seq, W_MIX), BF16),
            jax.ShapeDtypeStruct((batch, H_A, DK_A, DV_A), F32),
            jax.ShapeDtypeStruct((batch, CONV_W - 1, CONV_CH), F32),
            jax.ShapeDtypeStruct((batch, H_B, DK_B, DV_B), F32),
            jax.ShapeDtypeStruct((batch, H_B, DK_B), F32),
            jax.ShapeDtypeStruct((batch, 1, LANE), F32),
        ],
        scratch_shapes=[pltpu.VMEM((nb, CHUNK, CONV_CH), BF16),
                        pltpu.VMEM((nb, CHUNK, CONV_CH), F32)],
        compiler_params=pltpu.CompilerParams(
            dimension_semantics=("arbitrary", "arbitrary"), vmem_limit_bytes=VMEM_LIMIT),
        name="even_prompt",
    )(proj, small, sp, conv_w, _conv_shift_matrix(), gnw, mnw)


def _ret_log_gamma(h):
    return math.log(1.0 - 2.0 ** (-5.0 - h))


GLA_LEVELS = int(math.log2(CHUNK))
GLA_MM_LEVELS = (1, 2)
LOG2E = math.log2(math.e)


def _gla_chunk_tables():
    n = CHUNK
    t = np.arange(n)[:, None]
    i = np.arange(n)[None, :]
    blocks = [(i <= t)]
    for lvl in GLA_MM_LEVELS:
        hs = 1 << lvl
        p = (t & ~(2 * hs - 1)) + hs - 1
        upper = (t & hs) != 0
        blocks.append(np.where(upper, (i > p) & (i <= t), (i > t) & (i <= p)))
    sums = np.concatenate(blocks, axis=0).astype(np.float32)
    s = np.arange(n)[None, :]
    level = np.full((n, n), GLA_LEVELS + 1, np.int32)
    level[np.arange(n), np.arange(n)] = GLA_LEVELS
    for lvl in range(GLA_LEVELS):
        hs = 1 << lvl
        same = (t >> (lvl + 1)) == (s >> (lvl + 1))
        level[same & ((t & hs) != 0) & ((s & hs) == 0)] = lvl
    return jnp.asarray(sums), jnp.asarray(level)


def _odd_prompt_kernel(proj_ref, glr_ref, w2_ref, b2_ref, sums_ref, level_ref, cc_ref, ss_ref,
                       cnw_ref, dnw_ref, y_ref, gla_ref, ret_ref, st_ref, x_ref):
    for j in range(CHUNKS_PER_STEP):
        rows = pl.ds(j * CHUNK, CHUNK)
        _odd_prompt_chunk(j == 0, j == CHUNKS_PER_STEP - 1, proj_ref.at[rows], glr_ref.at[rows],
                          w2_ref, b2_ref, sums_ref, level_ref, cc_ref.at[rows], ss_ref.at[rows],
                          cnw_ref, dnw_ref, y_ref.at[rows], gla_ref, ret_ref, st_ref, x_ref)


def _odd_prompt_chunk(first_in_step, last_in_step, proj_ref, glr_ref, w2_ref, b2_ref, sums_ref,
                      level_ref, cc_ref, ss_ref, cnw_ref, dnw_ref, y_ref, gla_ref, ret_ref,
                      st_ref, x_ref):
    L = CHUNK
    c = pl.program_id(1)
    nc = pl.num_programs(1)

    if first_in_step:
        @pl.when(c == 0)
        def _():
            st_ref[...] = jnp.zeros_like(st_ref)
            ret_ref[...] = jnp.zeros_like(ret_ref)

    tri_incl, _ = _tri_masks(L)
    r64 = lax.broadcasted_iota(jnp.int32, (L, L), 0)
    c64 = lax.broadcasted_iota(jnp.int32, (L, L), 1)
    level = level_ref[...]

    glr = glr_ref[...].astype(BF16)
    gc2 = _log_sigmoid(_dot(glr, w2_ref[...]) + b2_ref[...]) * (LOG2E / GLA_TAU)
    x_ref[...] = _dot(sums_ref[...], gc2, precision=HI)
    odd_row = (lax.broadcasted_iota(jnp.int32, (L, DK_C), 0) & 1) == 1

    def level_exponent(lvl, h):
        cols = slice(h * DK_C, (h + 1) * DK_C)
        if lvl == 0:
            return jnp.where(odd_row, gc2[:, cols], 0.0)
        if lvl in GLA_MM_LEVELS:
            r0 = (1 + GLA_MM_LEVELS.index(lvl)) * L
            return x_ref[r0:r0 + L, cols]
        hs = 1 << lvl
        pieces = []
        for lo in range(0, L, 2 * hs):
            ref = x_ref[lo + hs - 1:lo + hs, cols]
            pieces.append(ref - x_ref[lo:lo + hs, cols])
            pieces.append(x_ref[lo + hs:lo + 2 * hs, cols] - ref)
        return jnp.concatenate(pieces, axis=0)

    cnw = cnw_ref[...]
    dnw = dnw_ref[...]

    heads = range(H_C)
    q = [proj_ref[:, O_QC + h * DK_C:O_QC + (h + 1) * DK_C].astype(F32) * (DK_C ** -0.5)
         for h in heads]
    k = [proj_ref[:, O_KC + h * DK_C:O_KC + (h + 1) * DK_C].astype(F32) for h in heads]
    v = [proj_ref[:, O_VC + h * DV_C:O_VC + (h + 1) * DV_C] for h in heads]
    att = [jnp.where(level == GLA_LEVELS, _dot_nt(q[h].astype(BF16), k[h].astype(BF16)), 0.0)
           for h in heads]
    for lvl in range(GLA_LEVELS):
        for h in heads:
            e = jnp.exp2(level_exponent(lvl, h))
            m = _dot_nt((q[h] * e).astype(BF16), (k[h] * e).astype(BF16))
            att[h] = jnp.where(level == lvl, m, att[h])
    b2 = [x_ref[0:L, h * DK_C:(h + 1) * DK_C] for h in heads]
    st_old = [st_ref[h] for h in heads]
    o = [_dot_nt((q[h] * jnp.exp2(b2[h])).astype(BF16), st_old[h].astype(BF16))
         + _dot(att[h].astype(BF16), v[h]) for h in heads]
    for h in heads:
        bl = b2[h][L - 1:L, :]
        kw = (k[h] * jnp.exp2(bl - b2[h])).astype(BF16)
        st_ref[h] = st_old[h] * jnp.exp2(bl) + _dot_tn(v[h], kw)
    for h in heads:
        z = proj_ref[:, O_Z + h * DV_C:O_Z + (h + 1) * DV_C].astype(F32)
        y_ref[:, h * DV_C:(h + 1) * DV_C] = (_rms_rows(o[h], cnw) * _silu(z)).astype(BF16)

    if last_in_step:
        @pl.when(c == nc - 1)
        def _():
            for h in range(H_C):
                st = st_ref[h]
                gla_ref[h, :, 0:128] = _transpose(st[0:128, :])
                gla_ref[h, :, 128:256] = _transpose(st[128:256, :])

    cc = cc_ref[...]
    ss = ss_ref[...]
    rel = (r64 - c64).astype(F32)
    tcol = lax.broadcasted_iota(jnp.int32, (L, 1), 0).astype(F32)
    heads = range(H_D)
    lg = [_ret_log_gamma(h) for h in heads]
    qb, kf, kb, vd = [], [], [], []
    for h in heads:
        qd = proj_ref[:, O_QD + h * DK_D:O_QD + (h + 1) * DK_D].astype(F32)
        kd = proj_ref[:, O_KD + h * DK_D:O_KD + (h + 1) * DK_D].astype(F32)
        qb.append((qd * cc + pltpu.roll(qd, DK_D // 2, 1) * ss).astype(BF16))
        kf.append((kd * cc + pltpu.roll(kd, DK_D // 2, 1) * ss) * (DK_D ** -0.5))
        kb.append(kf[h].astype(BF16))
        vd.append(proj_ref[:, O_VD + h * DV_D:O_VD + (h + 1) * DV_D])
    s_old = [ret_ref[h] for h in heads]
    attd = [_dot_nt(qb[h], kb[h])
            * jnp.where(tri_incl, jnp.exp(lg[h] * jnp.maximum(rel, 0.0)), 0.0) for h in heads]
    od = [_dot(qb[h], s_old[h].astype(BF16)) * jnp.exp(lg[h] * (tcol + 1.0))
          + _dot(attd[h].astype(BF16), vd[h]) for h in heads]
    for h in heads:
        end = jnp.exp(lg[h] * (L - 1.0 - tcol))
        ret_ref[h] = math.exp(lg[h] * L) * s_old[h] + _dot_tn((kf[h] * end).astype(BF16), vd[h])
    for h in heads:
        z = proj_ref[:, O_Z + H_C * DV_C + h * DV_D:
                     O_Z + H_C * DV_C + (h + 1) * DV_D].astype(F32)
        y_ref[:, H_C * DV_C + h * DV_D:H_C * DV_C + (h + 1) * DV_D] = (
            _rms_rows(od[h], dnw) * _silu(z)).astype(BF16)


def _odd_prompt(proj, glr, w2, b2, cc, ss, cnw, dnw, batch, seq):
    step = CHUNK * CHUNKS_PER_STEP
    nc = seq // step
    row = lambda b, c: (b * nc + c, 0)
    const2 = lambda b, c: (0, 0)
    sums, level = _gla_chunk_tables()
    n_sums = sums.shape[0]
    return pl.pallas_call(
        _odd_prompt_kernel,
        grid=(batch, nc),
        in_specs=[
            pl.BlockSpec((step, P_O), row),
            pl.BlockSpec((step, SMALL_O), row),
            pl.BlockSpec((LANE, H_C * DK_C), const2),
            pl.BlockSpec((1, H_C * DK_C), const2),
            pl.BlockSpec((n_sums, CHUNK), const2),
            pl.BlockSpec((CHUNK, CHUNK), const2),
            pl.BlockSpec((step, DK_D), lambda b, c: (c, 0)),
            pl.BlockSpec((step, DK_D), lambda b, c: (c, 0)),
            pl.BlockSpec((1, DV_C), const2),
            pl.BlockSpec((1, DV_D), const2),
        ],
        out_specs=[
            pl.BlockSpec((step, W_MIX), row),
            pl.BlockSpec((None, H_C, DK_C, DV_C), lambda b, c: (b, 0, 0, 0)),
            pl.BlockSpec((None, H_D, DK_D, DV_D), lambda b, c: (b, 0, 0, 0)),
        ],
        out_shape=[
            jax.ShapeDtypeStruct((batch * seq, W_MIX), BF16),
            jax.ShapeDtypeStruct((batch, H_C, DK_C, DV_C), F32),
            jax.ShapeDtypeStruct((batch, H_D, DK_D, DV_D), F32),
        ],
        scratch_shapes=[pltpu.VMEM((H_C, DV_C, DK_C), F32),
                        pltpu.VMEM((n_sums, H_C * DK_C), F32)],
        compiler_params=pltpu.CompilerParams(
            dimension_semantics=("arbitrary", "arbitrary"), vmem_limit_bytes=VMEM_LIMIT),
        name="odd_prompt",
    )(proj, glr, w2, b2, sums, level, cc, ss, cnw, dnw)


DEC_BB = 8


def _row_select(rows, row_index, acc, new):
    return jnp.where(rows == row_index, new, acc)


def _pad_rows16(x):
    return jnp.concatenate([x, jnp.zeros_like(x)], axis=0).astype(BF16)


def _transpose_bf16(x):
    eye = _eye(x.shape[1]).astype(BF16)
    return _dot_nt(eye, _pad_rows16(x)).astype(BF16)


def _even_sample_kernel(proj_ref, sm_ref, conv_ref, gdn_ref, mc_ref, mn_ref, mm_ref,
                        sp_ref, cw_ref, gnw_ref, mnw_ref,
                        y_ref, conv_o, gdn_o, mc_o, mn_o, mm_o, act_ref):
    nb = DEC_BB
    gnw = gnw_ref[...]
    mnw = mnw_ref[...]
    rows = lax.broadcasted_iota(jnp.int32, (nb, LANE), 0)

    sm = sm_ref[...]
    beta = _sigmoid(sm[:, 0:128])
    gam = jnp.exp(-jnp.exp(sp_ref[1:2, :]) * _softplus(sm[:, 128:256] + sp_ref[0:1, :]))
    ig = sm[:, 256:384] + sp_ref[2:3, :]
    lf = _log_sigmoid(sm[:, 384:512] + sp_ref[3:4, :])
    w0 = lf + mm_ref[...]
    m_t = jnp.maximum(w0, ig)
    s0 = jnp.exp(w0 - m_t)
    we = jnp.exp(ig - m_t)
    floor = jnp.exp(-m_t)
    mm_o[...] = m_t

    u = proj_ref[:, E_U:E_U + CONV_CH].astype(F32)
    cw = cw_ref[...]
    uc = (cw[0:1] * conv_ref[0] + cw[1:2] * conv_ref[1] + cw[2:3] * conv_ref[2] + cw[3:4] * u)
    conv_o[0] = conv_ref[1]
    conv_o[1] = conv_ref[2]
    conv_o[2] = u
    act_ref[...] = _silu(uc)

    for h in range(H_A):
        q = _l2n(act_ref[:, h * DK_A:(h + 1) * DK_A]) * (DK_A ** -0.5)
        k = _l2n(act_ref[:, H_A * DK_A + h * DK_A:H_A * DK_A + (h + 1) * DK_A])
        v = act_ref[:, 2 * H_A * DK_A + h * DV_A:2 * H_A * DK_A + (h + 1) * DV_A]
        kq = jnp.concatenate([k, q], axis=0).astype(BF16)
        kt = _transpose_bf16(k)
        ks = jnp.zeros((nb, DV_A), F32)
        qs = jnp.zeros((nb, DV_A), F32)
        for b in range(nb):
            r = _dot(kq, gdn_ref[b, h].astype(BF16))
            ks = _row_select(rows, b, ks, r[0:nb])
            qs = _row_select(rows, b, qs, r[nb:2 * nb])
        gam_h = gam[:, h:h + 1]
        uu = beta[:, h:h + 1] * (v - gam_h * ks)
        o = gam_h * qs + jnp.sum(q * k, axis=-1, keepdims=True) * uu
        for b in range(nb):
            usel = _pad_rows16(jnp.where(rows == b, uu, 0.0))
            gdn_o[b, h] = gam[b:b + 1, h:h + 1] * gdn_ref[b, h] + _dot(kt, usel)
        z = proj_ref[:, E_Z + h * DV_A:E_Z + (h + 1) * DV_A].astype(F32)
        y_ref[:, h * DV_A:(h + 1) * DV_A] = _rms_rows(o, gnw) * _silu(z)

    for h in range(H_B):
        qk_in = proj_ref[:, E_QK + h * 128:E_QK + (h + 1) * 128]
        qb = qk_in[:, 0:DK_B]
        q = qb.astype(F32)
        k = qk_in[:, DK_B:2 * DK_B].astype(F32) * (DK_B ** -0.5)
        v = proj_ref[:, E_V + h * DV_B:E_V + (h + 1) * DV_B].astype(F32)
        kt = _transpose_bf16(k)
        q16 = _pad_rows16(qb)
        qc = jnp.zeros((nb, DV_B), F32)
        for b in range(nb):
            r = _dot(q16, mc_ref[b, h].astype(BF16))
            qc = _row_select(rows, b, qc, r[0:nb])
        s0_h = s0[:, h:h + 1]
        we_h = we[:, h:h + 1]
        wv = we_h * v
        nn = s0_h * mn_ref[h] + we_h * k
        mn_o[h] = nn
        num = s0_h * qc + jnp.sum(q * k, axis=-1, keepdims=True) * wv
        den = jnp.sum(q * nn, axis=-1, keepdims=True)
        hh = num / jnp.maximum(jnp.abs(den), floor[:, h:h + 1])
        for b in range(nb):
            wsel = _pad_rows16(jnp.where(rows == b, wv, 0.0))
            mc_o[b, h] = s0[b:b + 1, h:h + 1] * mc_ref[b, h] + _dot(kt, wsel)
        o_pre = proj_ref[:, E_OP + h * DV_B:E_OP + (h + 1) * DV_B].astype(F32)
        z = proj_ref[:, E_Z + H_A * DV_A + h * DV_B:
                     E_Z + H_A * DV_A + (h + 1) * DV_B].astype(F32)
        y_ref[:, H_A * DV_A + h * DV_B:H_A * DV_A + (h + 1) * DV_B] = (
            _sigmoid(o_pre) * _rms_rows(hh, mnw) * _silu(z))


def _even_sample(proj, small, conv, gdn, mc, mn, mm, sp, cw, gnw, mnw):
    batch = proj.shape[0]
    nb = DEC_BB
    r2 = lambda i: (i, 0)
    m3 = lambda i: (0, i, 0)
    i4 = lambda i: (i, 0, 0, 0)
    c2 = lambda i: (0, 0)
    return pl.pallas_call(
        _even_sample_kernel,
        grid=(batch // nb,),
        in_specs=[
            pl.BlockSpec((nb, P_E), r2),
            pl.BlockSpec((nb, SMALL_E), r2),
            pl.BlockSpec((CONV_W - 1, nb, CONV_CH), m3),
            pl.BlockSpec((nb, H_A, DK_A, DV_A), i4),
            pl.BlockSpec((nb, H_B, DK_B, DV_B), i4),
            pl.BlockSpec((H_B, nb, DK_B), m3),
            pl.BlockSpec((nb, LANE), r2),
            pl.BlockSpec((8, LANE), c2),
            pl.BlockSpec((CONV_W, CONV_CH), c2),
            pl.BlockSpec((1, DV_A), c2),
            pl.BlockSpec((1, DV_B), c2),
        ],
        out_specs=[
            pl.BlockSpec((nb, W_MIX), r2),
            pl.BlockSpec((CONV_W - 1, nb, CONV_CH), m3),
            pl.BlockSpec((nb, H_A, DK_A, DV_A), i4),
            pl.BlockSpec((nb, H_B, DK_B, DV_B), i4),
            pl.BlockSpec((H_B, nb, DK_B), m3),
            pl.BlockSpec((nb, LANE), r2),
        ],
        out_shape=[
            jax.ShapeDtypeStruct((batch, W_MIX), F32),
            jax.ShapeDtypeStruct((CONV_W - 1, batch, CONV_CH), F32),
            jax.ShapeDtypeStruct((batch, H_A, DK_A, DV_A), F32),
            jax.ShapeDtypeStruct((batch, H_B, DK_B, DV_B), F32),
            jax.ShapeDtypeStruct((H_B, batch, DK_B), F32),
            jax.ShapeDtypeStruct((batch, LANE), F32),
        ],
        scratch_shapes=[pltpu.VMEM((nb, CONV_CH), F32)],
        compiler_params=pltpu.CompilerParams(
            dimension_semantics=("arbitrary",), vmem_limit_bytes=VMEM_LIMIT),
        name="even_sample",
    )(proj, small, conv, gdn, mc, mn, mm, sp, cw, gnw, mnw)


def _odd_sample_kernel(proj_ref, glr_ref, w2_ref, b2_ref, gla_ref, ret_ref, cc_ref, ss_ref,
                       cnw_ref, dnw_ref, y_ref, gla_o, ret_o):
    nb = DEC_BB
    cnw = cnw_ref[...]
    dnw = dnw_ref[...]
    cc = cc_ref[...]
    ss = ss_ref[...]
    rows = lax.broadcasted_iota(jnp.int32, (nb, DV_C), 0)
    g = _log_sigmoid(_dot(glr_ref[...].astype(BF16), w2_ref[...]) + b2_ref[...]) / GLA_TAU

    for h in range(H_C):
        q = proj_ref[:, O_QC + h * DK_C:O_QC + (h + 1) * DK_C].astype(F32) * (DK_C ** -0.5)
        k = proj_ref[:, O_KC + h * DK_C:O_KC + (h + 1) * DK_C]
        v = proj_ref[:, O_VC + h * DV_C:O_VC + (h + 1) * DV_C].astype(F32)
        eg = jnp.exp(g[:, h * DK_C:(h + 1) * DK_C])
        qe = _pad_rows16(q * eg)
        kt = _transpose_bf16(k)
        egt = _transpose(eg)
        qs = jnp.zeros((nb, DV_C), F32)
        for b in range(nb):
            qs = _row_select(rows, b, qs, _dot(qe, gla_ref[b, h].astype(BF16))[0:nb])
        o = qs + jnp.sum(q * k.astype(F32), axis=-1, keepdims=True) * v
        for b in range(nb):
            vsel = _pad_rows16(jnp.where(rows == b, v, 0.0))
            gla_o[b, h] = egt[:, b:b + 1] * gla_ref[b, h] + _dot(kt, vsel)
        z = proj_ref[:, O_Z + h * DV_C:O_Z + (h + 1) * DV_C].astype(F32)
        y_ref[:, h * DV_C:(h + 1) * DV_C] = _rms_rows(o, cnw) * _silu(z)

    for h in range(H_D):
        gamma = math.exp(_ret_log_gamma(h))
        qd = proj_ref[:, O_QD + h * DK_D:O_QD + (h + 1) * DK_D].astype(F32)
        kd = proj_ref[:, O_KD + h * DK_D:O_KD + (h + 1) * DK_D].astype(F32)
        qd = qd * cc + pltpu.roll(qd, DK_D // 2, 1) * ss
        kd = (kd * cc + pltpu.roll(kd, DK_D // 2, 1) * ss) * (DK_D ** -0.5)
        v = proj_ref[:, O_VD + h * DV_D:O_VD + (h + 1) * DV_D].astype(F32)
        qb = _pad_rows16(qd)
        kt = _transpose_bf16(kd)
        qs = jnp.zeros((nb, DV_D), F32)
        for b in range(nb):
            qs = _row_select(rows, b, qs, _dot(qb, ret_ref[b, h].astype(BF16))[0:nb])
        o = gamma * qs + jnp.sum(qd * kd, axis=-1, keepdims=True) * v
        for b in range(nb):
            vsel = _pad_rows16(jnp.where(rows == b, v, 0.0))
            ret_o[b, h] = gamma * ret_ref[b, h] + _dot(kt, vsel)
        z = proj_ref[:, O_Z + H_C * DV_C + h * DV_D:
                     O_Z + H_C * DV_C + (h + 1) * DV_D].astype(F32)
        y_ref[:, H_C * DV_C + h * DV_D:H_C * DV_C + (h + 1) * DV_D] = (
            _rms_rows(o, dnw) * _silu(z))


def _odd_sample(proj, glr, w2, b2, gla, ret, cc, ss, cnw, dnw):
    batch = proj.shape[0]
    nb = DEC_BB
    r2 = lambda i: (i, 0)
    i4 = lambda i: (i, 0, 0, 0)
    c2 = lambda i: (0, 0)
    return pl.pallas_call(
        _odd_sample_kernel,
        grid=(batch // nb,),
        in_specs=[
            pl.BlockSpec((nb, P_O), r2),
            pl.BlockSpec((nb, SMALL_O), r2),
            pl.BlockSpec((LANE, H_C * DK_C), c2),
            pl.BlockSpec((1, H_C * DK_C), c2),
            pl.BlockSpec((nb, H_C, DK_C, DV_C), i4),
            pl.BlockSpec((nb, H_D, DK_D, DV_D), i4),
            pl.BlockSpec((1, DK_D), c2),
            pl.BlockSpec((1, DK_D), c2),
            pl.BlockSpec((1, DV_C), c2),
            pl.BlockSpec((1, DV_D), c2),
        ],
        out_specs=[
            pl.BlockSpec((nb, W_MIX), r2),
            pl.BlockSpec((nb, H_C, DK_C, DV_C), i4),
            pl.BlockSpec((nb, H_D, DK_D, DV_D), i4),
        ],
        out_shape=[
            jax.ShapeDtypeStruct((batch, W_MIX), F32),
            jax.ShapeDtypeStruct((batch, H_C, DK_C, DV_C), F32),
            jax.ShapeDtypeStruct((batch, H_D, DK_D, DV_D), F32),
        ],
        compiler_params=pltpu.CompilerParams(
            dimension_semantics=("arbitrary",), vmem_limit_bytes=VMEM_LIMIT),
        name="odd_sample",
    )(proj, glr, w2, b2, gla, ret, cc, ss, cnw, dnw)


def _pad_cols(w, width):
    return jnp.pad(w, ((0, 0), (0, width - w.shape[1])))


def _prep_even_w_in(w):
    u = w[:, 0:3072]
    beta = w[:, 3072:3080]
    a = w[:, 3080:3088]
    qb = w[:, 3088:3600].reshape(-1, H_B, DK_B)
    kb = w[:, 3600:4112].reshape(-1, H_B, DK_B)
    vb = w[:, 4112:5136]
    ig = w[:, 5136:5144]
    fg = w[:, 5144:5152]
    op = w[:, 5152:6176]
    z = w[:, 6176:8224]
    qk = jnp.concatenate([qb, kb], axis=-1).reshape(-1, H_B * 2 * DK_B)
    small = jnp.concatenate([_pad_cols(beta, 128), _pad_cols(a, 128), _pad_cols(ig, 128),
                             _pad_cols(fg, 128)], axis=1)
    return jnp.concatenate([u, qk, vb, op, z], axis=1).astype(BF16), small.astype(BF16)


def _prep_odd_w_in(w):
    qc = w[:, 0:512]
    kc = w[:, 512:1024]
    vc = w[:, 1024:2048]
    glr = w[:, 2048:2064]
    qd = w[:, 2064:2576]
    kd = w[:, 2576:3088]
    vd = w[:, 3088:4112]
    z = w[:, 4112:6160]
    return (jnp.concatenate([qc, kc, vc, qd, kd, vd, z], axis=1).astype(BF16),
            _pad_cols(glr, SMALL_O).astype(BF16))


def _lane_row(vec, width=LANE):
    return jnp.pad(vec.astype(F32), (0, width - vec.shape[0]))


def _rotary_tables(pos):
    half = DK_D // 2
    inv = ROPE_BASE ** (-jnp.arange(half, dtype=F32) / half)
    ang = pos.astype(F32)[:, None] * inv[None, :]
    cos = jnp.cos(ang)
    sin = jnp.sin(ang)
    return jnp.concatenate([cos, cos], axis=-1), jnp.concatenate([-sin, sin], axis=-1)


def kernel(x_prompt, x_sample, c_prompt, c_sample, state_gdn, state_gdn_conv, state_mlstm_c,
           state_mlstm_n, state_mlstm_m, state_gla, state_ret, ada_w, ada_b, norm_w, ev_w_in,
           ev_w_out, gdn_conv_w, gdn_a_log, gdn_dt_bias, gdn_norm_w, mlstm_gate_b, mlstm_norm_w,
           od_w_in, od_w_out, gla_w2, gla_b2, gla_norm_w, ret_norm_w, final_norm_w):
    bp, tp, d = x_prompt.shape
    bs = x_sample.shape[0]
    n_p = bp * tp

    w_in_e, w_sm_e = _prep_even_w_in(ev_w_in[0])
    w_in_o, w_sm_o = _prep_odd_w_in(od_w_in[0])
    w_out_e = ev_w_out[0].astype(BF16)
    w_out_o = od_w_out[0].astype(BF16)
    sp = jnp.stack([_lane_row(gdn_dt_bias[0]), _lane_row(gdn_a_log[0]),
                    _lane_row(mlstm_gate_b[0, :H_B]), _lane_row(mlstm_gate_b[0, H_B:])]
                   + [jnp.zeros((LANE,), F32)] * 4)
    gnw = gdn_norm_w[0].reshape(1, DV_A)
    mnw = mlstm_norm_w[0].reshape(1, DV_B)
    cnw = gla_norm_w[0].reshape(1, DV_C)
    dnw = ret_norm_w[0].reshape(1, DV_D)
    w2 = jnp.pad(gla_w2[0], ((0, LANE - GLA_RANK), (0, 0))).astype(BF16)
    b2 = gla_b2[0].reshape(1, H_C * DK_C)

    mod = _modulation(jnp.concatenate([c_prompt, c_sample], axis=0), ada_w, ada_b)
    mod_p = [mod[l, :bp].reshape(bp, 1, 3 * d) for l in range(2)]
    mod_s = [mod[l, bp:].reshape(1, bs, 3 * d) for l in range(2)]

    xp = x_prompt.reshape(n_p, d)
    proj, small = _inproj(xp, mod_p[0], norm_w[0], w_in_e, w_sm_e, 1024, 2048)
    y, p_gdn, p_conv, p_mc, p_mn, p_mm = _even_prompt(
        proj.reshape(bp, tp, P_E), small.reshape(bp, tp, SMALL_E), sp, gdn_conv_w[0], gnw, mnw,
        bp, tp)
    xp = _outproj(y.reshape(n_p, W_MIX), xp, mod_p[0], w_out_e, final_norm_w, 512, False)
    proj, small = _inproj(xp, mod_p[1], norm_w[1], w_in_o, w_sm_o, 1024, 2048)
    cc_p, ss_p = _rotary_tables(jnp.arange(tp))
    y, p_gla, p_ret = _odd_prompt(proj, small, w2, b2, cc_p, ss_p, cnw, dnw, bp, tp)
    y_prompt = _outproj(y, xp, mod_p[1], w_out_o, final_norm_w, 512, True).reshape(bp, tp, d)

    xs = x_sample.reshape(bs, d)
    proj, small = _inproj(xs, mod_s[0], norm_w[0], w_in_e, w_sm_e, bs, 2048)
    mm_in = jnp.pad(state_mlstm_m[0], ((0, 0), (0, LANE - H_B)))
    y, s_conv, s_gdn, s_mc, s_mn, s_mm = _even_sample(
        proj, small, jnp.transpose(state_gdn_conv[0], (1, 0, 2)), state_gdn[0], state_mlstm_c[0],
        jnp.transpose(state_mlstm_n[0], (1, 0, 2)), mm_in, sp, gdn_conv_w[0], gnw, mnw)
    xs = _outproj(y, xs, mod_s[0], w_out_e, final_norm_w, bs, False)
    proj, small = _inproj(xs, mod_s[1], norm_w[1], w_in_o, w_sm_o, bs, 2048)
    cc_s, ss_s = _rotary_tables(PAST_LEN + jnp.arange(1))
    y, s_gla, s_ret = _odd_sample(proj, small, w2, b2, state_gla[0], state_ret[0], cc_s, ss_s,
                                  cnw, dnw)
    y_sample = _outproj(y, xs, mod_s[1], w_out_o, final_norm_w, bs, True)
    y_sample = y_sample.reshape(bs, 1, d)

    return (y_prompt, y_sample,
            p_gdn[None], p_conv[None], p_mc[None], p_mn[None], p_mm[None, :, 0, :H_B],
            p_gla[None], p_ret[None],
            s_gdn[None], jnp.transpose(s_conv, (1, 0, 2))[None], s_mc[None],
            jnp.transpose(s_mn, (1, 0, 2))[None], s_mm[None, :, :H_B], s_gla[None], s_ret[None])
```

```python
import functools
import math

import jax
import jax.numpy as jnp
import numpy as np
from jax import lax
from jax.experimental import pallas as pl
from jax.experimental.pallas import tpu as pltpu

F32 = jnp.float32
BF16 = jnp.bfloat16
HI = lax.Precision.HIGHEST

D_MODEL = 1024
W_MIX = 2048
H_A, DK_A, DV_A, CONV_W = 8, 128, 128, 4
CONV_CH = H_A * (2 * DK_A + DV_A)
H_B, DK_B, DV_B = 8, 64, 128
H_C, DK_C, DV_C, GLA_RANK, GLA_TAU = 4, 128, 256, 16, 16.0
H_D, DK_D, DV_D = 4, 128, 256
ROPE_BASE = 10000.0
PAST_LEN = 16384
EPS = 1e-6
LANE = 128

CHUNK = 64
CHUNKS_PER_STEP = 4

E_U, E_QK, E_V, E_OP, E_Z = 0, 3072, 4096, 5120, 6144
P_E = 8192
SMALL_E = 512
O_QC, O_KC, O_VC, O_QD, O_KD, O_VD, O_Z = 0, 512, 1024, 2048, 2560, 3072, 4096
P_O = 6144
SMALL_O = 128

VMEM_LIMIT = 56 * 1024 * 1024


def _sigmoid(x):
    return 0.5 * jnp.tanh(0.5 * x) + 0.5


def _softplus(x):
    return jnp.maximum(x, 0.0) + jnp.log(1.0 + jnp.exp(-jnp.abs(x)))


def _log_sigmoid(x):
    return -_softplus(-x)


def _silu(x):
    half = 0.5 * x
    return half + half * jnp.tanh(half)


def _rms_rows(x, w):
    return x * lax.rsqrt(jnp.mean(x * x, axis=-1, keepdims=True) + EPS) * w


def _l2n(x):
    return x * lax.rsqrt(jnp.sum(x * x, axis=-1, keepdims=True) + EPS)


def _dot(a, b, precision=None):
    return jnp.dot(a, b, preferred_element_type=F32, precision=precision)


def _dot_nt(a, b, precision=None):
    return lax.dot_general(a, b, (((1,), (1,)), ((), ())), preferred_element_type=F32,
                           precision=precision)


def _dot_tn(a, b, precision=None):
    return lax.dot_general(a, b, (((0,), (0,)), ((), ())), preferred_element_type=F32,
                           precision=precision)


def _eye(n):
    r = lax.broadcasted_iota(jnp.int32, (n, n), 0)
    c = lax.broadcasted_iota(jnp.int32, (n, n), 1)
    return (r == c).astype(F32)


def _transpose(x):
    return _dot_nt(_eye(x.shape[1]), x, precision=HI)


def _tri_masks(n):
    r = lax.broadcasted_iota(jnp.int32, (n, n), 0)
    c = lax.broadcasted_iota(jnp.int32, (n, n), 1)
    return r >= c, r > c


def _mod_kernel(c_ref, w_ref, b_ref, o_ref):
    cs = _silu(c_ref[...]).astype(BF16)
    o_ref[...] = _dot(cs, w_ref[...].astype(BF16)) + b_ref[...]


def _modulation(c_all, ada_w, ada_b):
    depth, d, d3 = ada_w.shape
    rows = c_all.shape[0]
    tn = 1024
    return pl.pallas_call(
        _mod_kernel,
        grid=(depth, d3 // tn),
        in_specs=[
            pl.BlockSpec((rows, d), lambda l, j: (0, 0)),
            pl.BlockSpec((None, d, tn), lambda l, j: (l, 0, j)),
            pl.BlockSpec((None, 1, tn), lambda l, j: (l, 0, j)),
        ],
        out_specs=pl.BlockSpec((None, rows, tn), lambda l, j: (l, 0, j)),
        out_shape=jax.ShapeDtypeStruct((depth, rows, d3), F32),
        compiler_params=pltpu.CompilerParams(
            dimension_semantics=("arbitrary", "arbitrary"), vmem_limit_bytes=VMEM_LIMIT),
        name="modulation",
    )(c_all, ada_w, ada_b.reshape(depth, 1, d3))


def _inproj_kernel(x_ref, mod_ref, nw_ref, w_ref, ws_ref, o_ref, os_ref, h_ref):
    @pl.when(pl.program_id(1) == 0)
    def _():
        x = x_ref[...]
        y = _rms_rows(x, nw_ref[...])
        shift = mod_ref[:, 0:D_MODEL]
        scale = mod_ref[:, D_MODEL:2 * D_MODEL]
        h_ref[...] = (y * (1.0 + scale) + shift).astype(BF16)
        os_ref[...] = _dot(h_ref[...], ws_ref[...])

    o_ref[...] = _dot(h_ref[...], w_ref[...]).astype(o_ref.dtype)


def _inproj(x, mod, norm_w, w, w_small, tm, tn):
    n, d = x.shape
    p = w.shape[1]
    ps = w_small.shape[1]
    g, r, _ = mod.shape
    tiles_per_group = (n // g) // tm
    return pl.pallas_call(
        _inproj_kernel,
        grid=(n // tm, p // tn),
        in_specs=[
            pl.BlockSpec((tm, d), lambda i, j: (i, 0)),
            pl.BlockSpec((None, r, 3 * d), lambda i, j: (i // tiles_per_group, 0, 0)),
            pl.BlockSpec((1, d), lambda i, j: (0, 0)),
            pl.BlockSpec((d, tn), lambda i, j: (0, j)),
            pl.BlockSpec((d, ps), lambda i, j: (0, 0)),
        ],
        out_specs=[pl.BlockSpec((tm, tn), lambda i, j: (i, j)),
                   pl.BlockSpec((tm, ps), lambda i, j: (i, 0))],
        out_shape=[jax.ShapeDtypeStruct((n, p), BF16), jax.ShapeDtypeStruct((n, ps), F32)],
        scratch_shapes=[pltpu.VMEM((tm, d), BF16)],
        compiler_params=pltpu.CompilerParams(
            dimension_semantics=("arbitrary", "arbitrary"), vmem_limit_bytes=VMEM_LIMIT),
        name="inproj",
    )(x, mod, norm_w.reshape(1, d), w, w_small)


def _outproj_kernel(y_ref, x_ref, mod_ref, w_ref, fw_ref, o_ref, *, final):
    acc = _dot(y_ref[...].astype(BF16), w_ref[...])
    gate = mod_ref[:, 2 * D_MODEL:3 * D_MODEL]
    xn = x_ref[...] + gate * acc
    if final:
        xn = _rms_rows(xn, fw_ref[...])
    o_ref[...] = xn


def _outproj(y, x, mod, w, final_w, tm, final):
    n, d = x.shape
    k = y.shape[1]
    g, r, _ = mod.shape
    tiles_per_group = (n // g) // tm
    return pl.pallas_call(
        functools.partial(_outproj_kernel, final=final),
        grid=(n // tm,),
        in_specs=[
            pl.BlockSpec((tm, k), lambda i: (i, 0)),
            pl.BlockSpec((tm, d), lambda i: (i, 0)),
            pl.BlockSpec((None, r, 3 * d), lambda i: (i // tiles_per_group, 0, 0)),
            pl.BlockSpec((k, d), lambda i: (0, 0)),
            pl.BlockSpec((1, d), lambda i: (0, 0)),
        ],
        out_specs=pl.BlockSpec((tm, d), lambda i: (i, 0)),
        out_shape=jax.ShapeDtypeStruct((n, d), F32),
        compiler_params=pltpu.CompilerParams(
            dimension_semantics=("arbitrary",), vmem_limit_bytes=VMEM_LIMIT),
        name="outproj",
    )(y, x, mod, w, final_w.reshape(1, d))


def _neumann_correction(a_list, n):
    ys = [-a for a in a_list]
    pbs = [a.astype(BF16) for a in a_list]
    for _ in range(int(math.log2(n)) - 1):
        for i in range(len(ys)):
            p = _dot(pbs[i], pbs[i])
            pbs[i] = p.astype(BF16)
            ys[i] = ys[i] + p + _dot(ys[i].astype(BF16), pbs[i])
        yield
    idx = range(len(ys))
    a_h = [a.astype(BF16) for a in a_list]
    a_l = [(a_list[i] - a_h[i].astype(F32)).astype(BF16) for i in idx]
    y_h = [y.astype(BF16) for y in ys]
    y_l = [(ys[i] - y_h[i].astype(F32)).astype(BF16) for i in idx]
    ay = [_dot(a_h[i], y_h[i]) + _dot(a_h[i], y_l[i]) + _dot(a_l[i], y_h[i]) for i in idx]
    e = [-(a_list[i] + ys[i] + ay[i]) for i in idx]
    return [ys[i] + e[i] + _dot(y_h[i], e[i].astype(BF16)) for i in idx]


def _run_staggered(stage_generators, offset):
    pending = list(stage_generators)
    active = []
    tick = 0
    while pending or active:
        if pending and tick % offset == 0:
            active.append(pending.pop(0))
        for g in list(active):
            try:
                next(g)
            except StopIteration:
                active.remove(g)
        tick += 1


EVEN_STAGGER = 16
ODD_STAGGER = 10


def _conv_shift_matrix():
    n = CHUNK
    r = np.arange((CONV_W - 1) * n)[:, None]
    col = np.arange(2 * n)[None, :]
    return jnp.asarray(col == n + (r % n) - (r // n + 1), BF16)


def _even_prompt_kernel(proj_ref, sm_ref, sp_ref, cw_ref, shift_ref, gnw_ref, mnw_ref,
                        y_ref, gdn_ref, conv_ref, mc_ref, mn_ref, mm_ref,
                        uprev_ref, qkv_ref):
    chunks = []
    for j in range(CHUNKS_PER_STEP):
        rows = pl.ds(j * CHUNK, CHUNK)
        chunks.append(_even_prompt_sequence(
            j == 0, proj_ref.at[0, rows], sm_ref.at[0, rows], sp_ref, cw_ref, shift_ref, gnw_ref,
            mnw_ref, y_ref.at[0, rows], gdn_ref.at[0], conv_ref.at[0], mc_ref.at[0], mn_ref.at[0],
            mm_ref.at[0], uprev_ref.at[0], qkv_ref.at[0]))
    _run_staggered(chunks, EVEN_STAGGER)


def _even_prompt_sequence(first_in_step, proj_ref, sm_ref, sp_ref, cw_ref, shift_ref, gnw_ref,
                          mnw_ref, y_ref, gdn_ref, conv_ref, mc_ref, mn_ref, mm_ref,
                          uprev_ref, qkv_ref):
    L = CHUNK

    if first_in_step:
        @pl.when(pl.program_id(1) == 0)
        def _():
            gdn_ref[...] = jnp.zeros_like(gdn_ref)
            mc_ref[...] = jnp.zeros_like(mc_ref)
            mn_ref[...] = jnp.zeros_like(mn_ref)
            mm_ref[...] = jnp.zeros_like(mm_ref)
            uprev_ref[...] = jnp.zeros_like(uprev_ref)

    tri_incl, tri_strict = _tri_masks(L)
    tri_f = tri_incl.astype(F32)

    ub = proj_ref[:, E_U:E_U + CONV_CH]
    shifted = _dot(shift_ref[...], jnp.concatenate([uprev_ref[...], ub], axis=0))
    uprev_ref[...] = ub
    u = ub.astype(F32)
    cw = cw_ref[...]
    uc = (cw[0:1] * shifted[2 * L:3 * L] + cw[1:2] * shifted[L:2 * L]
          + cw[2:3] * shifted[0:L] + cw[3:4] * u)
    conv_ref[...] = u[L - 3:L, :]
    qkv_ref[...] = _silu(uc)
    yield

    sm = sm_ref[...]
    head_lane = lax.broadcasted_iota(jnp.int32, (L, LANE), 1) < H_A
    beta = _sigmoid(sm[:, 0:128])
    g = -jnp.exp(sp_ref[1:2, :]) * _softplus(sm[:, 128:256] + sp_ref[0:1, :])
    ig = sm[:, 256:384] + sp_ref[2:3, :]
    lf = _log_sigmoid(sm[:, 384:512] + sp_ref[3:4, :])
    packed = jnp.where(head_lane, g, 0.0) + pltpu.roll(jnp.where(head_lane, lf, 0.0), H_A, 1)
    cum = _dot(tri_f, packed, precision=HI)
    cum_t = _transpose(cum + pltpu.roll(jnp.where(head_lane, ig, 0.0), H_A + H_B, 1))

    gnw = gnw_ref[...]
    mnw = mnw_ref[...]
    yield

    qkb, kf, decay, beta_c, gam, bh_c = [], [], [], [], [], []
    for h in range(H_A):
        q = _l2n(qkv_ref[:, h * DK_A:(h + 1) * DK_A]) * (DK_A ** -0.5)
        k = _l2n(qkv_ref[:, H_A * DK_A + h * DK_A:H_A * DK_A + (h + 1) * DK_A])
        kf.append(k)
        qkb.append(jnp.concatenate([q, k], axis=0).astype(BF16))
    yield
    qkk = [_dot_nt(qkb[h], qkb[h][L:2 * L]) for h in range(H_A)]
    yield
    a_list = []
    for h in range(H_A):
        bh_c.append(cum[:, h:h + 1])
        diff = bh_c[h] - cum_t[h:h + 1, :]
        decay.append(jnp.where(tri_incl, jnp.exp(jnp.where(tri_incl, diff, 0.0)), 0.0))
        beta_c.append(beta[:, h:h + 1])
        gam.append(jnp.exp(bh_c[h]))
        a_list.append(jnp.where(tri_strict, decay[h] * qkk[h][L:2 * L], 0.0) * beta_c[h])
    yield
    corr = yield from _neumann_correction(a_list, L)
    yield

    heads = range(H_A)
    s_old = [gdn_ref[h] for h in heads]
    qks = [_dot(qkb[h], s_old[h].astype(BF16)) for h in heads]
    yield
    rhs = [beta_c[h] * (qkv_ref[:, 2 * H_A * DK_A + h * DV_A:2 * H_A * DK_A + (h + 1) * DV_A]
                        - gam[h] * qks[h][L:2 * L]) for h in heads]
    ub = [(rhs[h] + _dot(corr[h].astype(BF16), rhs[h].astype(BF16))).astype(BF16) for h in heads]
    yield
    o = [gam[h] * qks[h][0:L] + _dot((qkk[h][0:L] * decay[h]).astype(BF16), ub[h])
         for h in heads]
    yield
    for h in heads:
        last = bh_c[h][L - 1:L, :]
        kw = (kf[h] * jnp.exp(last - bh_c[h])).astype(BF16)
        gdn_ref[h] = jnp.exp(last) * s_old[h] + _dot_tn(kw, ub[h])
    yield
    for h in heads:
        z = proj_ref[:, E_Z + h * DV_A:E_Z + (h + 1) * DV_A].astype(F32)
        y_ref[:, h * DV_A:(h + 1) * DV_A] = (_rms_rows(o[h], gnw) * _silu(z)).astype(BF16)
    yield

    heads = range(H_B)
    qs, qbs, kbs, vbs = [], [], [], []
    for h in heads:
        qk_in = proj_ref[:, E_QK + h * 128:E_QK + (h + 1) * 128]
        qbs.append(qk_in[:, 0:DK_B])
        qs.append(qbs[h].astype(F32))
        kbs.append((qk_in[:, DK_B:2 * DK_B].astype(F32) * (DK_B ** -0.5)).astype(BF16))
        vbs.append(proj_ref[:, E_V + h * DV_B:E_V + (h + 1) * DV_B])
    qkm = [_dot_nt(qbs[h], kbs[h]) for h in heads]
    yield
    b_c = [cum[:, H_A + h:H_A + h + 1] for h in heads]
    m0 = [mm_ref[0:1, h:h + 1] for h in heads]
    dmat = [jnp.where(tri_incl, b_c[h] - cum_t[H_A + h:H_A + h + 1, :]
                      + cum_t[H_A + H_B + h:H_A + H_B + h + 1, :], -jnp.inf) for h in heads]
    w0 = [b_c[h] + m0[h] for h in heads]
    m_t = [jnp.maximum(w0[h], jnp.max(dmat[h], axis=-1, keepdims=True)) for h in heads]
    p = [jnp.exp(dmat[h] - m_t[h]) * qkm[h] for h in heads]
    s0 = [jnp.exp(w0[h] - m_t[h]) for h in heads]
    yield
    c_old = [mc_ref[h] for h in heads]
    n_old = [mn_ref[h:h + 1, :] for h in heads]
    num = [s0[h] * _dot(qbs[h], c_old[h].astype(BF16)) + _dot(p[h].astype(BF16), vbs[h])
           for h in heads]
    den = [s0[h] * jnp.sum(qs[h] * n_old[h], axis=-1, keepdims=True)
           + jnp.sum(p[h], axis=-1, keepdims=True) for h in heads]
    hh = [num[h] / jnp.maximum(jnp.abs(den[h]), jnp.exp(-m_t[h])) for h in heads]
    yield
    for h in heads:
        m_end = m_t[h][L - 1:L, :]
        b_last = b_c[h][L - 1:L, :]
        we = jnp.exp(b_last - b_c[h] + ig[:, h:h + 1] - m_end)
        se = jnp.exp(b_last + m0[h] - m_end)
        kw = kbs[h].astype(F32) * we
        mc_ref[h] = se * c_old[h] + _dot_tn(kw.astype(BF16), vbs[h])
        mn_ref[h:h + 1, :] = se * n_old[h] + jnp.sum(kw, axis=0, keepdims=True)
        mm_ref[0:1, h:h + 1] = m_end
    yield
    for h in heads:
        o_pre = proj_ref[:, E_OP + h * DV_B:E_OP + (h + 1) * DV_B].astype(F32)
        z = proj_ref[:, E_Z + H_A * DV_A + h * DV_B:
                     E_Z + H_A * DV_A + (h + 1) * DV_B].astype(F32)
        y_ref[:, H_A * DV_A + h * DV_B:H_A * DV_A + (h + 1) * DV_B] = (
            _sigmoid(o_pre) * _rms_rows(hh[h], mnw) * _silu(z)).astype(BF16)


def _even_prompt(proj, small, sp, conv_w, gnw, mnw, batch, seq):
    step = CHUNK * CHUNKS_PER_STEP
    nc = seq // step
    nb = 1
    row = lambda b, c: (b, c, 0)
    const2 = lambda b, c: (0, 0)
    st3 = lambda b, c: (b, 0, 0)
    st4 = lambda b, c: (b, 0, 0, 0)
    return pl.pallas_call(
        _even_prompt_kernel,
        grid=(batch // nb, nc),
        in_specs=[
            pl.BlockSpec((nb, step, P_E), row),
            pl.BlockSpec((nb, step, SMALL_E), row),
            pl.BlockSpec((8, LANE), const2),
            pl.BlockSpec((CONV_W, CONV_CH), const2),
            pl.BlockSpec(((CONV_W - 1) * CHUNK, 2 * CHUNK), const2),
            pl.BlockSpec((1, DV_A), const2),
            pl.BlockSpec((1, DV_B), const2),
        ],
        out_specs=[
            pl.BlockSpec((nb, step, W_MIX), row),
            pl.BlockSpec((nb, H_A, DK_A, DV_A), st4),
            pl.BlockSpec((nb, CONV_W - 1, CONV_CH), st3),
            pl.BlockSpec((nb, H_B, DK_B, DV_B), st4),
            pl.BlockSpec((nb, H_B, DK_B), st3),
            pl.BlockSpec((nb, 1, LANE), st3),
        ],
        out_shape=[
            jax.ShapeDtypeStruct((batch, seq, W_MIX), BF16),
            jax.ShapeDtypeStruct((batch, H_A, DK_A, DV_A), F32),
            jax.ShapeDtypeStruct((batch, CONV_W - 1, CONV_CH), F32),
            jax.ShapeDtypeStruct((batch, H_B, DK_B, DV_B), F32),
            jax.ShapeDtypeStruct((batch, H_B, DK_B), F32),
            jax.ShapeDtypeStruct((batch, 1, LANE), F32),
        ],
        scratch_shapes=[pltpu.VMEM((nb, CHUNK, CONV_CH), BF16),
                        pltpu.VMEM((nb, CHUNK, CONV_CH), F32)],
        compiler_params=pltpu.CompilerParams(
            dimension_semantics=("arbitrary", "arbitrary"), vmem_limit_bytes=VMEM_LIMIT),
        name="even_prompt",
    )(proj, small, sp, conv_w, _conv_shift_matrix(), gnw, mnw)


def _ret_log_gamma(h):
    return math.log(1.0 - 2.0 ** (-5.0 - h))


GLA_LEVELS = int(math.log2(CHUNK))
GLA_MM_LEVELS = (1, 2)
LOG2E = math.log2(math.e)


def _gla_chunk_tables():
    n = CHUNK
    t = np.arange(n)[:, None]
    i = np.arange(n)[None, :]
    blocks = [(i <= t)]
    for lvl in GLA_MM_LEVELS:
        hs = 1 << lvl
        p = (t & ~(2 * hs - 1)) + hs - 1
        upper = (t & hs) != 0
        blocks.append(np.where(upper, (i > p) & (i <= t), (i > t) & (i <= p)))
    sums = np.concatenate(blocks, axis=0).astype(np.float32)
    s = np.arange(n)[None, :]
    level = np.full((n, n), GLA_LEVELS + 1, np.int32)
    level[np.arange(n), np.arange(n)] = GLA_LEVELS
    for lvl in range(GLA_LEVELS):
        hs = 1 << lvl
        same = (t >> (lvl + 1)) == (s >> (lvl + 1))
        level[same & ((t & hs) != 0) & ((s & hs) == 0)] = lvl
    return jnp.asarray(sums), jnp.asarray(level)


def _odd_prompt_kernel(proj_ref, glr_ref, w2_ref, b2_ref, sums_ref, level_ref, cc_ref, ss_ref,
                       cnw_ref, dnw_ref, y_ref, gla_ref, ret_ref, st_ref, x_ref):
    chunks = []
    for j in range(CHUNKS_PER_STEP):
        rows = pl.ds(j * CHUNK, CHUNK)
        chunks.append(_odd_prompt_chunk(
            j == 0, j == CHUNKS_PER_STEP - 1, proj_ref.at[rows], glr_ref.at[rows], w2_ref, b2_ref,
            sums_ref, level_ref, cc_ref.at[rows], ss_ref.at[rows], cnw_ref, dnw_ref,
            y_ref.at[rows], gla_ref, ret_ref, st_ref, x_ref))
    _run_staggered(chunks, ODD_STAGGER)


def _odd_prompt_chunk(first_in_step, last_in_step, proj_ref, glr_ref, w2_ref, b2_ref, sums_ref,
                      level_ref, cc_ref, ss_ref, cnw_ref, dnw_ref, y_ref, gla_ref, ret_ref,
                      st_ref, x_ref):
    L = CHUNK
    c = pl.program_id(1)
    nc = pl.num_programs(1)

    if first_in_step:
        @pl.when(c == 0)
        def _():
            st_ref[...] = jnp.zeros_like(st_ref)
            ret_ref[...] = jnp.zeros_like(ret_ref)

    tri_incl, _ = _tri_masks(L)
    r64 = lax.broadcasted_iota(jnp.int32, (L, L), 0)
    c64 = lax.broadcasted_iota(jnp.int32, (L, L), 1)
    level = level_ref[...]

    glr = glr_ref[...].astype(BF16)
    gc2 = _log_sigmoid(_dot(glr, w2_ref[...]) + b2_ref[...]) * (LOG2E / GLA_TAU)
    x_ref[...] = _dot(sums_ref[...], gc2, precision=HI)
    odd_row = (lax.broadcasted_iota(jnp.int32, (L, DK_C), 0) & 1) == 1
    yield

    def level_exponent(lvl, h):
        cols = slice(h * DK_C, (h + 1) * DK_C)
        if lvl == 0:
            return jnp.where(odd_row, gc2[:, cols], 0.0)
        if lvl in GLA_MM_LEVELS:
            r0 = (1 + GLA_MM_LEVELS.index(lvl)) * L
            return x_ref[r0:r0 + L, cols]
        hs = 1 << lvl
        pieces = []
        for lo in range(0, L, 2 * hs):
            ref = x_ref[lo + hs - 1:lo + hs, cols]
            pieces.append(ref - x_ref[lo:lo + hs, cols])
            pieces.append(x_ref[lo + hs:lo + 2 * hs, cols] - ref)
        return jnp.concatenate(pieces, axis=0)

    cnw = cnw_ref[...]
    dnw = dnw_ref[...]

    heads = range(H_C)
    q = [proj_ref[:, O_QC + h * DK_C:O_QC + (h + 1) * DK_C].astype(F32) * (DK_C ** -0.5)
         for h in heads]
    k = [proj_ref[:, O_KC + h * DK_C:O_KC + (h + 1) * DK_C].astype(F32) for h in heads]
    v = [proj_ref[:, O_VC + h * DV_C:O_VC + (h + 1) * DV_C] for h in heads]
    att = [jnp.where(level == GLA_LEVELS, _dot_nt(q[h].astype(BF16), k[h].astype(BF16)), 0.0)
           for h in heads]
    yield
    for lvl in range(GLA_LEVELS):
        for h in heads:
            e = jnp.exp2(level_exponent(lvl, h))
            m = _dot_nt((q[h] * e).astype(BF16), (k[h] * e).astype(BF16))
            att[h] = jnp.where(level == lvl, m, att[h])
        yield
    b2 = [x_ref[0:L, h * DK_C:(h + 1) * DK_C] for h in heads]
    st_old = [st_ref[h] for h in heads]
    o = [_dot_nt((q[h] * jnp.exp2(b2[h])).astype(BF16), st_old[h].astype(BF16))
         + _dot(att[h].astype(BF16), v[h]) for h in heads]
    yield
    for h in heads:
        bl = b2[h][L - 1:L, :]
        kw = (k[h] * jnp.exp2(bl - b2[h])).astype(BF16)
        st_ref[h] = st_old[h] * jnp.exp2(bl) + _dot_tn(v[h], kw)
    yield
    for h in heads:
        z = proj_ref[:, O_Z + h * DV_C:O_Z + (h + 1) * DV_C].astype(F32)
        y_ref[:, h * DV_C:(h + 1) * DV_C] = (_rms_rows(o[h], cnw) * _silu(z)).astype(BF16)
    yield

    if last_in_step:
        @pl.when(c == nc - 1)
        def _():
            for h in range(H_C):
                st = st_ref[h]
                gla_ref[h, :, 0:128] = _transpose(st[0:128, :])
                gla_ref[h, :, 128:256] = _transpose(st[128:256, :])

    cc = cc_ref[...]
    ss = ss_ref[...]
    rel = (r64 - c64).astype(F32)
    tcol = lax.broadcasted_iota(jnp.int32, (L, 1), 0).astype(F32)
    heads = range(H_D)
    lg = [_ret_log_gamma(h) for h in heads]
    qb, kf, kb, vd = [], [], [], []
    for h in heads:
        qd = proj_ref[:, O_QD + h * DK_D:O_QD + (h + 1) * DK_D].astype(F32)
        kd = proj_ref[:, O_KD + h * DK_D:O_KD + (h + 1) * DK_D].astype(F32)
        qb.append((qd * cc + pltpu.roll(qd, DK_D // 2, 1) * ss).astype(BF16))
        kf.append((kd * cc + pltpu.roll(kd, DK_D // 2, 1) * ss) * (DK_D ** -0.5))
        kb.append(kf[h].astype(BF16))
        vd.append(proj_ref[:, O_VD + h * DV_D:O_VD + (h + 1) * DV_D])
    yield
    s_old = [ret_ref[h] for h in heads]
    attd = [_dot_nt(qb[h], kb[h])
            * jnp.where(tri_incl, jnp.exp(lg[h] * jnp.maximum(rel, 0.0)), 0.0) for h in heads]
    od = [_dot(qb[h], s_old[h].astype(BF16)) * jnp.exp(lg[h] * (tcol + 1.0))
          + _dot(attd[h].astype(BF16), vd[h]) for h in heads]
    yield
    for h in heads:
        end = jnp.exp(lg[h] * (L - 1.0 - tcol))
        ret_ref[h] = math.exp(lg[h] * L) * s_old[h] + _dot_tn((kf[h] * end).astype(BF16), vd[h])
    yield
    for h in heads:
        z = proj_ref[:, O_Z + H_C * DV_C + h * DV_D:
                     O_Z + H_C * DV_C + (h + 1) * DV_D].astype(F32)
        y_ref[:, H_C * DV_C + h * DV_D:H_C * DV_C + (h + 1) * DV_D] = (
            _rms_rows(od[h], dnw) * _silu(z)).astype(BF16)


def _odd_prompt(proj, glr, w2, b2, cc, ss, cnw, dnw, batch, seq):
    step = CHUNK * CHUNKS_PER_STEP
    nc = seq // step
    row = lambda b, c: (b * nc + c, 0)
    const2 = lambda b, c: (0, 0)
    sums, level = _gla_chunk_tables()
    n_sums = sums.shape[0]
    return pl.pallas_call(
        _odd_prompt_kernel,
        grid=(batch, nc),
        in_specs=[
            pl.BlockSpec((step, P_O), row),
            pl.BlockSpec((step, SMALL_O), row),
            pl.BlockSpec((LANE, H_C * DK_C), const2),
            pl.BlockSpec((1, H_C * DK_C), const2),
            pl.BlockSpec((n_sums, CHUNK), const2),
            pl.BlockSpec((CHUNK, CHUNK), const2),
            pl.BlockSpec((step, DK_D), lambda b, c: (c, 0)),
            pl.BlockSpec((step, DK_D), lambda b, c: (c, 0)),
            pl.BlockSpec((1, DV_C), const2),
            pl.BlockSpec((1, DV_D), const2),
        ],
        out_specs=[
            pl.BlockSpec((step, W_MIX), row),
            pl.BlockSpec((None, H_C, DK_C, DV_C), lambda b, c: (b, 0, 0, 0)),
            pl.BlockSpec((None, H_D, DK_D, DV_D), lambda b, c: (b, 0, 0, 0)),
        ],
        out_shape=[
            jax.ShapeDtypeStruct((batch * seq, W_MIX), BF16),
            jax.ShapeDtypeStruct((batch, H_C, DK_C, DV_C), F32),
            jax.ShapeDtypeStruct((batch, H_D, DK_D, DV_D), F32),
        ],
        scratch_shapes=[pltpu.VMEM((H_C, DV_C, DK_C), F32),
                        pltpu.VMEM((n_sums, H_C * DK_C), F32)],
        compiler_params=pltpu.CompilerParams(
            dimension_semantics=("arbitrary", "arbitrary"), vmem_limit_bytes=VMEM_LIMIT),
        name="odd_prompt",
    )(proj, glr, w2, b2, sums, level, cc, ss, cnw, dnw)


DEC_BB = 8


def _row_select(rows, row_index, acc, new):
    return jnp.where(rows == row_index, new, acc)


def _pad_rows16(x):
    return jnp.concatenate([x, jnp.zeros_like(x)], axis=0).astype(BF16)


def _transpose_bf16(x):
    eye = _eye(x.shape[1]).astype(BF16)
    return _dot_nt(eye, _pad_rows16(x)).astype(BF16)


def _even_sample_kernel(proj_ref, sm_ref, conv_ref, gdn_ref, mc_ref, mn_ref, mm_ref,
                        sp_ref, cw_ref, gnw_ref, mnw_ref,
                        y_ref, conv_o, gdn_o, mc_o, mn_o, mm_o, act_ref):
    nb = DEC_BB
    gnw = gnw_ref[...]
    mnw = mnw_ref[...]
    rows = lax.broadcasted_iota(jnp.int32, (nb, LANE), 0)

    sm = sm_ref[...]
    beta = _sigmoid(sm[:, 0:128])
    gam = jnp.exp(-jnp.exp(sp_ref[1:2, :]) * _softplus(sm[:, 128:256] + sp_ref[0:1, :]))
    ig = sm[:, 256:384] + sp_ref[2:3, :]
    lf = _log_sigmoid(sm[:, 384:512] + sp_ref[3:4, :])
    w0 = lf + mm_ref[...]
    m_t = jnp.maximum(w0, ig)
    s0 = jnp.exp(w0 - m_t)
    we = jnp.exp(ig - m_t)
    floor = jnp.exp(-m_t)
    mm_o[...] = m_t

    u = proj_ref[:, E_U:E_U + CONV_CH].astype(F32)
    cw = cw_ref[...]
    uc = (cw[0:1] * conv_ref[0] + cw[1:2] * conv_ref[1] + cw[2:3] * conv_ref[2] + cw[3:4] * u)
    conv_o[0] = conv_ref[1]
    conv_o[1] = conv_ref[2]
    conv_o[2] = u
    act_ref[...] = _silu(uc)

    for h in range(H_A):
        q = _l2n(act_ref[:, h * DK_A:(h + 1) * DK_A]) * (DK_A ** -0.5)
        k = _l2n(act_ref[:, H_A * DK_A + h * DK_A:H_A * DK_A + (h + 1) * DK_A])
        v = act_ref[:, 2 * H_A * DK_A + h * DV_A:2 * H_A * DK_A + (h + 1) * DV_A]
        kq = jnp.concatenate([k, q], axis=0).astype(BF16)
        kt = _transpose_bf16(k)
        ks = jnp.zeros((nb, DV_A), F32)
        qs = jnp.zeros((nb, DV_A), F32)
        for b in range(nb):
            r = _dot(kq, gdn_ref[b, h].astype(BF16))
            ks = _row_select(rows, b, ks, r[0:nb])
            qs = _row_select(rows, b, qs, r[nb:2 * nb])
        gam_h = gam[:, h:h + 1]
        uu = beta[:, h:h + 1] * (v - gam_h * ks)
        o = gam_h * qs + jnp.sum(q * k, axis=-1, keepdims=True) * uu
        for b in range(nb):
            usel = _pad_rows16(jnp.where(rows == b, uu, 0.0))
            gdn_o[b, h] = gam[b:b + 1, h:h + 1] * gdn_ref[b, h] + _dot(kt, usel)
        z = proj_ref[:, E_Z + h * DV_A:E_Z + (h + 1) * DV_A].astype(F32)
        y_ref[:, h * DV_A:(h + 1) * DV_A] = _rms_rows(o, gnw) * _silu(z)

    for h in range(H_B):
        qk_in = proj_ref[:, E_QK + h * 128:E_QK + (h + 1) * 128]
        qb = qk_in[:, 0:DK_B]
        q = qb.astype(F32)
        k = qk_in[:, DK_B:2 * DK_B].astype(F32) * (DK_B ** -0.5)
        v = proj_ref[:, E_V + h * DV_B:E_V + (h + 1) * DV_B].astype(F32)
        kt = _transpose_bf16(k)
        q16 = _pad_rows16(qb)
        qc = jnp.zeros((nb, DV_B), F32)
        for b in range(nb):
            r = _dot(q16, mc_ref[b, h].astype(BF16))
            qc = _row_select(rows, b, qc, r[0:nb])
        s0_h = s0[:, h:h + 1]
        we_h = we[:, h:h + 1]
        wv = we_h * v
        nn = s0_h * mn_ref[h] + we_h * k
        mn_o[h] = nn
        num = s0_h * qc + jnp.sum(q * k, axis=-1, keepdims=True) * wv
        den = jnp.sum(q * nn, axis=-1, keepdims=True)
        hh = num / jnp.maximum(jnp.abs(den), floor[:, h:h + 1])
        for b in range(nb):
            wsel = _pad_rows16(jnp.where(rows == b, wv, 0.0))
            mc_o[b, h] = s0[b:b + 1, h:h + 1] * mc_ref[b, h] + _dot(kt, wsel)
        o_pre = proj_ref[:, E_OP + h * DV_B:E_OP + (h + 1) * DV_B].astype(F32)
        z = proj_ref[:, E_Z + H_A * DV_A + h * DV_B:
                     E_Z + H_A * DV_A + (h + 1) * DV_B].astype(F32)
        y_ref[:, H_A * DV_A + h * DV_B:H_A * DV_A + (h + 1) * DV_B] = (
            _sigmoid(o_pre) * _rms_rows(hh, mnw) * _silu(z))


def _even_sample(proj, small, conv, gdn, mc, mn, mm, sp, cw, gnw, mnw):
    batch = proj.shape[0]
    nb = DEC_BB
    r2 = lambda i: (i, 0)
    m3 = lambda i: (0, i, 0)
    i4 = lambda i: (i, 0, 0, 0)
    c2 = lambda i: (0, 0)
    return pl.pallas_call(
        _even_sample_kernel,
        grid=(batch // nb,),
        in_specs=[
            pl.BlockSpec((nb, P_E), r2),
            pl.BlockSpec((nb, SMALL_E), r2),
            pl.BlockSpec((CONV_W - 1, nb, CONV_CH), m3),
            pl.BlockSpec((nb, H_A, DK_A, DV_A), i4),
            pl.BlockSpec((nb, H_B, DK_B, DV_B), i4),
            pl.BlockSpec((H_B, nb, DK_B), m3),
            pl.BlockSpec((nb, LANE), r2),
            pl.BlockSpec((8, LANE), c2),
            pl.BlockSpec((CONV_W, CONV_CH), c2),
            pl.BlockSpec((1, DV_A), c2),
            pl.BlockSpec((1, DV_B), c2),
        ],
        out_specs=[
            pl.BlockSpec((nb, W_MIX), r2),
            pl.BlockSpec((CONV_W - 1, nb, CONV_CH), m3),
            pl.BlockSpec((nb, H_A, DK_A, DV_A), i4),
            pl.BlockSpec((nb, H_B, DK_B, DV_B), i4),
            pl.BlockSpec((H_B, nb, DK_B), m3),
            pl.BlockSpec((nb, LANE), r2),
        ],
        out_shape=[
            jax.ShapeDtypeStruct((batch, W_MIX), F32),
            jax.ShapeDtypeStruct((CONV_W - 1, batch, CONV_CH), F32),
            jax.ShapeDtypeStruct((batch, H_A, DK_A, DV_A), F32),
            jax.ShapeDtypeStruct((batch, H_B, DK_B, DV_B), F32),
            jax.ShapeDtypeStruct((H_B, batch, DK_B), F32),
            jax.ShapeDtypeStruct((batch, LANE), F32),
        ],
        scratch_shapes=[pltpu.VMEM((nb, CONV_CH), F32)],
        compiler_params=pltpu.CompilerParams(
            dimension_semantics=("arbitrary",), vmem_limit_bytes=VMEM_LIMIT),
        name="even_sample",
    )(proj, small, conv, gdn, mc, mn, mm, sp, cw, gnw, mnw)


def _odd_sample_kernel(proj_ref, glr_ref, w2_ref, b2_ref, gla_ref, ret_ref, cc_ref, ss_ref,
                       cnw_ref, dnw_ref, y_ref, gla_o, ret_o):
    nb = DEC_BB
    cnw = cnw_ref[...]
    dnw = dnw_ref[...]
    cc = cc_ref[...]
    ss = ss_ref[...]
    rows = lax.broadcasted_iota(jnp.int32, (nb, DV_C), 0)
    g = _log_sigmoid(_dot(glr_ref[...].astype(BF16), w2_ref[...]) + b2_ref[...]) / GLA_TAU

    for h in range(H_C):
        q = proj_ref[:, O_QC + h * DK_C:O_QC + (h + 1) * DK_C].astype(F32) * (DK_C ** -0.5)
        k = proj_ref[:, O_KC + h * DK_C:O_KC + (h + 1) * DK_C]
        v = proj_ref[:, O_VC + h * DV_C:O_VC + (h + 1) * DV_C].astype(F32)
        eg = jnp.exp(g[:, h * DK_C:(h + 1) * DK_C])
        qe = _pad_rows16(q * eg)
        kt = _transpose_bf16(k)
        egt = _transpose(eg)
        qs = jnp.zeros((nb, DV_C), F32)
        for b in range(nb):
            qs = _row_select(rows, b, qs, _dot(qe, gla_ref[b, h].astype(BF16))[0:nb])
        o = qs + jnp.sum(q * k.astype(F32), axis=-1, keepdims=True) * v
        for b in range(nb):
            vsel = _pad_rows16(jnp.where(rows == b, v, 0.0))
            gla_o[b, h] = egt[:, b:b + 1] * gla_ref[b, h] + _dot(kt, vsel)
        z = proj_ref[:, O_Z + h * DV_C:O_Z + (h + 1) * DV_C].astype(F32)
        y_ref[:, h * DV_C:(h + 1) * DV_C] = _rms_rows(o, cnw) * _silu(z)

    for h in range(H_D):
        gamma = math.exp(_ret_log_gamma(h))
        qd = proj_ref[:, O_QD + h * DK_D:O_QD + (h + 1) * DK_D].astype(F32)
        kd = proj_ref[:, O_KD + h * DK_D:O_KD + (h + 1) * DK_D].astype(F32)
        qd = qd * cc + pltpu.roll(qd, DK_D // 2, 1) * ss
        kd = (kd * cc + pltpu.roll(kd, DK_D // 2, 1) * ss) * (DK_D ** -0.5)
        v = proj_ref[:, O_VD + h * DV_D:O_VD + (h + 1) * DV_D].astype(F32)
        qb = _pad_rows16(qd)
        kt = _transpose_bf16(kd)
        qs = jnp.zeros((nb, DV_D), F32)
        for b in range(nb):
            qs = _row_select(rows, b, qs, _dot(qb, ret_ref[b, h].astype(BF16))[0:nb])
        o = gamma * qs + jnp.sum(qd * kd, axis=-1, keepdims=True) * v
        for b in range(nb):
            vsel = _pad_rows16(jnp.where(rows == b, v, 0.0))
            ret_o[b, h] = gamma * ret_ref[b, h] + _dot(kt, vsel)
        z = proj_ref[:, O_Z + H_C * DV_C + h * DV_D:
                     O_Z + H_C * DV_C + (h + 1) * DV_D].astype(F32)
        y_ref[:, H_C * DV_C + h * DV_D:H_C * DV_C + (h + 1) * DV_D] = (
            _rms_rows(o, dnw) * _silu(z))


def _odd_sample(proj, glr, w2, b2, gla, ret, cc, ss, cnw, dnw):
    batch = proj.shape[0]
    nb = DEC_BB
    r2 = lambda i: (i, 0)
    i4 = lambda i: (i, 0, 0, 0)
    c2 = lambda i: (0, 0)
    return pl.pallas_call(
        _odd_sample_kernel,
        grid=(batch // nb,),
        in_specs=[
            pl.BlockSpec((nb, P_O), r2),
            pl.BlockSpec((nb, SMALL_O), r2),
            pl.BlockSpec((LANE, H_C * DK_C), c2),
            pl.BlockSpec((1, H_C * DK_C), c2),
            pl.BlockSpec((nb, H_C, DK_C, DV_C), i4),
            pl.BlockSpec((nb, H_D, DK_D, DV_D), i4),
            pl.BlockSpec((1, DK_D), c2),
            pl.BlockSpec((1, DK_D), c2),
            pl.BlockSpec((1, DV_C), c2),
            pl.BlockSpec((1, DV_D), c2),
        ],
        out_specs=[
            pl.BlockSpec((nb, W_MIX), r2),
            pl.BlockSpec((nb, H_C, DK_C, DV_C), i4),
            pl.BlockSpec((nb, H_D, DK_D, DV_D), i4),
        ],
        out_shape=[
            jax.ShapeDtypeStruct((batch, W_MIX), F32),
            jax.ShapeDtypeStruct((batch, H_C, DK_C, DV_C), F32),
            jax.ShapeDtypeStruct((batch, H_D, DK_D, DV_D), F32),
        ],
        compiler_params=pltpu.CompilerParams(
            dimension_semantics=("arbitrary",), vmem_limit_bytes=VMEM_LIMIT),
        name="odd_sample",
    )(proj, glr, w2, b2, gla, ret, cc, ss, cnw, dnw)


def _pad_cols(w, width):
    return jnp.pad(w, ((0, 0), (0, width - w.shape[1])))


def _prep_even_w_in(w):
    u = w[:, 0:3072]
    beta = w[:, 3072:3080]
    a = w[:, 3080:3088]
    qb = w[:, 3088:3600].reshape(-1, H_B, DK_B)
    kb = w[:, 3600:4112].reshape(-1, H_B, DK_B)
    vb = w[:, 4112:5136]
    ig = w[:, 5136:5144]
    fg = w[:, 5144:5152]
    op = w[:, 5152:6176]
    z = w[:, 6176:8224]
    qk = jnp.concatenate([qb, kb], axis=-1).reshape(-1, H_B * 2 * DK_B)
    small = jnp.concatenate([_pad_cols(beta, 128), _pad_cols(a, 128), _pad_cols(ig, 128),
                             _pad_cols(fg, 128)], axis=1)
    return jnp.concatenate([u, qk, vb, op, z], axis=1).astype(BF16), small.astype(BF16)


def _prep_odd_w_in(w):
    qc = w[:, 0:512]
    kc = w[:, 512:1024]
    vc = w[:, 1024:2048]
    glr = w[:, 2048:2064]
    qd = w[:, 2064:2576]
    kd = w[:, 2576:3088]
    vd = w[:, 3088:4112]
    z = w[:, 4112:6160]
    return (jnp.concatenate([qc, kc, vc, qd, kd, vd, z], axis=1).astype(BF16),
            _pad_cols(glr, SMALL_O).astype(BF16))


def _lane_row(vec, width=LANE):
    return jnp.pad(vec.astype(F32), (0, width - vec.shape[0]))


def _rotary_tables(pos):
    half = DK_D // 2
    inv = ROPE_BASE ** (-jnp.arange(half, dtype=F32) / half)
    ang = pos.astype(F32)[:, None] * inv[None, :]
    cos = jnp.cos(ang)
    sin = jnp.sin(ang)
    return jnp.concatenate([cos, cos], axis=-1), jnp.concatenate([-sin, sin], axis=-1)


def kernel(x_prompt, x_sample, c_prompt, c_sample, state_gdn, state_gdn_conv, state_mlstm_c,
           state_mlstm_n, state_mlstm_m, state_gla, state_ret, ada_w, ada_b, norm_w, ev_w_in,
           ev_w_out, gdn_conv_w, gdn_a_log, gdn_dt_bias, gdn_norm_w, mlstm_gate_b, mlstm_norm_w,
           od_w_in, od_w_out, gla_w2, gla_b2, gla_norm_w, ret_norm_w, final_norm_w):
    bp, tp, d = x_prompt.shape
    bs = x_sample.shape[0]
    n_p = bp * tp

    w_in_e, w_sm_e = _prep_even_w_in(ev_w_in[0])
    w_in_o, w_sm_o = _prep_odd_w_in(od_w_in[0])
    w_out_e = ev_w_out[0].astype(BF16)
    w_out_o = od_w_out[0].astype(BF16)
    sp = jnp.stack([_lane_row(gdn_dt_bias[0]), _lane_row(gdn_a_log[0]),
                    _lane_row(mlstm_gate_b[0, :H_B]), _lane_row(mlstm_gate_b[0, H_B:])]
                   + [jnp.zeros((LANE,), F32)] * 4)
    gnw = gdn_norm_w[0].reshape(1, DV_A)
    mnw = mlstm_norm_w[0].reshape(1, DV_B)
    cnw = gla_norm_w[0].reshape(1, DV_C)
    dnw = ret_norm_w[0].reshape(1, DV_D)
    w2 = jnp.pad(gla_w2[0], ((0, LANE - GLA_RANK), (0, 0))).astype(BF16)
    b2 = gla_b2[0].reshape(1, H_C * DK_C)

    mod = _modulation(jnp.concatenate([c_prompt, c_sample], axis=0), ada_w, ada_b)
    mod_p = [mod[l, :bp].reshape(bp, 1, 3 * d) for l in range(2)]
    mod_s = [mod[l, bp:].reshape(1, bs, 3 * d) for l in range(2)]

    xp = x_prompt.reshape(n_p, d)
    proj, small = _inproj(xp, mod_p[0], norm_w[0], w_in_e, w_sm_e, 1024, 2048)
    y, p_gdn, p_conv, p_mc, p_mn, p_mm = _even_prompt(
        proj.reshape(bp, tp, P_E), small.reshape(bp, tp, SMALL_E), sp, gdn_conv_w[0], gnw, mnw,
        bp, tp)
    xp = _outproj(y.reshape(n_p, W_MIX), xp, mod_p[0], w_out_e, final_norm_w, 512, False)
    proj, small = _inproj(xp, mod_p[1], norm_w[1], w_in_o, w_sm_o, 1024, 2048)
    cc_p, ss_p = _rotary_tables(jnp.arange(tp))
    y, p_gla, p_ret = _odd_prompt(proj, small, w2, b2, cc_p, ss_p, cnw, dnw, bp, tp)
    y_prompt = _outproj(y, xp, mod_p[1], w_out_o, final_norm_w, 512, True).reshape(bp, tp, d)

    xs = x_sample.reshape(bs, d)
    proj, small = _inproj(xs, mod_s[0], norm_w[0], w_in_e, w_sm_e, bs, 2048)
    mm_in = jnp.pad(state_mlstm_m[0], ((0, 0), (0, LANE - H_B)))
    y, s_conv, s_gdn, s_mc, s_mn, s_mm = _even_sample(
        proj, small, jnp.transpose(state_gdn_conv[0], (1, 0, 2)), state_gdn[0], state_mlstm_c[0],
        jnp.transpose(state_mlstm_n[0], (1, 0, 2)), mm_in, sp, gdn_conv_w[0], gnw, mnw)
    xs = _outproj(y, xs, mod_s[0], w_out_e, final_norm_w, bs, False)
    proj, small = _inproj(xs, mod_s[1], norm_w[1], w_in_o, w_sm_o, bs, 2048)
    cc_s, ss_s = _rotary_tables(PAST_LEN + jnp.arange(1))
    y, s_gla, s_ret = _odd_sample(proj, small, w2, b2, state_gla[0], state_ret[0], cc_s, ss_s,
                                  cnw, dnw)
    y_sample = _outproj(y, xs, mod_s[1], w_out_o, final_norm_w, bs, True)
    y_sample = y_sample.reshape(bs, 1, d)

    return (y_prompt, y_sample,
            p_gdn[None], p_conv[None], p_mc[None], p_mn[None], p_mm[None, :, 0, :H_B],
            p_gla[None], p_ret[None],
            s_gdn[None], jnp.transpose(s_conv, (1, 0, 2))[None], s_mc[None],
            jnp.transpose(s_mn, (1, 0, 2))[None], s_mm[None, :, :H_B], s_gla[None], s_ret[None])
```

```python
import functools
import math

import jax
import jax.numpy as jnp
import numpy as np
from jax import lax
from jax.experimental import pallas as pl
from jax.experimental.pallas import tpu as pltpu

F32 = jnp.float32
BF16 = jnp.bfloat16
HI = lax.Precision.HIGHEST

D_MODEL = 1024
W_MIX = 2048
H_A, DK_A, DV_A, CONV_W = 8, 128, 128, 4
CONV_CH = H_A * (2 * DK_A + DV_A)
H_B, DK_B, DV_B = 8, 64, 128
H_C, DK_C, DV_C, GLA_RANK, GLA_TAU = 4, 128, 256, 16, 16.0
H_D, DK_D, DV_D = 4, 128, 256
ROPE_BASE = 10000.0
PAST_LEN = 16384
EPS = 1e-6
LANE = 128

CHUNK = 64
CHUNKS_PER_STEP = 4

E_U, E_QK, E_V, E_OP, E_Z = 0, 3072, 4096, 5120, 6144
P_E = 8192
SMALL_E = 512
O_QC, O_KC, O_VC, O_QD, O_KD, O_VD, O_Z = 0, 512, 1024, 2048, 2560, 3072, 4096
P_O = 6144
SMALL_O = 128

VMEM_LIMIT = 56 * 1024 * 1024


def _sigmoid(x):
    return 0.5 * jnp.tanh(0.5 * x) + 0.5


def _softplus(x):
    return jnp.maximum(x, 0.0) + jnp.log(1.0 + jnp.exp(-jnp.abs(x)))


def _log_sigmoid(x):
    return -_softplus(-x)


def _silu(x):
    half = 0.5 * x
    return half + half * jnp.tanh(half)


def _rms_rows(x, w):
    return x * lax.rsqrt(jnp.mean(x * x, axis=-1, keepdims=True) + EPS) * w


def _l2n(x):
    return x * lax.rsqrt(jnp.sum(x * x, axis=-1, keepdims=True) + EPS)


def _dot(a, b, precision=None):
    return jnp.dot(a, b, preferred_element_type=F32, precision=precision)


def _dot_nt(a, b, precision=None):
    return lax.dot_general(a, b, (((1,), (1,)), ((), ())), preferred_element_type=F32,
                           precision=precision)


def _dot_tn(a, b, precision=None):
    return lax.dot_general(a, b, (((0,), (0,)), ((), ())), preferred_element_type=F32,
                           precision=precision)


def _eye(n):
    r = lax.broadcasted_iota(jnp.int32, (n, n), 0)
    c = lax.broadcasted_iota(jnp.int32, (n, n), 1)
    return (r == c).astype(F32)


def _bf16_terms(x, terms):
    parts = []
    rest = x
    for _ in range(terms):
        piece = rest.astype(BF16)
        parts.append(piece)
        rest = rest - piece.astype(F32)
    return parts


def _select_dot(sel, x, terms):
    n = x.shape[1]
    out = _dot(sel.astype(BF16), jnp.concatenate(_bf16_terms(x, terms), axis=1))
    acc = out[:, 0:n]
    for i in range(1, terms):
        acc = acc + out[:, i * n:(i + 1) * n]
    return acc


def _transpose(x):
    return _dot_nt(_eye(x.shape[1]), x, precision=HI)


def _tri_masks(n):
    r = lax.broadcasted_iota(jnp.int32, (n, n), 0)
    c = lax.broadcasted_iota(jnp.int32, (n, n), 1)
    return r >= c, r > c


def _mod_kernel(c_ref, w_ref, b_ref, o_ref):
    cs = _silu(c_ref[...]).astype(BF16)
    o_ref[...] = _dot(cs, w_ref[...].astype(BF16)) + b_ref[...]


def _modulation(c_all, ada_w, ada_b):
    depth, d, d3 = ada_w.shape
    rows = c_all.shape[0]
    tn = 1024
    return pl.pallas_call(
        _mod_kernel,
        grid=(depth, d3 // tn),
        in_specs=[
            pl.BlockSpec((rows, d), lambda l, j: (0, 0)),
            pl.BlockSpec((None, d, tn), lambda l, j: (l, 0, j)),
            pl.BlockSpec((None, 1, tn), lambda l, j: (l, 0, j)),
        ],
        out_specs=pl.BlockSpec((None, rows, tn), lambda l, j: (l, 0, j)),
        out_shape=jax.ShapeDtypeStruct((depth, rows, d3), F32),
        compiler_params=pltpu.CompilerParams(
            dimension_semantics=("arbitrary", "arbitrary"), vmem_limit_bytes=VMEM_LIMIT),
        name="modulation",
    )(c_all, ada_w, ada_b.reshape(depth, 1, d3))


def _inproj_kernel(x_ref, mod_ref, nw_ref, w_ref, ws_ref, o_ref, os_ref, h_ref):
    @pl.when(pl.program_id(1) == 0)
    def _():
        x = x_ref[...]
        y = _rms_rows(x, nw_ref[...])
        shift = mod_ref[:, 0:D_MODEL]
        scale = mod_ref[:, D_MODEL:2 * D_MODEL]
        h_ref[...] = (y * (1.0 + scale) + shift).astype(BF16)
        os_ref[...] = _dot(h_ref[...], ws_ref[...])

    o_ref[...] = _dot(h_ref[...], w_ref[...]).astype(o_ref.dtype)


def _inproj(x, mod, norm_w, w, w_small, tm, tn):
    n, d = x.shape
    p = w.shape[1]
    ps = w_small.shape[1]
    g, r, _ = mod.shape
    tiles_per_group = (n // g) // tm
    return pl.pallas_call(
        _inproj_kernel,
        grid=(n // tm, p // tn),
        in_specs=[
            pl.BlockSpec((tm, d), lambda i, j: (i, 0)),
            pl.BlockSpec((None, r, 3 * d), lambda i, j: (i // tiles_per_group, 0, 0)),
            pl.BlockSpec((1, d), lambda i, j: (0, 0)),
            pl.BlockSpec((d, tn), lambda i, j: (0, j)),
            pl.BlockSpec((d, ps), lambda i, j: (0, 0)),
        ],
        out_specs=[pl.BlockSpec((tm, tn), lambda i, j: (i, j)),
                   pl.BlockSpec((tm, ps), lambda i, j: (i, 0))],
        out_shape=[jax.ShapeDtypeStruct((n, p), BF16), jax.ShapeDtypeStruct((n, ps), F32)],
        scratch_shapes=[pltpu.VMEM((tm, d), BF16)],
        compiler_params=pltpu.CompilerParams(
            dimension_semantics=("arbitrary", "arbitrary"), vmem_limit_bytes=VMEM_LIMIT),
        name="inproj",
    )(x, mod, norm_w.reshape(1, d), w, w_small)


def _outproj_kernel(y_ref, x_ref, mod_ref, w_ref, fw_ref, o_ref, *, final):
    acc = _dot(y_ref[...].astype(BF16), w_ref[...])
    gate = mod_ref[:, 2 * D_MODEL:3 * D_MODEL]
    xn = x_ref[...] + gate * acc
    if final:
        xn = _rms_rows(xn, fw_ref[...])
    o_ref[...] = xn


def _outproj(y, x, mod, w, final_w, tm, final):
    n, d = x.shape
    k = y.shape[1]
    g, r, _ = mod.shape
    tiles_per_group = (n // g) // tm
    return pl.pallas_call(
        functools.partial(_outproj_kernel, final=final),
        grid=(n // tm,),
        in_specs=[
            pl.BlockSpec((tm, k), lambda i: (i, 0)),
            pl.BlockSpec((tm, d), lambda i: (i, 0)),
            pl.BlockSpec((None, r, 3 * d), lambda i: (i // tiles_per_group, 0, 0)),
            pl.BlockSpec((k, d), lambda i: (0, 0)),
            pl.BlockSpec((1, d), lambda i: (0, 0)),
        ],
        out_specs=pl.BlockSpec((tm, d), lambda i: (i, 0)),
        out_shape=jax.ShapeDtypeStruct((n, d), F32),
        compiler_params=pltpu.CompilerParams(
            dimension_semantics=("arbitrary",), vmem_limit_bytes=VMEM_LIMIT),
        name="outproj",
    )(y, x, mod, w, final_w.reshape(1, d))


def _neumann_correction(a_list, n):
    ys = [-a for a in a_list]
    pbs = [a.astype(BF16) for a in a_list]
    for _ in range(int(math.log2(n)) - 1):
        for i in range(len(ys)):
            p = _dot(pbs[i], pbs[i])
            pbs[i] = p.astype(BF16)
            ys[i] = ys[i] + p + _dot(ys[i].astype(BF16), pbs[i])
        yield
    idx = range(len(ys))
    a_h = [a.astype(BF16) for a in a_list]
    a_l = [(a_list[i] - a_h[i].astype(F32)).astype(BF16) for i in idx]
    y_h = [y.astype(BF16) for y in ys]
    y_l = [(ys[i] - y_h[i].astype(F32)).astype(BF16) for i in idx]
    ay = [_dot(a_h[i], y_h[i]) + _dot(a_h[i], y_l[i]) + _dot(a_l[i], y_h[i]) for i in idx]
    e = [-(a_list[i] + ys[i] + ay[i]) for i in idx]
    return [ys[i] + e[i] + _dot(y_h[i], e[i].astype(BF16)) for i in idx]


def _run_staggered(stage_generators, offset):
    pending = list(stage_generators)
    active = []
    tick = 0
    while pending or active:
        if pending and tick % offset == 0:
            active.append(pending.pop(0))
        for g in list(active):
            try:
                next(g)
            except StopIteration:
                active.remove(g)
        tick += 1


EVEN_STAGGER = 16
ODD_STAGGER = 10


def _conv_shift_matrix():
    n = CHUNK
    r = np.arange((CONV_W - 1) * n)[:, None]
    col = np.arange(2 * n)[None, :]
    return jnp.asarray(col == n + (r % n) - (r // n + 1), BF16)


def _even_prompt_kernel(proj_ref, sm_ref, sp_ref, cw_ref, shift_ref, gnw_ref, mnw_ref,
                        y_ref, gdn_ref, conv_ref, mc_ref, mn_ref, mm_ref,
                        uprev_ref, qkv_ref):
    chunks = []
    for j in range(CHUNKS_PER_STEP):
        rows = pl.ds(j * CHUNK, CHUNK)
        chunks.append(_even_prompt_sequence(
            j == 0, proj_ref.at[0, rows], sm_ref.at[0, rows], sp_ref, cw_ref, shift_ref, gnw_ref,
            mnw_ref, y_ref.at[0, rows], gdn_ref.at[0], conv_ref.at[0], mc_ref.at[0], mn_ref.at[0],
            mm_ref.at[0], uprev_ref.at[0], qkv_ref.at[0]))
    _run_staggered(chunks, EVEN_STAGGER)


def _even_prompt_sequence(first_in_step, proj_ref, sm_ref, sp_ref, cw_ref, shift_ref, gnw_ref,
                          mnw_ref, y_ref, gdn_ref, conv_ref, mc_ref, mn_ref, mm_ref,
                          uprev_ref, qkv_ref):
    L = CHUNK

    if first_in_step:
        @pl.when(pl.program_id(1) == 0)
        def _():
            gdn_ref[...] = jnp.zeros_like(gdn_ref)
            mc_ref[...] = jnp.zeros_like(mc_ref)
            mn_ref[...] = jnp.zeros_like(mn_ref)
            mm_ref[...] = jnp.zeros_like(mm_ref)
            uprev_ref[...] = jnp.zeros_like(uprev_ref)

    tri_incl, tri_strict = _tri_masks(L)
    tri_f = tri_incl.astype(F32)

    ub = proj_ref[:, E_U:E_U + CONV_CH]
    shifted = _dot(shift_ref[...], jnp.concatenate([uprev_ref[...], ub], axis=0))
    uprev_ref[...] = ub
    u = ub.astype(F32)
    cw = cw_ref[...]
    uc = (cw[0:1] * shifted[2 * L:3 * L] + cw[1:2] * shifted[L:2 * L]
          + cw[2:3] * shifted[0:L] + cw[3:4] * u)
    conv_ref[...] = u[L - 3:L, :]
    qkv_ref[...] = _silu(uc)
    yield

    sm = sm_ref[...]
    head_lane = lax.broadcasted_iota(jnp.int32, (L, LANE), 1) < H_A
    beta = _sigmoid(sm[:, 0:128])
    g = -jnp.exp(sp_ref[1:2, :]) * _softplus(sm[:, 128:256] + sp_ref[0:1, :])
    ig = sm[:, 256:384] + sp_ref[2:3, :]
    lf = _log_sigmoid(sm[:, 384:512] + sp_ref[3:4, :])
    packed = jnp.where(head_lane, g, 0.0) + pltpu.roll(jnp.where(head_lane, lf, 0.0), H_A, 1)
    cum = _dot(tri_f, packed, precision=HI)
    cum_t = _transpose(cum + pltpu.roll(jnp.where(head_lane, ig, 0.0), H_A + H_B, 1))

    gnw = gnw_ref[...]
    mnw = mnw_ref[...]
    yield

    qkb, kf, decay, beta_c, gam, bh_c = [], [], [], [], [], []
    for h in range(H_A):
        q = _l2n(qkv_ref[:, h * DK_A:(h + 1) * DK_A]) * (DK_A ** -0.5)
        k = _l2n(qkv_ref[:, H_A * DK_A + h * DK_A:H_A * DK_A + (h + 1) * DK_A])
        kf.append(k)
        qkb.append(jnp.concatenate([q, k], axis=0).astype(BF16))
    yield
    qkk = [_dot_nt(qkb[h], qkb[h][L:2 * L]) for h in range(H_A)]
    yield
    a_list = []
    for h in range(H_A):
        bh_c.append(cum[:, h:h + 1])
        diff = bh_c[h] - cum_t[h:h + 1, :]
        decay.append(jnp.where(tri_incl, jnp.exp(jnp.where(tri_incl, diff, 0.0)), 0.0))
        beta_c.append(beta[:, h:h + 1])
        gam.append(jnp.exp(bh_c[h]))
        a_list.append(jnp.where(tri_strict, decay[h] * qkk[h][L:2 * L], 0.0) * beta_c[h])
    yield
    corr = yield from _neumann_correction(a_list, L)
    yield

    heads = range(H_A)
    s_old = [gdn_ref[h] for h in heads]
    qks = [_dot(qkb[h], s_old[h].astype(BF16)) for h in heads]
    yield
    rhs = [beta_c[h] * (qkv_ref[:, 2 * H_A * DK_A + h * DV_A:2 * H_A * DK_A + (h + 1) * DV_A]
                        - gam[h] * qks[h][L:2 * L]) for h in heads]
    ub = [(rhs[h] + _dot(corr[h].astype(BF16), rhs[h].astype(BF16))).astype(BF16) for h in heads]
    yield
    o = [gam[h] * qks[h][0:L] + _dot((qkk[h][0:L] * decay[h]).astype(BF16), ub[h])
         for h in heads]
    yield
    for h in heads:
        last = bh_c[h][L - 1:L, :]
        kw = (kf[h] * jnp.exp(last - bh_c[h])).astype(BF16)
        gdn_ref[h] = jnp.exp(last) * s_old[h] + _dot_tn(kw, ub[h])
    yield
    for h in heads:
        z = proj_ref[:, E_Z + h * DV_A:E_Z + (h + 1) * DV_A].astype(F32)
        y_ref[:, h * DV_A:(h + 1) * DV_A] = (_rms_rows(o[h], gnw) * _silu(z)).astype(BF16)
    yield

    heads = range(H_B)
    qs, qbs, kbs, vbs = [], [], [], []
    for h in heads:
        qk_in = proj_ref[:, E_QK + h * 128:E_QK + (h + 1) * 128]
        qbs.append(qk_in[:, 0:DK_B])
        qs.append(qbs[h].astype(F32))
        kbs.append((qk_in[:, DK_B:2 * DK_B].astype(F32) * (DK_B ** -0.5)).astype(BF16))
        vbs.append(proj_ref[:, E_V + h * DV_B:E_V + (h + 1) * DV_B])
    qkm = [_dot_nt(qbs[h], kbs[h]) for h in heads]
    yield
    b_c = [cum[:, H_A + h:H_A + h + 1] for h in heads]
    m0 = [mm_ref[0:1, h:h + 1] for h in heads]
    dmat = [jnp.where(tri_incl, b_c[h] - cum_t[H_A + h:H_A + h + 1, :]
                      + cum_t[H_A + H_B + h:H_A + H_B + h + 1, :], -jnp.inf) for h in heads]
    w0 = [b_c[h] + m0[h] for h in heads]
    m_t = [jnp.maximum(w0[h], jnp.max(dmat[h], axis=-1, keepdims=True)) for h in heads]
    p = [jnp.exp(dmat[h] - m_t[h]) * qkm[h] for h in heads]
    s0 = [jnp.exp(w0[h] - m_t[h]) for h in heads]
    yield
    c_old = [mc_ref[h] for h in heads]
    n_old = [mn_ref[h:h + 1, :] for h in heads]
    num = [s0[h] * _dot(qbs[h], c_old[h].astype(BF16)) + _dot(p[h].astype(BF16), vbs[h])
           for h in heads]
    den = [s0[h] * jnp.sum(qs[h] * n_old[h], axis=-1, keepdims=True)
           + jnp.sum(p[h], axis=-1, keepdims=True) for h in heads]
    hh = [num[h] / jnp.maximum(jnp.abs(den[h]), jnp.exp(-m_t[h])) for h in heads]
    yield
    for h in heads:
        m_end = m_t[h][L - 1:L, :]
        b_last = b_c[h][L - 1:L, :]
        we = jnp.exp(b_last - b_c[h] + ig[:, h:h + 1] - m_end)
        se = jnp.exp(b_last + m0[h] - m_end)
        kw = kbs[h].astype(F32) * we
        mc_ref[h] = se * c_old[h] + _dot_tn(kw.astype(BF16), vbs[h])
        mn_ref[h:h + 1, :] = se * n_old[h] + jnp.sum(kw, axis=0, keepdims=True)
        mm_ref[0:1, h:h + 1] = m_end
    yield
    for h in heads:
        o_pre = proj_ref[:, E_OP + h * DV_B:E_OP + (h + 1) * DV_B].astype(F32)
        z = proj_ref[:, E_Z + H_A * DV_A + h * DV_B:
                     E_Z + H_A * DV_A + (h + 1) * DV_B].astype(F32)
        y_ref[:, H_A * DV_A + h * DV_B:H_A * DV_A + (h + 1) * DV_B] = (
            _sigmoid(o_pre) * _rms_rows(hh[h], mnw) * _silu(z)).astype(BF16)


def _even_prompt(proj, small, sp, conv_w, gnw, mnw, batch, seq):
    step = CHUNK * CHUNKS_PER_STEP
    nc = seq // step
    nb = 1
    row = lambda b, c: (b, c, 0)
    const2 = lambda b, c: (0, 0)
    st3 = lambda b, c: (b, 0, 0)
    st4 = lambda b, c: (b, 0, 0, 0)
    return pl.pallas_call(
        _even_prompt_kernel,
        grid=(batch // nb, nc),
        in_specs=[
            pl.BlockSpec((nb, step, P_E), row),
            pl.BlockSpec((nb, step, SMALL_E), row),
            pl.BlockSpec((8, LANE), const2),
            pl.BlockSpec((CONV_W, CONV_CH), const2),
            pl.BlockSpec(((CONV_W - 1) * CHUNK, 2 * CHUNK), const2),
            pl.BlockSpec((1, DV_A), const2),
            pl.BlockSpec((1, DV_B), const2),
        ],
        out_specs=[
            pl.BlockSpec((nb, step, W_MIX), row),
            pl.BlockSpec((nb, H_A, DK_A, DV_A), st4),
            pl.BlockSpec((nb, CONV_W - 1, CONV_CH), st3),
            pl.BlockSpec((nb, H_B, DK_B, DV_B), st4),
            pl.BlockSpec((nb, H_B, DK_B), st3),
            pl.BlockSpec((nb, 1, LANE), st3),
        ],
        out_shape=[
            jax.ShapeDtypeStruct((batch, seq, W_MIX), BF16),
            jax.ShapeDtypeStruct((batch, H_A, DK_A, DV_A), F32),
            jax.ShapeDtypeStruct((batch, CONV_W - 1, CONV_CH), F32),
            jax.ShapeDtypeStruct((batch, H_B, DK_B, DV_B), F32),
            jax.ShapeDtypeStruct((batch, H_B, DK_B), F32),
            jax.ShapeDtypeStruct((batch, 1, LANE), F32),
        ],
        scratch_shapes=[pltpu.VMEM((nb, CHUNK, CONV_CH), BF16),
                        pltpu.VMEM((nb, CHUNK, CONV_CH), F32)],
        compiler_params=pltpu.CompilerParams(
            dimension_semantics=("arbitrary", "arbitrary"), vmem_limit_bytes=VMEM_LIMIT),
        name="even_prompt",
    )(proj, small, sp, conv_w, _conv_shift_matrix(), gnw, mnw)


def _ret_log_gamma(h):
    return math.log(1.0 - 2.0 ** (-5.0 - h))


GLA_LEVELS = int(math.log2(CHUNK))
GLA_MM_LEVELS = (1, 2)
LOG2E = math.log2(math.e)


def _gla_chunk_tables():
    n = CHUNK
    t = np.arange(n)[:, None]
    i = np.arange(n)[None, :]
    blocks = [(i <= t)]
    for lvl in GLA_MM_LEVELS:
        hs = 1 << lvl
        p = (t & ~(2 * hs - 1)) + hs - 1
        upper = (t & hs) != 0
        blocks.append(np.where(upper, (i > p) & (i <= t), (i > t) & (i <= p)))
    sums = np.concatenate(blocks, axis=0).astype(np.float32)
    s = np.arange(n)[None, :]
    level = np.full((n, n), GLA_LEVELS + 1, np.int32)
    level[np.arange(n), np.arange(n)] = GLA_LEVELS
    for lvl in range(GLA_LEVELS):
        hs = 1 << lvl
        same = (t >> (lvl + 1)) == (s >> (lvl + 1))
        level[same & ((t & hs) != 0) & ((s & hs) == 0)] = lvl
    return jnp.asarray(sums), jnp.asarray(level)


def _odd_prompt_kernel(proj_ref, glr_ref, w2_ref, b2_ref, sums_ref, level_ref, cc_ref, ss_ref,
                       cnw_ref, dnw_ref, y_ref, gla_ref, ret_ref, x_ref):
    chunks = []
    for j in range(CHUNKS_PER_STEP):
        rows = pl.ds(j * CHUNK, CHUNK)
        chunks.append(_odd_prompt_chunk(
            j == 0, proj_ref.at[rows], glr_ref.at[rows], w2_ref, b2_ref, sums_ref, level_ref,
            cc_ref.at[rows], ss_ref.at[rows], cnw_ref, dnw_ref, y_ref.at[rows], gla_ref, ret_ref,
            x_ref))
    _run_staggered(chunks, ODD_STAGGER)


def _odd_prompt_chunk(first_in_step, proj_ref, glr_ref, w2_ref, b2_ref, sums_ref, level_ref,
                      cc_ref, ss_ref, cnw_ref, dnw_ref, y_ref, gla_ref, ret_ref, x_ref):
    L = CHUNK

    if first_in_step:
        @pl.when(pl.program_id(1) == 0)
        def _():
            gla_ref[...] = jnp.zeros_like(gla_ref)
            ret_ref[...] = jnp.zeros_like(ret_ref)

    tri_incl, _ = _tri_masks(L)
    r64 = lax.broadcasted_iota(jnp.int32, (L, L), 0)
    c64 = lax.broadcasted_iota(jnp.int32, (L, L), 1)
    level = level_ref[...]

    glr = glr_ref[...].astype(BF16)
    gc2 = _log_sigmoid(_dot(glr, w2_ref[...]) + b2_ref[...]) * (LOG2E / GLA_TAU)
    x_ref[...] = _select_dot(sums_ref[...], gc2, 3)
    odd_row = (lax.broadcasted_iota(jnp.int32, (L, DK_C), 0) & 1) == 1
    yield

    def level_exponent(lvl, h):
        cols = slice(h * DK_C, (h + 1) * DK_C)
        if lvl == 0:
            return jnp.where(odd_row, gc2[:, cols], 0.0)
        if lvl in GLA_MM_LEVELS:
            r0 = (1 + GLA_MM_LEVELS.index(lvl)) * L
            return x_ref[r0:r0 + L, cols]
        hs = 1 << lvl
        pieces = []
        for lo in range(0, L, 2 * hs):
            ref = x_ref[lo + hs - 1:lo + hs, cols]
            pieces.append(ref - x_ref[lo:lo + hs, cols])
            pieces.append(x_ref[lo + hs:lo + 2 * hs, cols] - ref)
        return jnp.concatenate(pieces, axis=0)

    cnw = cnw_ref[...]
    dnw = dnw_ref[...]

    heads = range(H_C)
    q = [proj_ref[:, O_QC + h * DK_C:O_QC + (h + 1) * DK_C].astype(F32) * (DK_C ** -0.5)
         for h in heads]
    k = [proj_ref[:, O_KC + h * DK_C:O_KC + (h + 1) * DK_C].astype(F32) for h in heads]
    v = [proj_ref[:, O_VC + h * DV_C:O_VC + (h + 1) * DV_C] for h in heads]
    att = [jnp.where(level == GLA_LEVELS, _dot_nt(q[h].astype(BF16), k[h].astype(BF16)), 0.0)
           for h in heads]
    yield
    for lvl in range(GLA_LEVELS):
        for h in heads:
            e = jnp.exp2(level_exponent(lvl, h))
            m = _dot_nt((q[h] * e).astype(BF16), (k[h] * e).astype(BF16))
            att[h] = jnp.where(level == lvl, m, att[h])
        yield
    b2 = [x_ref[0:L, h * DK_C:(h + 1) * DK_C] for h in heads]
    s_gla = [gla_ref[h] for h in heads]
    o = [_dot((q[h] * jnp.exp2(b2[h])).astype(BF16), s_gla[h].astype(BF16))
         + _dot(att[h].astype(BF16), v[h]) for h in heads]
    yield
    for h in heads:
        bl = b2[h][L - 1:L, :]
        kw = (k[h] * jnp.exp2(bl - b2[h])).astype(BF16)
        col = jnp.transpose(jnp.broadcast_to(jnp.exp2(bl), (DK_C, DK_C)))
        upd = _dot_tn(kw, v[h])
        for half in range(DV_C // DK_C):
            lanes = slice(half * DK_C, (half + 1) * DK_C)
            gla_ref[h, :, lanes] = s_gla[h][:, lanes] * col + upd[:, lanes]
    yield
    for h in heads:
        z = proj_ref[:, O_Z + h * DV_C:O_Z + (h + 1) * DV_C].astype(F32)
        y_ref[:, h * DV_C:(h + 1) * DV_C] = (_rms_rows(o[h], cnw) * _silu(z)).astype(BF16)
    yield

    cc = cc_ref[...]
    ss = ss_ref[...]
    rel = (r64 - c64).astype(F32)
    tcol = lax.broadcasted_iota(jnp.int32, (L, 1), 0).astype(F32)
    heads = range(H_D)
    lg = [_ret_log_gamma(h) for h in heads]
    qb, kf, kb, vd = [], [], [], []
    for h in heads:
        qd = proj_ref[:, O_QD + h * DK_D:O_QD + (h + 1) * DK_D].astype(F32)
        kd = proj_ref[:, O_KD + h * DK_D:O_KD + (h + 1) * DK_D].astype(F32)
        qb.append((qd * cc + pltpu.roll(qd, DK_D // 2, 1) * ss).astype(BF16))
        kf.append((kd * cc + pltpu.roll(kd, DK_D // 2, 1) * ss) * (DK_D ** -0.5))
        kb.append(kf[h].astype(BF16))
        vd.append(proj_ref[:, O_VD + h * DV_D:O_VD + (h + 1) * DV_D])
    yield
    s_old = [ret_ref[h] for h in heads]
    attd = [_dot_nt(qb[h], kb[h])
            * jnp.where(tri_incl, jnp.exp(lg[h] * jnp.maximum(rel, 0.0)), 0.0) for h in heads]
    od = [_dot(qb[h], s_old[h].astype(BF16)) * jnp.exp(lg[h] * (tcol + 1.0))
          + _dot(attd[h].astype(BF16), vd[h]) for h in heads]
    yield
    for h in heads:
        end = jnp.exp(lg[h] * (L - 1.0 - tcol))
        ret_ref[h] = math.exp(lg[h] * L) * s_old[h] + _dot_tn((kf[h] * end).astype(BF16), vd[h])
    yield
    for h in heads:
        z = proj_ref[:, O_Z + H_C * DV_C + h * DV_D:
                     O_Z + H_C * DV_C + (h + 1) * DV_D].astype(F32)
        y_ref[:, H_C * DV_C + h * DV_D:H_C * DV_C + (h + 1) * DV_D] = (
            _rms_rows(od[h], dnw) * _silu(z)).astype(BF16)


def _odd_prompt(proj, glr, w2, b2, cc, ss, cnw, dnw, batch, seq):
    step = CHUNK * CHUNKS_PER_STEP
    nc = seq // step
    row = lambda b, c: (b * nc + c, 0)
    const2 = lambda b, c: (0, 0)
    sums, level = _gla_chunk_tables()
    n_sums = sums.shape[0]
    return pl.pallas_call(
        _odd_prompt_kernel,
        grid=(batch, nc),
        in_specs=[
            pl.BlockSpec((step, P_O), row),
            pl.BlockSpec((step, SMALL_O), row),
            pl.BlockSpec((LANE, H_C * DK_C), const2),
            pl.BlockSpec((1, H_C * DK_C), const2),
            pl.BlockSpec((n_sums, CHUNK), const2),
            pl.BlockSpec((CHUNK, CHUNK), const2),
            pl.BlockSpec((step, DK_D), lambda b, c: (c, 0)),
            pl.BlockSpec((step, DK_D), lambda b, c: (c, 0)),
            pl.BlockSpec((1, DV_C), const2),
            pl.BlockSpec((1, DV_D), const2),
        ],
        out_specs=[
            pl.BlockSpec((step, W_MIX), row),
            pl.BlockSpec((None, H_C, DK_C, DV_C), lambda b, c: (b, 0, 0, 0)),
            pl.BlockSpec((None, H_D, DK_D, DV_D), lambda b, c: (b, 0, 0, 0)),
        ],
        out_shape=[
            jax.ShapeDtypeStruct((batch * seq, W_MIX), BF16),
            jax.ShapeDtypeStruct((batch, H_C, DK_C, DV_C), F32),
            jax.ShapeDtypeStruct((batch, H_D, DK_D, DV_D), F32),
        ],
        scratch_shapes=[pltpu.VMEM((n_sums, H_C * DK_C), F32)],
        compiler_params=pltpu.CompilerParams(
            dimension_semantics=("arbitrary", "arbitrary"), vmem_limit_bytes=VMEM_LIMIT),
        name="odd_prompt",
    )(proj, glr, w2, b2, sums, level, cc, ss, cnw, dnw)


DEC_BB = 8


def _row_select(rows, row_index, acc, new):
    return jnp.where(rows == row_index, new, acc)


def _pad_rows16(x):
    return jnp.concatenate([x, jnp.zeros_like(x)], axis=0).astype(BF16)


def _transpose_bf16(x):
    eye = _eye(x.shape[1]).astype(BF16)
    return _dot_nt(eye, _pad_rows16(x)).astype(BF16)


def _even_sample_kernel(proj_ref, sm_ref, conv_ref, gdn_ref, mc_ref, mn_ref, mm_ref,
                        sp_ref, cw_ref, gnw_ref, mnw_ref,
                        y_ref, conv_o, gdn_o, mc_o, mn_o, mm_o, act_ref):
    nb = DEC_BB
    gnw = gnw_ref[...]
    mnw = mnw_ref[...]
    rows = lax.broadcasted_iota(jnp.int32, (nb, LANE), 0)

    sm = sm_ref[...]
    beta = _sigmoid(sm[:, 0:128])
    gam = jnp.exp(-jnp.exp(sp_ref[1:2, :]) * _softplus(sm[:, 128:256] + sp_ref[0:1, :]))
    ig = sm[:, 256:384] + sp_ref[2:3, :]
    lf = _log_sigmoid(sm[:, 384:512] + sp_ref[3:4, :])
    w0 = lf + mm_ref[...]
    m_t = jnp.maximum(w0, ig)
    s0 = jnp.exp(w0 - m_t)
    we = jnp.exp(ig - m_t)
    floor = jnp.exp(-m_t)
    mm_o[...] = m_t

    u = proj_ref[:, E_U:E_U + CONV_CH].astype(F32)
    cw = cw_ref[...]
    uc = (cw[0:1] * conv_ref[0] + cw[1:2] * conv_ref[1] + cw[2:3] * conv_ref[2] + cw[3:4] * u)
    conv_o[0] = conv_ref[1]
    conv_o[1] = conv_ref[2]
    conv_o[2] = u
    act_ref[...] = _silu(uc)

    for h in range(H_A):
        q = _l2n(act_ref[:, h * DK_A:(h + 1) * DK_A]) * (DK_A ** -0.5)
        k = _l2n(act_ref[:, H_A * DK_A + h * DK_A:H_A * DK_A + (h + 1) * DK_A])
        v = act_ref[:, 2 * H_A * DK_A + h * DV_A:2 * H_A * DK_A + (h + 1) * DV_A]
        kq = jnp.concatenate([k, q], axis=0).astype(BF16)
        kt = _transpose_bf16(k)
        ks = jnp.zeros((nb, DV_A), F32)
        qs = jnp.zeros((nb, DV_A), F32)
        for b in range(nb):
            r = _dot(kq, gdn_ref[b, h].astype(BF16))
            ks = _row_select(rows, b, ks, r[0:nb])
            qs = _row_select(rows, b, qs, r[nb:2 * nb])
        gam_h = gam[:, h:h + 1]
        uu = beta[:, h:h + 1] * (v - gam_h * ks)
        o = gam_h * qs + jnp.sum(q * k, axis=-1, keepdims=True) * uu
        for b in range(nb):
            usel = _pad_rows16(jnp.where(rows == b, uu, 0.0))
            gdn_o[b, h] = gam[b:b + 1, h:h + 1] * gdn_ref[b, h] + _dot(kt, usel)
        z = proj_ref[:, E_Z + h * DV_A:E_Z + (h + 1) * DV_A].astype(F32)
        y_ref[:, h * DV_A:(h + 1) * DV_A] = _rms_rows(o, gnw) * _silu(z)

    for h in range(H_B):
        qk_in = proj_ref[:, E_QK + h * 128:E_QK + (h + 1) * 128]
        qb = qk_in[:, 0:DK_B]
        q = qb.astype(F32)
        k = qk_in[:, DK_B:2 * DK_B].astype(F32) * (DK_B ** -0.5)
        v = proj_ref[:, E_V + h * DV_B:E_V + (h + 1) * DV_B].astype(F32)
        kt = _transpose_bf16(k)
        q16 = _pad_rows16(qb)
        qc = jnp.zeros((nb, DV_B), F32)
        for b in range(nb):
            r = _dot(q16, mc_ref[b, h].astype(BF16))
            qc = _row_select(rows, b, qc, r[0:nb])
        s0_h = s0[:, h:h + 1]
        we_h = we[:, h:h + 1]
        wv = we_h * v
        nn = s0_h * mn_ref[h] + we_h * k
        mn_o[h] = nn
        num = s0_h * qc + jnp.sum(q * k, axis=-1, keepdims=True) * wv
        den = jnp.sum(q * nn, axis=-1, keepdims=True)
        hh = num / jnp.maximum(jnp.abs(den), floor[:, h:h + 1])
        for b in range(nb):
            wsel = _pad_rows16(jnp.where(rows == b, wv, 0.0))
            mc_o[b, h] = s0[b:b + 1, h:h + 1] * mc_ref[b, h] + _dot(kt, wsel)
        o_pre = proj_ref[:, E_OP + h * DV_B:E_OP + (h + 1) * DV_B].astype(F32)
        z = proj_ref[:, E_Z + H_A * DV_A + h * DV_B:
                     E_Z + H_A * DV_A + (h + 1) * DV_B].astype(F32)
        y_ref[:, H_A * DV_A + h * DV_B:H_A * DV_A + (h + 1) * DV_B] = (
            _sigmoid(o_pre) * _rms_rows(hh, mnw) * _silu(z))


def _even_sample(proj, small, conv, gdn, mc, mn, mm, sp, cw, gnw, mnw):
    batch = proj.shape[0]
    nb = DEC_BB
    r2 = lambda i: (i, 0)
    m3 = lambda i: (0, i, 0)
    i4 = lambda i: (i, 0, 0, 0)
    c2 = lambda i: (0, 0)
    return pl.pallas_call(
        _even_sample_kernel,
        grid=(batch // nb,),
        in_specs=[
            pl.BlockSpec((nb, P_E), r2),
            pl.BlockSpec((nb, SMALL_E), r2),
            pl.BlockSpec((CONV_W - 1, nb, CONV_CH), m3),
            pl.BlockSpec((nb, H_A, DK_A, DV_A), i4),
            pl.BlockSpec((nb, H_B, DK_B, DV_B), i4),
            pl.BlockSpec((H_B, nb, DK_B), m3),
            pl.BlockSpec((nb, LANE), r2),
            pl.BlockSpec((8, LANE), c2),
            pl.BlockSpec((CONV_W, CONV_CH), c2),
            pl.BlockSpec((1, DV_A), c2),
            pl.BlockSpec((1, DV_B), c2),
        ],
        out_specs=[
            pl.BlockSpec((nb, W_MIX), r2),
            pl.BlockSpec((CONV_W - 1, nb, CONV_CH), m3),
            pl.BlockSpec((nb, H_A, DK_A, DV_A), i4),
            pl.BlockSpec((nb, H_B, DK_B, DV_B), i4),
            pl.BlockSpec((H_B, nb, DK_B), m3),
            pl.BlockSpec((nb, LANE), r2),
        ],
        out_shape=[
            jax.ShapeDtypeStruct((batch, W_MIX), F32),
            jax.ShapeDtypeStruct((CONV_W - 1, batch, CONV_CH), F32),
            jax.ShapeDtypeStruct((batch, H_A, DK_A, DV_A), F32),
            jax.ShapeDtypeStruct((batch, H_B, DK_B, DV_B), F32),
            jax.ShapeDtypeStruct((H_B, batch, DK_B), F32),
            jax.ShapeDtypeStruct((batch, LANE), F32),
        ],
        scratch_shapes=[pltpu.VMEM((nb, CONV_CH), F32)],
        compiler_params=pltpu.CompilerParams(
            dimension_semantics=("arbitrary",), vmem_limit_bytes=VMEM_LIMIT),
        name="even_sample",
    )(proj, small, conv, gdn, mc, mn, mm, sp, cw, gnw, mnw)


def _odd_sample_kernel(proj_ref, glr_ref, w2_ref, b2_ref, gla_ref, ret_ref, cc_ref, ss_ref,
                       cnw_ref, dnw_ref, y_ref, gla_o, ret_o):
    nb = DEC_BB
    cnw = cnw_ref[...]
    dnw = dnw_ref[...]
    cc = cc_ref[...]
    ss = ss_ref[...]
    rows = lax.broadcasted_iota(jnp.int32, (nb, DV_C), 0)
    g = _log_sigmoid(_dot(glr_ref[...].astype(BF16), w2_ref[...]) + b2_ref[...]) / GLA_TAU

    for h in range(H_C):
        q = proj_ref[:, O_QC + h * DK_C:O_QC + (h + 1) * DK_C].astype(F32) * (DK_C ** -0.5)
        k = proj_ref[:, O_KC + h * DK_C:O_KC + (h + 1) * DK_C]
        v = proj_ref[:, O_VC + h * DV_C:O_VC + (h + 1) * DV_C].astype(F32)
        eg = jnp.exp(g[:, h * DK_C:(h + 1) * DK_C])
        qe = _pad_rows16(q * eg)
        kt = _transpose_bf16(k)
        egt = _transpose(eg)
        qs = jnp.zeros((nb, DV_C), F32)
        for b in range(nb):
            qs = _row_select(rows, b, qs, _dot(qe, gla_ref[b, h].astype(BF16))[0:nb])
        o = qs + jnp.sum(q * k.astype(F32), axis=-1, keepdims=True) * v
        for b in range(nb):
            vsel = _pad_rows16(jnp.where(rows == b, v, 0.0))
            gla_o[b, h] = egt[:, b:b + 1] * gla_ref[b, h] + _dot(kt, vsel)
        z = proj_ref[:, O_Z + h * DV_C:O_Z + (h + 1) * DV_C].astype(F32)
        y_ref[:, h * DV_C:(h + 1) * DV_C] = _rms_rows(o, cnw) * _silu(z)

    for h in range(H_D):
        gamma = math.exp(_ret_log_gamma(h))
        qd = proj_ref[:, O_QD + h * DK_D:O_QD + (h + 1) * DK_D].astype(F32)
        kd = proj_ref[:, O_KD + h * DK_D:O_KD + (h + 1) * DK_D].astype(F32)
        qd = qd * cc + pltpu.roll(qd, DK_D // 2, 1) * ss
        kd = (kd * cc + pltpu.roll(kd, DK_D // 2, 1) * ss) * (DK_D ** -0.5)
        v = proj_ref[:, O_VD + h * DV_D:O_VD + (h + 1) * DV_D].astype(F32)
        qb = _pad_rows16(qd)
        kt = _transpose_bf16(kd)
        qs = jnp.zeros((nb, DV_D), F32)
        for b in range(nb):
            qs = _row_select(rows, b, qs, _dot(qb, ret_ref[b, h].astype(BF16))[0:nb])
        o = gamma * qs + jnp.sum(qd * kd, axis=-1, keepdims=True) * v
        for b in range(nb):
            vsel = _pad_rows16(jnp.where(rows == b, v, 0.0))
            ret_o[b, h] = gamma * ret_ref[b, h] + _dot(kt, vsel)
        z = proj_ref[:, O_Z + H_C * DV_C + h * DV_D:
                     O_Z + H_C * DV_C + (h + 1) * DV_D].astype(F32)
        y_ref[:, H_C * DV_C + h * DV_D:H_C * DV_C + (h + 1) * DV_D] = (
            _rms_rows(o, dnw) * _silu(z))


def _odd_sample(proj, glr, w2, b2, gla, ret, cc, ss, cnw, dnw):
    batch = proj.shape[0]
    nb = DEC_BB
    r2 = lambda i: (i, 0)
    i4 = lambda i: (i, 0, 0, 0)
    c2 = lambda i: (0, 0)
    return pl.pallas_call(
        _odd_sample_kernel,
        grid=(batch // nb,),
        in_specs=[
            pl.BlockSpec((nb, P_O), r2),
            pl.BlockSpec((nb, SMALL_O), r2),
            pl.BlockSpec((LANE, H_C * DK_C), c2),
            pl.BlockSpec((1, H_C * DK_C), c2),
            pl.BlockSpec((nb, H_C, DK_C, DV_C), i4),
            pl.BlockSpec((nb, H_D, DK_D, DV_D), i4),
            pl.BlockSpec((1, DK_D), c2),
            pl.BlockSpec((1, DK_D), c2),
            pl.BlockSpec((1, DV_C), c2),
            pl.BlockSpec((1, DV_D), c2),
        ],
        out_specs=[
            pl.BlockSpec((nb, W_MIX), r2),
            pl.BlockSpec((nb, H_C, DK_C, DV_C), i4),
            pl.BlockSpec((nb, H_D, DK_D, DV_D), i4),
        ],
        out_shape=[
            jax.ShapeDtypeStruct((batch, W_MIX), F32),
            jax.ShapeDtypeStruct((batch, H_C, DK_C, DV_C), F32),
            jax.ShapeDtypeStruct((batch, H_D, DK_D, DV_D), F32),
        ],
        compiler_params=pltpu.CompilerParams(
            dimension_semantics=("arbitrary",), vmem_limit_bytes=VMEM_LIMIT),
        name="odd_sample",
    )(proj, glr, w2, b2, gla, ret, cc, ss, cnw, dnw)


def _pad_cols(w, width):
    return jnp.pad(w, ((0, 0), (0, width - w.shape[1])))


def _prep_even_w_in(w):
    u = w[:, 0:3072]
    beta = w[:, 3072:3080]
    a = w[:, 3080:3088]
    qb = w[:, 3088:3600].reshape(-1, H_B, DK_B)
    kb = w[:, 3600:4112].reshape(-1, H_B, DK_B)
    vb = w[:, 4112:5136]
    ig = w[:, 5136:5144]
    fg = w[:, 5144:5152]
    op = w[:, 5152:6176]
    z = w[:, 6176:8224]
    qk = jnp.concatenate([qb, kb], axis=-1).reshape(-1, H_B * 2 * DK_B)
    small = jnp.concatenate([_pad_cols(beta, 128), _pad_cols(a, 128), _pad_cols(ig, 128),
                             _pad_cols(fg, 128)], axis=1)
    return jnp.concatenate([u, qk, vb, op, z], axis=1).astype(BF16), small.astype(BF16)


def _prep_odd_w_in(w):
    qc = w[:, 0:512]
    kc = w[:, 512:1024]
    vc = w[:, 1024:2048]
    glr = w[:, 2048:2064]
    qd = w[:, 2064:2576]
    kd = w[:, 2576:3088]
    vd = w[:, 3088:4112]
    z = w[:, 4112:6160]
    return (jnp.concatenate([qc, kc, vc, qd, kd, vd, z], axis=1).astype(BF16),
            _pad_cols(glr, SMALL_O).astype(BF16))


def _lane_row(vec, width=LANE):
    return jnp.pad(vec.astype(F32), (0, width - vec.shape[0]))


def _rotary_tables(pos):
    half = DK_D // 2
    inv = ROPE_BASE ** (-jnp.arange(half, dtype=F32) / half)
    ang = pos.astype(F32)[:, None] * inv[None, :]
    cos = jnp.cos(ang)
    sin = jnp.sin(ang)
    return jnp.concatenate([cos, cos], axis=-1), jnp.concatenate([-sin, sin], axis=-1)


def kernel(x_prompt, x_sample, c_prompt, c_sample, state_gdn, state_gdn_conv, state_mlstm_c,
           state_mlstm_n, state_mlstm_m, state_gla, state_ret, ada_w, ada_b, norm_w, ev_w_in,
           ev_w_out, gdn_conv_w, gdn_a_log, gdn_dt_bias, gdn_norm_w, mlstm_gate_b, mlstm_norm_w,
           od_w_in, od_w_out, gla_w2, gla_b2, gla_norm_w, ret_norm_w, final_norm_w):
    bp, tp, d = x_prompt.shape
    bs = x_sample.shape[0]
    n_p = bp * tp

    w_in_e, w_sm_e = _prep_even_w_in(ev_w_in[0])
    w_in_o, w_sm_o = _prep_odd_w_in(od_w_in[0])
    w_out_e = ev_w_out[0].astype(BF16)
    w_out_o = od_w_out[0].astype(BF16)
    sp = jnp.stack([_lane_row(gdn_dt_bias[0]), _lane_row(gdn_a_log[0]),
                    _lane_row(mlstm_gate_b[0, :H_B]), _lane_row(mlstm_gate_b[0, H_B:])]
                   + [jnp.zeros((LANE,), F32)] * 4)
    gnw = gdn_norm_w[0].reshape(1, DV_A)
    mnw = mlstm_norm_w[0].reshape(1, DV_B)
    cnw = gla_norm_w[0].reshape(1, DV_C)
    dnw = ret_norm_w[0].reshape(1, DV_D)
    w2 = jnp.pad(gla_w2[0], ((0, LANE - GLA_RANK), (0, 0))).astype(BF16)
    b2 = gla_b2[0].reshape(1, H_C * DK_C)

    mod = _modulation(jnp.concatenate([c_prompt, c_sample], axis=0), ada_w, ada_b)
    mod_p = [mod[l, :bp].reshape(bp, 1, 3 * d) for l in range(2)]
    mod_s = [mod[l, bp:].reshape(1, bs, 3 * d) for l in range(2)]

    xp = x_prompt.reshape(n_p, d)
    proj, small = _inproj(xp, mod_p[0], norm_w[0], w_in_e, w_sm_e, 1024, 2048)
    y, p_gdn, p_conv, p_mc, p_mn, p_mm = _even_prompt(
        proj.reshape(bp, tp, P_E), small.reshape(bp, tp, SMALL_E), sp, gdn_conv_w[0], gnw, mnw,
        bp, tp)
    xp = _outproj(y.reshape(n_p, W_MIX), xp, mod_p[0], w_out_e, final_norm_w, 512, False)
    proj, small = _inproj(xp, mod_p[1], norm_w[1], w_in_o, w_sm_o, 1024, 2048)
    cc_p, ss_p = _rotary_tables(jnp.arange(tp))
    y, p_gla, p_ret = _odd_prompt(proj, small, w2, b2, cc_p, ss_p, cnw, dnw, bp, tp)
    y_prompt = _outproj(y, xp, mod_p[1], w_out_o, final_norm_w, 512, True).reshape(bp, tp, d)

    xs = x_sample.reshape(bs, d)
    proj, small = _inproj(xs, mod_s[0], norm_w[0], w_in_e, w_sm_e, bs, 2048)
    mm_in = jnp.pad(state_mlstm_m[0], ((0, 0), (0, LANE - H_B)))
    y, s_conv, s_gdn, s_mc, s_mn, s_mm = _even_sample(
        proj, small, jnp.transpose(state_gdn_conv[0], (1, 0, 2)), state_gdn[0], state_mlstm_c[0],
        jnp.transpose(state_mlstm_n[0], (1, 0, 2)), mm_in, sp, gdn_conv_w[0], gnw, mnw)
    xs = _outproj(y, xs, mod_s[0], w_out_e, final_norm_w, bs, False)
    proj, small = _inproj(xs, mod_s[1], norm_w[1], w_in_o, w_sm_o, bs, 2048)
    cc_s, ss_s = _rotary_tables(PAST_LEN + jnp.arange(1))
    y, s_gla, s_ret = _odd_sample(proj, small, w2, b2, state_gla[0], state_ret[0], cc_s, ss_s,
                                  cnw, dnw)
    y_sample = _outproj(y, xs, mod_s[1], w_out_o, final_norm_w, bs, True)
    y_sample = y_sample.reshape(bs, 1, d)

    return (y_prompt, y_sample,
            p_gdn[None], p_conv[None], p_mc[None], p_mn[None], p_mm[None, :, 0, :H_B],
            p_gla[None], p_ret[None],
            s_gdn[None], jnp.transpose(s_conv, (1, 0, 2))[None], s_mc[None],
            jnp.transpose(s_mn, (1, 0, 2))[None], s_mm[None, :, :H_B], s_gla[None], s_ret[None])
```

```python
import functools
import math

import jax
import jax.numpy as jnp
import numpy as np
from jax import lax
from jax.experimental import pallas as pl
from jax.experimental.pallas import tpu as pltpu

F32 = jnp.float32
BF16 = jnp.bfloat16
HI = lax.Precision.HIGHEST

D_MODEL = 1024
W_MIX = 2048
H_A, DK_A, DV_A, CONV_W = 8, 128, 128, 4
CONV_CH = H_A * (2 * DK_A + DV_A)
H_B, DK_B, DV_B = 8, 64, 128
H_C, DK_C, DV_C, GLA_RANK, GLA_TAU = 4, 128, 256, 16, 16.0
H_D, DK_D, DV_D = 4, 128, 256
ROPE_BASE = 10000.0
PAST_LEN = 16384
EPS = 1e-6
LANE = 128

CHUNK = 64
CHUNKS_PER_STEP = 4

E_U, E_QK, E_V, E_OP, E_Z = 0, 3072, 4096, 5120, 6144
P_E = 8192
SMALL_E = 512
O_QC, O_KC, O_VC, O_QD, O_KD, O_VD, O_Z = 0, 512, 1024, 2048, 2560, 3072, 4096
P_O = 6144
SMALL_O = 128

VMEM_LIMIT = 56 * 1024 * 1024


def _sigmoid(x):
    return 0.5 * jnp.tanh(0.5 * x) + 0.5


def _softplus(x):
    return jnp.maximum(x, 0.0) + jnp.log(1.0 + jnp.exp(-jnp.abs(x)))


def _log_sigmoid(x):
    return -_softplus(-x)


def _silu(x):
    half = 0.5 * x
    return half + half * jnp.tanh(half)


def _rms_rows(x, w):
    return x * lax.rsqrt(jnp.mean(x * x, axis=-1, keepdims=True) + EPS) * w


def _l2n(x):
    return x * lax.rsqrt(jnp.sum(x * x, axis=-1, keepdims=True) + EPS)


def _dot(a, b, precision=None):
    return jnp.dot(a, b, preferred_element_type=F32, precision=precision)


def _dot_nt(a, b, precision=None):
    return lax.dot_general(a, b, (((1,), (1,)), ((), ())), preferred_element_type=F32,
                           precision=precision)


def _dot_tn(a, b, precision=None):
    return lax.dot_general(a, b, (((0,), (0,)), ((), ())), preferred_element_type=F32,
                           precision=precision)


def _eye(n):
    r = lax.broadcasted_iota(jnp.int32, (n, n), 0)
    c = lax.broadcasted_iota(jnp.int32, (n, n), 1)
    return (r == c).astype(F32)


def _bf16_terms(x, terms):
    parts = []
    rest = x
    for _ in range(terms):
        piece = rest.astype(BF16)
        parts.append(piece)
        rest = rest - piece.astype(F32)
    return parts


def _select_dot(sel, x, terms):
    n = x.shape[1]
    out = _dot(sel.astype(BF16), jnp.concatenate(_bf16_terms(x, terms), axis=1))
    acc = out[:, 0:n]
    for i in range(1, terms):
        acc = acc + out[:, i * n:(i + 1) * n]
    return acc


def _transpose(x):
    return _dot_nt(_eye(x.shape[1]), x, precision=HI)


def _tri_masks(n):
    r = lax.broadcasted_iota(jnp.int32, (n, n), 0)
    c = lax.broadcasted_iota(jnp.int32, (n, n), 1)
    return r >= c, r > c


def _mod_kernel(c_ref, w_ref, b_ref, o_ref):
    cs = _silu(c_ref[...]).astype(BF16)
    o_ref[...] = _dot(cs, w_ref[...].astype(BF16)) + b_ref[...]


def _modulation(c_all, ada_w, ada_b):
    depth, d, d3 = ada_w.shape
    rows = c_all.shape[0]
    tn = 1024
    return pl.pallas_call(
        _mod_kernel,
        grid=(depth, d3 // tn),
        in_specs=[
            pl.BlockSpec((rows, d), lambda l, j: (0, 0)),
            pl.BlockSpec((None, d, tn), lambda l, j: (l, 0, j)),
            pl.BlockSpec((None, 1, tn), lambda l, j: (l, 0, j)),
        ],
        out_specs=pl.BlockSpec((None, rows, tn), lambda l, j: (l, 0, j)),
        out_shape=jax.ShapeDtypeStruct((depth, rows, d3), F32),
        compiler_params=pltpu.CompilerParams(
            dimension_semantics=("arbitrary", "arbitrary"), vmem_limit_bytes=VMEM_LIMIT),
        name="modulation",
    )(c_all, ada_w, ada_b.reshape(depth, 1, d3))


def _inproj_kernel(x_ref, mod_ref, nw_ref, w_ref, ws_ref, o_ref, os_ref, h_ref):
    @pl.when(pl.program_id(1) == 0)
    def _():
        x = x_ref[...]
        y = _rms_rows(x, nw_ref[...])
        shift = mod_ref[:, 0:D_MODEL]
        scale = mod_ref[:, D_MODEL:2 * D_MODEL]
        h_ref[...] = (y * (1.0 + scale) + shift).astype(BF16)
        os_ref[...] = _dot(h_ref[...], ws_ref[...])

    o_ref[...] = _dot(h_ref[...], w_ref[...]).astype(o_ref.dtype)


def _inproj(x, mod, norm_w, w, w_small, tm, tn):
    n, d = x.shape
    p = w.shape[1]
    ps = w_small.shape[1]
    g, r, _ = mod.shape
    tiles_per_group = (n // g) // tm
    return pl.pallas_call(
        _inproj_kernel,
        grid=(n // tm, p // tn),
        in_specs=[
            pl.BlockSpec((tm, d), lambda i, j: (i, 0)),
            pl.BlockSpec((None, r, 3 * d), lambda i, j: (i // tiles_per_group, 0, 0)),
            pl.BlockSpec((1, d), lambda i, j: (0, 0)),
            pl.BlockSpec((d, tn), lambda i, j: (0, j)),
            pl.BlockSpec((d, ps), lambda i, j: (0, 0)),
        ],
        out_specs=[pl.BlockSpec((tm, tn), lambda i, j: (i, j)),
                   pl.BlockSpec((tm, ps), lambda i, j: (i, 0))],
        out_shape=[jax.ShapeDtypeStruct((n, p), BF16), jax.ShapeDtypeStruct((n, ps), F32)],
        scratch_shapes=[pltpu.VMEM((tm, d), BF16)],
        compiler_params=pltpu.CompilerParams(
            dimension_semantics=("arbitrary", "arbitrary"), vmem_limit_bytes=VMEM_LIMIT),
        name="inproj",
    )(x, mod, norm_w.reshape(1, d), w, w_small)


def _outproj_kernel(y_ref, x_ref, mod_ref, w_ref, fw_ref, o_ref, *, final):
    acc = _dot(y_ref[...].astype(BF16), w_ref[...])
    gate = mod_ref[:, 2 * D_MODEL:3 * D_MODEL]
    xn = x_ref[...] + gate * acc
    if final:
        xn = _rms_rows(xn, fw_ref[...])
    o_ref[...] = xn


def _outproj(y, x, mod, w, final_w, tm, final):
    n, d = x.shape
    k = y.shape[1]
    g, r, _ = mod.shape
    tiles_per_group = (n // g) // tm
    return pl.pallas_call(
        functools.partial(_outproj_kernel, final=final),
        grid=(n // tm,),
        in_specs=[
            pl.BlockSpec((tm, k), lambda i: (i, 0)),
            pl.BlockSpec((tm, d), lambda i: (i, 0)),
            pl.BlockSpec((None, r, 3 * d), lambda i: (i // tiles_per_group, 0, 0)),
            pl.BlockSpec((k, d), lambda i: (0, 0)),
            pl.BlockSpec((1, d), lambda i: (0, 0)),
        ],
        out_specs=pl.BlockSpec((tm, d), lambda i: (i, 0)),
        out_shape=jax.ShapeDtypeStruct((n, d), F32),
        compiler_params=pltpu.CompilerParams(
            dimension_semantics=("arbitrary",), vmem_limit_bytes=VMEM_LIMIT),
        name="outproj",
    )(y, x, mod, w, final_w.reshape(1, d))


def _neumann_correction(a_list, n):
    ys = [-a for a in a_list]
    pbs = [a.astype(BF16) for a in a_list]
    for _ in range(int(math.log2(n)) - 1):
        for i in range(len(ys)):
            p = _dot(pbs[i], pbs[i])
            pbs[i] = p.astype(BF16)
            ys[i] = ys[i] + p + _dot(ys[i].astype(BF16), pbs[i])
        yield
    idx = range(len(ys))
    a_h = [a.astype(BF16) for a in a_list]
    a_l = [(a_list[i] - a_h[i].astype(F32)).astype(BF16) for i in idx]
    y_h = [y.astype(BF16) for y in ys]
    y_l = [(ys[i] - y_h[i].astype(F32)).astype(BF16) for i in idx]
    ay = [_dot(a_h[i], y_h[i]) + _dot(a_h[i], y_l[i]) + _dot(a_l[i], y_h[i]) for i in idx]
    e = [-(a_list[i] + ys[i] + ay[i]) for i in idx]
    return [ys[i] + e[i] + _dot(y_h[i], e[i].astype(BF16)) for i in idx]


def _run_staggered(stage_generators, offset):
    pending = list(stage_generators)
    active = []
    tick = 0
    while pending or active:
        if pending and tick % offset == 0:
            active.append(pending.pop(0))
        for g in list(active):
            try:
                next(g)
            except StopIteration:
                active.remove(g)
        tick += 1


EVEN_STAGGER = 14
ODD_STAGGER = 9


def _conv_shift_matrix():
    n = CHUNK
    r = np.arange((CONV_W - 1) * n)[:, None]
    col = np.arange(2 * n)[None, :]
    return jnp.asarray(col == n + (r % n) - (r // n + 1), BF16)


def _even_prompt_kernel(proj_ref, sm_ref, sp_ref, cw_ref, shift_ref, gnw_ref, mnw_ref,
                        y_ref, gdn_ref, conv_ref, mc_ref, mn_ref, mm_ref,
                        uprev_ref, qkv_ref):
    chunks = []
    for j in range(CHUNKS_PER_STEP):
        rows = pl.ds(j * CHUNK, CHUNK)
        chunks.append(_even_prompt_sequence(
            j == 0, proj_ref.at[0, rows], sm_ref.at[0, rows], sp_ref, cw_ref, shift_ref, gnw_ref,
            mnw_ref, y_ref.at[0, rows], gdn_ref.at[0], conv_ref.at[0], mc_ref.at[0], mn_ref.at[0],
            mm_ref.at[0], uprev_ref.at[0], qkv_ref.at[0]))
    _run_staggered(chunks, EVEN_STAGGER)


def _even_prompt_sequence(first_in_step, proj_ref, sm_ref, sp_ref, cw_ref, shift_ref, gnw_ref,
                          mnw_ref, y_ref, gdn_ref, conv_ref, mc_ref, mn_ref, mm_ref,
                          uprev_ref, qkv_ref):
    L = CHUNK

    if first_in_step:
        @pl.when(pl.program_id(1) == 0)
        def _():
            gdn_ref[...] = jnp.zeros_like(gdn_ref)
            mc_ref[...] = jnp.zeros_like(mc_ref)
            mn_ref[...] = jnp.zeros_like(mn_ref)
            mm_ref[...] = jnp.zeros_like(mm_ref)
            uprev_ref[...] = jnp.zeros_like(uprev_ref)

    tri_incl, tri_strict = _tri_masks(L)
    tri_f = tri_incl.astype(F32)

    ub = proj_ref[:, E_U:E_U + CONV_CH]
    shifted = _dot(shift_ref[...], jnp.concatenate([uprev_ref[...], ub], axis=0))
    uprev_ref[...] = ub
    u = ub.astype(F32)
    cw = cw_ref[...]
    uc = (cw[0:1] * shifted[2 * L:3 * L] + cw[1:2] * shifted[L:2 * L]
          + cw[2:3] * shifted[0:L] + cw[3:4] * u)
    conv_ref[...] = u[L - 3:L, :]
    qkv_ref[...] = _silu(uc)
    yield

    sm = sm_ref[...]
    head_lane = lax.broadcasted_iota(jnp.int32, (L, LANE), 1) < H_A
    beta = _sigmoid(sm[:, 0:128])
    g = -jnp.exp(sp_ref[1:2, :]) * _softplus(sm[:, 128:256] + sp_ref[0:1, :])
    ig = sm[:, 256:384] + sp_ref[2:3, :]
    lf = _log_sigmoid(sm[:, 384:512] + sp_ref[3:4, :])
    packed = jnp.where(head_lane, g, 0.0) + pltpu.roll(jnp.where(head_lane, lf, 0.0), H_A, 1)
    cum = _dot(tri_f, packed, precision=HI)
    cum_t = _transpose(cum + pltpu.roll(jnp.where(head_lane, ig, 0.0), H_A + H_B, 1))

    gnw = gnw_ref[...]
    mnw = mnw_ref[...]
    yield

    qkb, kf, decay, beta_c, gam, bh_c = [], [], [], [], [], []
    for h in range(H_A):
        q = _l2n(qkv_ref[:, h * DK_A:(h + 1) * DK_A]) * (DK_A ** -0.5)
        k = _l2n(qkv_ref[:, H_A * DK_A + h * DK_A:H_A * DK_A + (h + 1) * DK_A])
        kf.append(k)
        qkb.append(jnp.concatenate([q, k], axis=0).astype(BF16))
    yield
    qkk = [_dot_nt(qkb[h], qkb[h][L:2 * L]) for h in range(H_A)]
    yield
    a_list = []
    for h in range(H_A):
        bh_c.append(cum[:, h:h + 1])
        diff = bh_c[h] - cum_t[h:h + 1, :]
        decay.append(jnp.where(tri_incl, jnp.exp(jnp.where(tri_incl, diff, 0.0)), 0.0))
        beta_c.append(beta[:, h:h + 1])
        gam.append(jnp.exp(bh_c[h]))
        a_list.append(jnp.where(tri_strict, decay[h] * qkk[h][L:2 * L], 0.0) * beta_c[h])
    yield
    corr = yield from _neumann_correction(a_list, L)
    yield

    heads = range(H_A)
    s_old = [gdn_ref[h] for h in heads]
    qks = [_dot(qkb[h], s_old[h].astype(BF16)) for h in heads]
    yield
    rhs = [beta_c[h] * (qkv_ref[:, 2 * H_A * DK_A + h * DV_A:2 * H_A * DK_A + (h + 1) * DV_A]
                        - gam[h] * qks[h][L:2 * L]) for h in heads]
    ub = [(rhs[h] + _dot(corr[h].astype(BF16), rhs[h].astype(BF16))).astype(BF16) for h in heads]
    yield
    o = [gam[h] * qks[h][0:L] + _dot((qkk[h][0:L] * decay[h]).astype(BF16), ub[h])
         for h in heads]
    yield
    for h in heads:
        last = bh_c[h][L - 1:L, :]
        kw = (kf[h] * jnp.exp(last - bh_c[h])).astype(BF16)
        gdn_ref[h] = jnp.exp(last) * s_old[h] + _dot_tn(kw, ub[h])
    yield
    for h in heads:
        z = proj_ref[:, E_Z + h * DV_A:E_Z + (h + 1) * DV_A].astype(F32)
        y_ref[:, h * DV_A:(h + 1) * DV_A] = (_rms_rows(o[h], gnw) * _silu(z)).astype(BF16)
    yield

    heads = range(H_B)
    qs, qbs, kbs, vbs = [], [], [], []
    for h in heads:
        qk_in = proj_ref[:, E_QK + h * 128:E_QK + (h + 1) * 128]
        qbs.append(qk_in[:, 0:DK_B])
        qs.append(qbs[h].astype(F32))
        kbs.append((qk_in[:, DK_B:2 * DK_B].astype(F32) * (DK_B ** -0.5)).astype(BF16))
        vbs.append(proj_ref[:, E_V + h * DV_B:E_V + (h + 1) * DV_B])
    qkm = [_dot_nt(qbs[h], kbs[h]) for h in heads]
    yield
    b_c = [cum[:, H_A + h:H_A + h + 1] for h in heads]
    m0 = [mm_ref[0:1, h:h + 1] for h in heads]
    dmat = [jnp.where(tri_incl, b_c[h] - cum_t[H_A + h:H_A + h + 1, :]
                      + cum_t[H_A + H_B + h:H_A + H_B + h + 1, :], -jnp.inf) for h in heads]
    w0 = [b_c[h] + m0[h] for h in heads]
    m_t = [jnp.maximum(w0[h], jnp.max(dmat[h], axis=-1, keepdims=True)) for h in heads]
    p = [jnp.exp(dmat[h] - m_t[h]) * qkm[h] for h in heads]
    s0 = [jnp.exp(w0[h] - m_t[h]) for h in heads]
    yield
    c_old = [mc_ref[h] for h in heads]
    n_old = [mn_ref[h:h + 1, :] for h in heads]
    num = [s0[h] * _dot(qbs[h], c_old[h].astype(BF16)) + _dot(p[h].astype(BF16), vbs[h])
           for h in heads]
    den = [s0[h] * jnp.sum(qs[h] * n_old[h], axis=-1, keepdims=True)
           + jnp.sum(p[h], axis=-1, keepdims=True) for h in heads]
    hh = [num[h] / jnp.maximum(jnp.abs(den[h]), jnp.exp(-m_t[h])) for h in heads]
    yield
    for h in heads:
        m_end = m_t[h][L - 1:L, :]
        b_last = b_c[h][L - 1:L, :]
        we = jnp.exp(b_last - b_c[h] + ig[:, h:h + 1] - m_end)
        se = jnp.exp(b_last + m0[h] - m_end)
        kw = kbs[h].astype(F32) * we
        mc_ref[h] = se * c_old[h] + _dot_tn(kw.astype(BF16), vbs[h])
        mn_ref[h:h + 1, :] = se * n_old[h] + jnp.sum(kw, axis=0, keepdims=True)
        mm_ref[0:1, h:h + 1] = m_end
    yield
    for h in heads:
        o_pre = proj_ref[:, E_OP + h * DV_B:E_OP + (h + 1) * DV_B].astype(F32)
        z = proj_ref[:, E_Z + H_A * DV_A + h * DV_B:
                     E_Z + H_A * DV_A + (h + 1) * DV_B].astype(F32)
        y_ref[:, H_A * DV_A + h * DV_B:H_A * DV_A + (h + 1) * DV_B] = (
            _sigmoid(o_pre) * _rms_rows(hh[h], mnw) * _silu(z)).astype(BF16)


def _even_prompt(proj, small, sp, conv_w, gnw, mnw, batch, seq):
    step = CHUNK * CHUNKS_PER_STEP
    nc = seq // step
    nb = 1
    row = lambda b, c: (b, c, 0)
    const2 = lambda b, c: (0, 0)
    st3 = lambda b, c: (b, 0, 0)
    st4 = lambda b, c: (b, 0, 0, 0)
    return pl.pallas_call(
        _even_prompt_kernel,
        grid=(batch // nb, nc),
        in_specs=[
            pl.BlockSpec((nb, step, P_E), row),
            pl.BlockSpec((nb, step, SMALL_E), row),
            pl.BlockSpec((8, LANE), const2),
            pl.BlockSpec((CONV_W, CONV_CH), const2),
            pl.BlockSpec(((CONV_W - 1) * CHUNK, 2 * CHUNK), const2),
            pl.BlockSpec((1, DV_A), const2),
            pl.BlockSpec((1, DV_B), const2),
        ],
        out_specs=[
            pl.BlockSpec((nb, step, W_MIX), row),
            pl.BlockSpec((nb, H_A, DK_A, DV_A), st4),
            pl.BlockSpec((nb, CONV_W - 1, CONV_CH), st3),
            pl.BlockSpec((nb, H_B, DK_B, DV_B), st4),
            pl.BlockSpec((nb, H_B, DK_B), st3),
            pl.BlockSpec((nb, 1, LANE), st3),
        ],
        out_shape=[
            jax.ShapeDtypeStruct((batch, seq, W_MIX), BF16),
            jax.ShapeDtypeStruct((batch, H_A, DK_A, DV_A), F32),
            jax.ShapeDtypeStruct((batch, CONV_W - 1, CONV_CH), F32),
            jax.ShapeDtypeStruct((batch, H_B, DK_B, DV_B), F32),
            jax.ShapeDtypeStruct((batch, H_B, DK_B), F32),
            jax.ShapeDtypeStruct((batch, 1, LANE), F32),
        ],
        scratch_shapes=[pltpu.VMEM((nb, CHUNK, CONV_CH), BF16),
                        pltpu.VMEM((nb, CHUNK, CONV_CH), F32)],
        compiler_params=pltpu.CompilerParams(
            dimension_semantics=("arbitrary", "arbitrary"), vmem_limit_bytes=VMEM_LIMIT),
        name="even_prompt",
    )(proj, small, sp, conv_w, _conv_shift_matrix(), gnw, mnw)


def _ret_log_gamma(h):
    return math.log(1.0 - 2.0 ** (-5.0 - h))


GLA_LEVELS = int(math.log2(CHUNK))
GLA_MM_LEVELS = (1, 2)
LOG2E = math.log2(math.e)


def _gla_chunk_tables():
    n = CHUNK
    t = np.arange(n)[:, None]
    i = np.arange(n)[None, :]
    blocks = [(i <= t)]
    for lvl in GLA_MM_LEVELS:
        hs = 1 << lvl
        p = (t & ~(2 * hs - 1)) + hs - 1
        upper = (t & hs) != 0
        blocks.append(np.where(upper, (i > p) & (i <= t), (i > t) & (i <= p)))
    sums = np.concatenate(blocks, axis=0).astype(np.float32)
    s = np.arange(n)[None, :]
    level = np.full((n, n), GLA_LEVELS + 1, np.int32)
    level[np.arange(n), np.arange(n)] = GLA_LEVELS
    for lvl in range(GLA_LEVELS):
        hs = 1 << lvl
        same = (t >> (lvl + 1)) == (s >> (lvl + 1))
        level[same & ((t & hs) != 0) & ((s & hs) == 0)] = lvl
    return jnp.asarray(sums), jnp.asarray(level)


def _odd_prompt_kernel(proj_ref, glr_ref, w2_ref, b2_ref, sums_ref, level_ref, cc_ref, ss_ref,
                       cnw_ref, dnw_ref, y_ref, gla_ref, ret_ref, x_ref):
    chunks = []
    for j in range(CHUNKS_PER_STEP):
        rows = pl.ds(j * CHUNK, CHUNK)
        chunks.append(_odd_prompt_chunk(
            j == 0, proj_ref.at[rows], glr_ref.at[rows], w2_ref, b2_ref, sums_ref, level_ref,
            cc_ref.at[rows], ss_ref.at[rows], cnw_ref, dnw_ref, y_ref.at[rows], gla_ref, ret_ref,
            x_ref))
    _run_staggered(chunks, ODD_STAGGER)


def _odd_prompt_chunk(first_in_step, proj_ref, glr_ref, w2_ref, b2_ref, sums_ref, level_ref,
                      cc_ref, ss_ref, cnw_ref, dnw_ref, y_ref, gla_ref, ret_ref, x_ref):
    L = CHUNK

    if first_in_step:
        @pl.when(pl.program_id(1) == 0)
        def _():
            gla_ref[...] = jnp.zeros_like(gla_ref)
            ret_ref[...] = jnp.zeros_like(ret_ref)

    tri_incl, _ = _tri_masks(L)
    r64 = lax.broadcasted_iota(jnp.int32, (L, L), 0)
    c64 = lax.broadcasted_iota(jnp.int32, (L, L), 1)
    level = level_ref[...]

    glr = glr_ref[...].astype(BF16)
    gc2 = _log_sigmoid(_dot(glr, w2_ref[...]) + b2_ref[...]) * (LOG2E / GLA_TAU)
    x_ref[...] = _select_dot(sums_ref[...], gc2, 3)
    odd_row = (lax.broadcasted_iota(jnp.int32, (L, DK_C), 0) & 1) == 1
    yield

    def level_exponent(lvl, h):
        cols = slice(h * DK_C, (h + 1) * DK_C)
        if lvl == 0:
            return jnp.where(odd_row, gc2[:, cols], 0.0)
        if lvl in GLA_MM_LEVELS:
            r0 = (1 + GLA_MM_LEVELS.index(lvl)) * L
            return x_ref[r0:r0 + L, cols]
        hs = 1 << lvl
        pieces = []
        for lo in range(0, L, 2 * hs):
            ref = x_ref[lo + hs - 1:lo + hs, cols]
            pieces.append(ref - x_ref[lo:lo + hs, cols])
            pieces.append(x_ref[lo + hs:lo + 2 * hs, cols] - ref)
        return jnp.concatenate(pieces, axis=0)

    cnw = cnw_ref[...]
    dnw = dnw_ref[...]

    heads = range(H_C)
    q = [proj_ref[:, O_QC + h * DK_C:O_QC + (h + 1) * DK_C].astype(F32) * (DK_C ** -0.5)
         for h in heads]
    k = [proj_ref[:, O_KC + h * DK_C:O_KC + (h + 1) * DK_C].astype(F32) for h in heads]
    v = [proj_ref[:, O_VC + h * DV_C:O_VC + (h + 1) * DV_C] for h in heads]
    att = [jnp.where(level == GLA_LEVELS, _dot_nt(q[h].astype(BF16), k[h].astype(BF16)), 0.0)
           for h in heads]
    yield
    for lvl in range(GLA_LEVELS):
        for h in heads:
            e = jnp.exp2(level_exponent(lvl, h))
            m = _dot_nt((q[h] * e).astype(BF16), (k[h] * e).astype(BF16))
            att[h] = jnp.where(level == lvl, m, att[h])
        yield
    b2 = [x_ref[0:L, h * DK_C:(h + 1) * DK_C] for h in heads]
    s_gla = [gla_ref[h] for h in heads]
    o = [_dot((q[h] * jnp.exp2(b2[h])).astype(BF16), s_gla[h].astype(BF16))
         + _dot(att[h].astype(BF16), v[h]) for h in heads]
    yield
    for h in heads:
        bl = b2[h][L - 1:L, :]
        kw = (k[h] * jnp.exp2(bl - b2[h])).astype(BF16)
        col = jnp.transpose(jnp.broadcast_to(jnp.exp2(bl), (DK_C, DK_C)))
        upd = _dot_tn(kw, v[h])
        for half in range(DV_C // DK_C):
            lanes = slice(half * DK_C, (half + 1) * DK_C)
            gla_ref[h, :, lanes] = s_gla[h][:, lanes] * col + upd[:, lanes]
    yield
    for h in heads:
        z = proj_ref[:, O_Z + h * DV_C:O_Z + (h + 1) * DV_C].astype(F32)
        y_ref[:, h * DV_C:(h + 1) * DV_C] = (_rms_rows(o[h], cnw) * _silu(z)).astype(BF16)
    yield

    cc = cc_ref[...]
    ss = ss_ref[...]
    rel = (r64 - c64).astype(F32)
    tcol = lax.broadcasted_iota(jnp.int32, (L, 1), 0).astype(F32)
    heads = range(H_D)
    lg = [_ret_log_gamma(h) for h in heads]
    qb, kf, kb, vd = [], [], [], []
    for h in heads:
        qd = proj_ref[:, O_QD + h * DK_D:O_QD + (h + 1) * DK_D].astype(F32)
        kd = proj_ref[:, O_KD + h * DK_D:O_KD + (h + 1) * DK_D].astype(F32)
        qb.append((qd * cc + pltpu.roll(qd, DK_D // 2, 1) * ss).astype(BF16))
        kf.append((kd * cc + pltpu.roll(kd, DK_D // 2, 1) * ss) * (DK_D ** -0.5))
        kb.append(kf[h].astype(BF16))
        vd.append(proj_ref[:, O_VD + h * DV_D:O_VD + (h + 1) * DV_D])
    yield
    s_old = [ret_ref[h] for h in heads]
    attd = [_dot_nt(qb[h], kb[h])
            * jnp.where(tri_incl, jnp.exp(lg[h] * jnp.maximum(rel, 0.0)), 0.0) for h in heads]
    od = [_dot(qb[h], s_old[h].astype(BF16)) * jnp.exp(lg[h] * (tcol + 1.0))
          + _dot(attd[h].astype(BF16), vd[h]) for h in heads]
    yield
    for h in heads:
        end = jnp.exp(lg[h] * (L - 1.0 - tcol))
        ret_ref[h] = math.exp(lg[h] * L) * s_old[h] + _dot_tn((kf[h] * end).astype(BF16), vd[h])
    yield
    for h in heads:
        z = proj_ref[:, O_Z + H_C * DV_C + h * DV_D:
                     O_Z + H_C * DV_C + (h + 1) * DV_D].astype(F32)
        y_ref[:, H_C * DV_C + h * DV_D:H_C * DV_C + (h + 1) * DV_D] = (
            _rms_rows(od[h], dnw) * _silu(z)).astype(BF16)


def _odd_prompt(proj, glr, w2, b2, cc, ss, cnw, dnw, batch, seq):
    step = CHUNK * CHUNKS_PER_STEP
    nc = seq // step
    row = lambda b, c: (b * nc + c, 0)
    const2 = lambda b, c: (0, 0)
    sums, level = _gla_chunk_tables()
    n_sums = sums.shape[0]
    return pl.pallas_call(
        _odd_prompt_kernel,
        grid=(batch, nc),
        in_specs=[
            pl.BlockSpec((step, P_O), row),
            pl.BlockSpec((step, SMALL_O), row),
            pl.BlockSpec((LANE, H_C * DK_C), const2),
            pl.BlockSpec((1, H_C * DK_C), const2),
            pl.BlockSpec((n_sums, CHUNK), const2),
            pl.BlockSpec((CHUNK, CHUNK), const2),
            pl.BlockSpec((step, DK_D), lambda b, c: (c, 0)),
            pl.BlockSpec((step, DK_D), lambda b, c: (c, 0)),
            pl.BlockSpec((1, DV_C), const2),
            pl.BlockSpec((1, DV_D), const2),
        ],
        out_specs=[
            pl.BlockSpec((step, W_MIX), row),
            pl.BlockSpec((None, H_C, DK_C, DV_C), lambda b, c: (b, 0, 0, 0)),
            pl.BlockSpec((None, H_D, DK_D, DV_D), lambda b, c: (b, 0, 0, 0)),
        ],
        out_shape=[
            jax.ShapeDtypeStruct((batch * seq, W_MIX), BF16),
            jax.ShapeDtypeStruct((batch, H_C, DK_C, DV_C), F32),
            jax.ShapeDtypeStruct((batch, H_D, DK_D, DV_D), F32),
        ],
        scratch_shapes=[pltpu.VMEM((n_sums, H_C * DK_C), F32)],
        compiler_params=pltpu.CompilerParams(
            dimension_semantics=("arbitrary", "arbitrary"), vmem_limit_bytes=VMEM_LIMIT),
        name="odd_prompt",
    )(proj, glr, w2, b2, sums, level, cc, ss, cnw, dnw)


DEC_BB = 8


def _row_select(rows, row_index, acc, new):
    return jnp.where(rows == row_index, new, acc)


def _pad_rows16(x):
    return jnp.concatenate([x, jnp.zeros_like(x)], axis=0).astype(BF16)


def _transpose_bf16(x):
    eye = _eye(x.shape[1]).astype(BF16)
    return _dot_nt(eye, _pad_rows16(x)).astype(BF16)


def _even_sample_kernel(proj_ref, sm_ref, conv_ref, gdn_ref, mc_ref, mn_ref, mm_ref,
                        sp_ref, cw_ref, gnw_ref, mnw_ref,
                        y_ref, conv_o, gdn_o, mc_o, mn_o, mm_o, act_ref):
    nb = DEC_BB
    gnw = gnw_ref[...]
    mnw = mnw_ref[...]
    rows = lax.broadcasted_iota(jnp.int32, (nb, LANE), 0)

    sm = sm_ref[...]
    beta = _sigmoid(sm[:, 0:128])
    gam = jnp.exp(-jnp.exp(sp_ref[1:2, :]) * _softplus(sm[:, 128:256] + sp_ref[0:1, :]))
    ig = sm[:, 256:384] + sp_ref[2:3, :]
    lf = _log_sigmoid(sm[:, 384:512] + sp_ref[3:4, :])
    w0 = lf + mm_ref[...]
    m_t = jnp.maximum(w0, ig)
    s0 = jnp.exp(w0 - m_t)
    we = jnp.exp(ig - m_t)
    floor = jnp.exp(-m_t)
    mm_o[...] = m_t

    u = proj_ref[:, E_U:E_U + CONV_CH].astype(F32)
    cw = cw_ref[...]
    uc = (cw[0:1] * conv_ref[0] + cw[1:2] * conv_ref[1] + cw[2:3] * conv_ref[2] + cw[3:4] * u)
    conv_o[0] = conv_ref[1]
    conv_o[1] = conv_ref[2]
    conv_o[2] = u
    act_ref[...] = _silu(uc)

    for h in range(H_A):
        q = _l2n(act_ref[:, h * DK_A:(h + 1) * DK_A]) * (DK_A ** -0.5)
        k = _l2n(act_ref[:, H_A * DK_A + h * DK_A:H_A * DK_A + (h + 1) * DK_A])
        v = act_ref[:, 2 * H_A * DK_A + h * DV_A:2 * H_A * DK_A + (h + 1) * DV_A]
        kq = jnp.concatenate([k, q], axis=0).astype(BF16)
        kt = _transpose_bf16(k)
        ks = jnp.zeros((nb, DV_A), F32)
        qs = jnp.zeros((nb, DV_A), F32)
        for b in range(nb):
            r = _dot(kq, gdn_ref[b, h].astype(BF16))
            ks = _row_select(rows, b, ks, r[0:nb])
            qs = _row_select(rows, b, qs, r[nb:2 * nb])
        gam_h = gam[:, h:h + 1]
        uu = beta[:, h:h + 1] * (v - gam_h * ks)
        o = gam_h * qs + jnp.sum(q * k, axis=-1, keepdims=True) * uu
        for b in range(nb):
            usel = _pad_rows16(jnp.where(rows == b, uu, 0.0))
            gdn_o[b, h] = gam[b:b + 1, h:h + 1] * gdn_ref[b, h] + _dot(kt, usel)
        z = proj_ref[:, E_Z + h * DV_A:E_Z + (h + 1) * DV_A].astype(F32)
        y_ref[:, h * DV_A:(h + 1) * DV_A] = _rms_rows(o, gnw) * _silu(z)

    for h in range(H_B):
        qk_in = proj_ref[:, E_QK + h * 128:E_QK + (h + 1) * 128]
        qb = qk_in[:, 0:DK_B]
        q = qb.astype(F32)
        k = qk_in[:, DK_B:2 * DK_B].astype(F32) * (DK_B ** -0.5)
        v = proj_ref[:, E_V + h * DV_B:E_V + (h + 1) * DV_B].astype(F32)
        kt = _transpose_bf16(k)
        q16 = _pad_rows16(qb)
        qc = jnp.zeros((nb, DV_B), F32)
        for b in range(nb):
            r = _dot(q16, mc_ref[b, h].astype(BF16))
            qc = _row_select(rows, b, qc, r[0:nb])
        s0_h = s0[:, h:h + 1]
        we_h = we[:, h:h + 1]
        wv = we_h * v
        nn = s0_h * mn_ref[h] + we_h * k
        mn_o[h] = nn
        num = s0_h * qc + jnp.sum(q * k, axis=-1, keepdims=True) * wv
        den = jnp.sum(q * nn, axis=-1, keepdims=True)
        hh = num / jnp.maximum(jnp.abs(den), floor[:, h:h + 1])
        for b in range(nb):
            wsel = _pad_rows16(jnp.where(rows == b, wv, 0.0))
            mc_o[b, h] = s0[b:b + 1, h:h + 1] * mc_ref[b, h] + _dot(kt, wsel)
        o_pre = proj_ref[:, E_OP + h * DV_B:E_OP + (h + 1) * DV_B].astype(F32)
        z = proj_ref[:, E_Z + H_A * DV_A + h * DV_B:
                     E_Z + H_A * DV_A + (h + 1) * DV_B].astype(F32)
        y_ref[:, H_A * DV_A + h * DV_B:H_A * DV_A + (h + 1) * DV_B] = (
            _sigmoid(o_pre) * _rms_rows(hh, mnw) * _silu(z))


def _even_sample(proj, small, conv, gdn, mc, mn, mm, sp, cw, gnw, mnw):
    batch = proj.shape[0]
    nb = DEC_BB
    r2 = lambda i: (i, 0)
    m3 = lambda i: (0, i, 0)
    i4 = lambda i: (i, 0, 0, 0)
    c2 = lambda i: (0, 0)
    return pl.pallas_call(
        _even_sample_kernel,
        grid=(batch // nb,),
        in_specs=[
            pl.BlockSpec((nb, P_E), r2),
            pl.BlockSpec((nb, SMALL_E), r2),
            pl.BlockSpec((CONV_W - 1, nb, CONV_CH), m3),
            pl.BlockSpec((nb, H_A, DK_A, DV_A), i4),
            pl.BlockSpec((nb, H_B, DK_B, DV_B), i4),
            pl.BlockSpec((H_B, nb, DK_B), m3),
            pl.BlockSpec((nb, LANE), r2),
            pl.BlockSpec((8, LANE), c2),
            pl.BlockSpec((CONV_W, CONV_CH), c2),
            pl.BlockSpec((1, DV_A), c2),
            pl.BlockSpec((1, DV_B), c2),
        ],
        out_specs=[
            pl.BlockSpec((nb, W_MIX), r2),
            pl.BlockSpec((CONV_W - 1, nb, CONV_CH), m3),
            pl.BlockSpec((nb, H_A, DK_A, DV_A), i4),
            pl.BlockSpec((nb, H_B, DK_B, DV_B), i4),
            pl.BlockSpec((H_B, nb, DK_B), m3),
            pl.BlockSpec((nb, LANE), r2),
        ],
        out_shape=[
            jax.ShapeDtypeStruct((batch, W_MIX), F32),
            jax.ShapeDtypeStruct((CONV_W - 1, batch, CONV_CH), F32),
            jax.ShapeDtypeStruct((batch, H_A, DK_A, DV_A), F32),
            jax.ShapeDtypeStruct((batch, H_B, DK_B, DV_B), F32),
            jax.ShapeDtypeStruct((H_B, batch, DK_B), F32),
            jax.ShapeDtypeStruct((batch, LANE), F32),
        ],
        scratch_shapes=[pltpu.VMEM((nb, CONV_CH), F32)],
        compiler_params=pltpu.CompilerParams(
            dimension_semantics=("arbitrary",), vmem_limit_bytes=VMEM_LIMIT),
        name="even_sample",
    )(proj, small, conv, gdn, mc, mn, mm, sp, cw, gnw, mnw)


def _odd_sample_kernel(proj_ref, glr_ref, w2_ref, b2_ref, gla_ref, ret_ref, cc_ref, ss_ref,
                       cnw_ref, dnw_ref, y_ref, gla_o, ret_o):
    nb = DEC_BB
    cnw = cnw_ref[...]
    dnw = dnw_ref[...]
    cc = cc_ref[...]
    ss = ss_ref[...]
    rows = lax.broadcasted_iota(jnp.int32, (nb, DV_C), 0)
    g = _log_sigmoid(_dot(glr_ref[...].astype(BF16), w2_ref[...]) + b2_ref[...]) / GLA_TAU

    for h in range(H_C):
        q = proj_ref[:, O_QC + h * DK_C:O_QC + (h + 1) * DK_C].astype(F32) * (DK_C ** -0.5)
        k = proj_ref[:, O_KC + h * DK_C:O_KC + (h + 1) * DK_C]
        v = proj_ref[:, O_VC + h * DV_C:O_VC + (h + 1) * DV_C].astype(F32)
        eg = jnp.exp(g[:, h * DK_C:(h + 1) * DK_C])
        qe = _pad_rows16(q * eg)
        kt = _transpose_bf16(k)
        egt = _transpose(eg)
        qs = jnp.zeros((nb, DV_C), F32)
        for b in range(nb):
            qs = _row_select(rows, b, qs, _dot(qe, gla_ref[b, h].astype(BF16))[0:nb])
        o = qs + jnp.sum(q * k.astype(F32), axis=-1, keepdims=True) * v
        for b in range(nb):
            vsel = _pad_rows16(jnp.where(rows == b, v, 0.0))
            gla_o[b, h] = egt[:, b:b + 1] * gla_ref[b, h] + _dot(kt, vsel)
        z = proj_ref[:, O_Z + h * DV_C:O_Z + (h + 1) * DV_C].astype(F32)
        y_ref[:, h * DV_C:(h + 1) * DV_C] = _rms_rows(o, cnw) * _silu(z)

    for h in range(H_D):
        gamma = math.exp(_ret_log_gamma(h))
        qd = proj_ref[:, O_QD + h * DK_D:O_QD + (h + 1) * DK_D].astype(F32)
        kd = proj_ref[:, O_KD + h * DK_D:O_KD + (h + 1) * DK_D].astype(F32)
        qd = qd * cc + pltpu.roll(qd, DK_D // 2, 1) * ss
        kd = (kd * cc + pltpu.roll(kd, DK_D // 2, 1) * ss) * (DK_D ** -0.5)
        v = proj_ref[:, O_VD + h * DV_D:O_VD + (h + 1) * DV_D].astype(F32)
        qb = _pad_rows16(qd)
        kt = _transpose_bf16(kd)
        qs = jnp.zeros((nb, DV_D), F32)
        for b in range(nb):
            qs = _row_select(rows, b, qs, _dot(qb, ret_ref[b, h].astype(BF16))[0:nb])
        o = gamma * qs + jnp.sum(qd * kd, axis=-1, keepdims=True) * v
        for b in range(nb):
            vsel = _pad_rows16(jnp.where(rows == b, v, 0.0))
            ret_o[b, h] = gamma * ret_ref[b, h] + _dot(kt, vsel)
        z = proj_ref[:, O_Z + H_C * DV_C + h * DV_D:
                     O_Z + H_C * DV_C + (h + 1) * DV_D].astype(F32)
        y_ref[:, H_C * DV_C + h * DV_D:H_C * DV_C + (h + 1) * DV_D] = (
            _rms_rows(o, dnw) * _silu(z))


def _odd_sample(proj, glr, w2, b2, gla, ret, cc, ss, cnw, dnw):
    batch = proj.shape[0]
    nb = DEC_BB
    r2 = lambda i: (i, 0)
    i4 = lambda i: (i, 0, 0, 0)
    c2 = lambda i: (0, 0)
    return pl.pallas_call(
        _odd_sample_kernel,
        grid=(batch // nb,),
        in_specs=[
            pl.BlockSpec((nb, P_O), r2),
            pl.BlockSpec((nb, SMALL_O), r2),
            pl.BlockSpec((LANE, H_C * DK_C), c2),
            pl.BlockSpec((1, H_C * DK_C), c2),
            pl.BlockSpec((nb, H_C, DK_C, DV_C), i4),
            pl.BlockSpec((nb, H_D, DK_D, DV_D), i4),
            pl.BlockSpec((1, DK_D), c2),
            pl.BlockSpec((1, DK_D), c2),
            pl.BlockSpec((1, DV_C), c2),
            pl.BlockSpec((1, DV_D), c2),
        ],
        out_specs=[
            pl.BlockSpec((nb, W_MIX), r2),
            pl.BlockSpec((nb, H_C, DK_C, DV_C), i4),
            pl.BlockSpec((nb, H_D, DK_D, DV_D), i4),
        ],
        out_shape=[
            jax.ShapeDtypeStruct((batch, W_MIX), F32),
            jax.ShapeDtypeStruct((batch, H_C, DK_C, DV_C), F32),
            jax.ShapeDtypeStruct((batch, H_D, DK_D, DV_D), F32),
        ],
        compiler_params=pltpu.CompilerParams(
            dimension_semantics=("arbitrary",), vmem_limit_bytes=VMEM_LIMIT),
        name="odd_sample",
    )(proj, glr, w2, b2, gla, ret, cc, ss, cnw, dnw)


def _pad_cols(w, width):
    return jnp.pad(w, ((0, 0), (0, width - w.shape[1])))


def _prep_even_w_in(w):
    u = w[:, 0:3072]
    beta = w[:, 3072:3080]
    a = w[:, 3080:3088]
    qb = w[:, 3088:3600].reshape(-1, H_B, DK_B)
    kb = w[:, 3600:4112].reshape(-1, H_B, DK_B)
    vb = w[:, 4112:5136]
    ig = w[:, 5136:5144]
    fg = w[:, 5144:5152]
    op = w[:, 5152:6176]
    z = w[:, 6176:8224]
    qk = jnp.concatenate([qb, kb], axis=-1).reshape(-1, H_B * 2 * DK_B)
    small = jnp.concatenate([_pad_cols(beta, 128), _pad_cols(a, 128), _pad_cols(ig, 128),
                             _pad_cols(fg, 128)], axis=1)
    return jnp.concatenate([u, qk, vb, op, z], axis=1).astype(BF16), small.astype(BF16)


def _prep_odd_w_in(w):
    qc = w[:, 0:512]
    kc = w[:, 512:1024]
    vc = w[:, 1024:2048]
    glr = w[:, 2048:2064]
    qd = w[:, 2064:2576]
    kd = w[:, 2576:3088]
    vd = w[:, 3088:4112]
    z = w[:, 4112:6160]
    return (jnp.concatenate([qc, kc, vc, qd, kd, vd, z], axis=1).astype(BF16),
            _pad_cols(glr, SMALL_O).astype(BF16))


def _lane_row(vec, width=LANE):
    return jnp.pad(vec.astype(F32), (0, width - vec.shape[0]))


def _rotary_tables(pos):
    half = DK_D // 2
    inv = ROPE_BASE ** (-jnp.arange(half, dtype=F32) / half)
    ang = pos.astype(F32)[:, None] * inv[None, :]
    cos = jnp.cos(ang)
    sin = jnp.sin(ang)
    return jnp.concatenate([cos, cos], axis=-1), jnp.concatenate([-sin, sin], axis=-1)


def kernel(x_prompt, x_sample, c_prompt, c_sample, state_gdn, state_gdn_conv, state_mlstm_c,
           state_mlstm_n, state_mlstm_m, state_gla, state_ret, ada_w, ada_b, norm_w, ev_w_in,
           ev_w_out, gdn_conv_w, gdn_a_log, gdn_dt_bias, gdn_norm_w, mlstm_gate_b, mlstm_norm_w,
           od_w_in, od_w_out, gla_w2, gla_b2, gla_norm_w, ret_norm_w, final_norm_w):
    bp, tp, d = x_prompt.shape
    bs = x_sample.shape[0]
    n_p = bp * tp

    w_in_e, w_sm_e = _prep_even_w_in(ev_w_in[0])
    w_in_o, w_sm_o = _prep_odd_w_in(od_w_in[0])
    w_out_e = ev_w_out[0].astype(BF16)
    w_out_o = od_w_out[0].astype(BF16)
    sp = jnp.stack([_lane_row(gdn_dt_bias[0]), _lane_row(gdn_a_log[0]),
                    _lane_row(mlstm_gate_b[0, :H_B]), _lane_row(mlstm_gate_b[0, H_B:])]
                   + [jnp.zeros((LANE,), F32)] * 4)
    gnw = gdn_norm_w[0].reshape(1, DV_A)
    mnw = mlstm_norm_w[0].reshape(1, DV_B)
    cnw = gla_norm_w[0].reshape(1, DV_C)
    dnw = ret_norm_w[0].reshape(1, DV_D)
    w2 = jnp.pad(gla_w2[0], ((0, LANE - GLA_RANK), (0, 0))).astype(BF16)
    b2 = gla_b2[0].reshape(1, H_C * DK_C)

    mod = _modulation(jnp.concatenate([c_prompt, c_sample], axis=0), ada_w, ada_b)
    mod_p = [mod[l, :bp].reshape(bp, 1, 3 * d) for l in range(2)]
    mod_s = [mod[l, bp:].reshape(1, bs, 3 * d) for l in range(2)]

    xp = x_prompt.reshape(n_p, d)
    proj, small = _inproj(xp, mod_p[0], norm_w[0], w_in_e, w_sm_e, 1024, 2048)
    y, p_gdn, p_conv, p_mc, p_mn, p_mm = _even_prompt(
        proj.reshape(bp, tp, P_E), small.reshape(bp, tp, SMALL_E), sp, gdn_conv_w[0], gnw, mnw,
        bp, tp)
    xp = _outproj(y.reshape(n_p, W_MIX), xp, mod_p[0], w_out_e, final_norm_w, 512, False)
    proj, small = _inproj(xp, mod_p[1], norm_w[1], w_in_o, w_sm_o, 1024, 2048)
    cc_p, ss_p = _rotary_tables(jnp.arange(tp))
    y, p_gla, p_ret = _odd_prompt(proj, small, w2, b2, cc_p, ss_p, cnw, dnw, bp, tp)
    y_prompt = _outproj(y, xp, mod_p[1], w_out_o, final_norm_w, 512, True).reshape(bp, tp, d)

    xs = x_sample.reshape(bs, d)
    proj, small = _inproj(xs, mod_s[0], norm_w[0], w_in_e, w_sm_e, bs, 2048)
    mm_in = jnp.pad(state_mlstm_m[0], ((0, 0), (0, LANE - H_B)))
    y, s_conv, s_gdn, s_mc, s_mn, s_mm = _even_sample(
        proj, small, jnp.transpose(state_gdn_conv[0], (1, 0, 2)), state_gdn[0], state_mlstm_c[0],
        jnp.transpose(state_mlstm_n[0], (1, 0, 2)), mm_in, sp, gdn_conv_w[0], gnw, mnw)
    xs = _outproj(y, xs, mod_s[0], w_out_e, final_norm_w, bs, False)
    proj, small = _inproj(xs, mod_s[1], norm_w[1], w_in_o, w_sm_o, bs, 2048)
    cc_s, ss_s = _rotary_tables(PAST_LEN + jnp.arange(1))
    y, s_gla, s_ret = _odd_sample(proj, small, w2, b2, state_gla[0], state_ret[0], cc_s, ss_s,
                                  cnw, dnw)
    y_sample = _outproj(y, xs, mod_s[1], w_out_o, final_norm_w, bs, True)
    y_sample = y_sample.reshape(bs, 1, d)

    return (y_prompt, y_sample,
            p_gdn[None], p_conv[None], p_mc[None], p_mn[None], p_mm[None, :, 0, :H_B],
            p_gla[None], p_ret[None],
            s_gdn[None], jnp.transpose(s_conv, (1, 0, 2))[None], s_mc[None],
            jnp.transpose(s_mn, (1, 0, 2))[None], s_mm[None, :, :H_B], s_gla[None], s_ret[None])
```

```python
import functools
import math

import jax
import jax.numpy as jnp
import numpy as np
from jax import lax
from jax.experimental import pallas as pl
from jax.experimental.pallas import tpu as pltpu

F32 = jnp.float32
BF16 = jnp.bfloat16
HI = lax.Precision.HIGHEST

D_MODEL = 1024
W_MIX = 2048
H_A, DK_A, DV_A, CONV_W = 8, 128, 128, 4
CONV_CH = H_A * (2 * DK_A + DV_A)
H_B, DK_B, DV_B = 8, 64, 128
H_C, DK_C, DV_C, GLA_RANK, GLA_TAU = 4, 128, 256, 16, 16.0
H_D, DK_D, DV_D = 4, 128, 256
ROPE_BASE = 10000.0
PAST_LEN = 16384
EPS = 1e-6
LANE = 128

CHUNK = 64
CHUNKS_PER_STEP = 4

E_U, E_QK, E_V, E_OP, E_Z = 0, 3072, 4096, 5120, 6144
P_E = 8192
SMALL_E = 512
O_QC, O_KC, O_VC, O_QD, O_KD, O_VD, O_Z = 0, 512, 1024, 2048, 2560, 3072, 4096
P_O = 6144
SMALL_O = 128

VMEM_LIMIT = 56 * 1024 * 1024


def _sigmoid(x):
    return 0.5 * jnp.tanh(0.5 * x) + 0.5


def _softplus(x):
    return jnp.maximum(x, 0.0) + jnp.log(1.0 + jnp.exp(-jnp.abs(x)))


def _log_sigmoid(x):
    return -_softplus(-x)


def _silu(x):
    half = 0.5 * x
    return half + half * jnp.tanh(half)


def _rms_rows(x, w):
    return x * lax.rsqrt(jnp.mean(x * x, axis=-1, keepdims=True) + EPS) * w


def _l2n(x):
    return x * lax.rsqrt(jnp.sum(x * x, axis=-1, keepdims=True) + EPS)


def _dot(a, b, precision=None):
    return jnp.dot(a, b, preferred_element_type=F32, precision=precision)


def _dot_nt(a, b, precision=None):
    return lax.dot_general(a, b, (((1,), (1,)), ((), ())), preferred_element_type=F32,
                           precision=precision)


def _dot_tn(a, b, precision=None):
    return lax.dot_general(a, b, (((0,), (0,)), ((), ())), preferred_element_type=F32,
                           precision=precision)


def _eye(n):
    r = lax.broadcasted_iota(jnp.int32, (n, n), 0)
    c = lax.broadcasted_iota(jnp.int32, (n, n), 1)
    return (r == c).astype(F32)


def _bf16_terms(x, terms):
    parts = []
    rest = x
    for _ in range(terms):
        piece = rest.astype(BF16)
        parts.append(piece)
        rest = rest - piece.astype(F32)
    return parts


def _select_dot(sel, x, terms):
    n = x.shape[1]
    out = _dot(sel.astype(BF16), jnp.concatenate(_bf16_terms(x, terms), axis=1))
    acc = out[:, 0:n]
    for i in range(1, terms):
        acc = acc + out[:, i * n:(i + 1) * n]
    return acc


def _transpose(x):
    return _dot_nt(_eye(x.shape[1]), x, precision=HI)


def _tri_masks(n):
    r = lax.broadcasted_iota(jnp.int32, (n, n), 0)
    c = lax.broadcasted_iota(jnp.int32, (n, n), 1)
    return r >= c, r > c


def _mod_kernel(c_ref, w_ref, b_ref, o_ref):
    cs = _silu(c_ref[...]).astype(BF16)
    o_ref[...] = _dot(cs, w_ref[...].astype(BF16)) + b_ref[...]


def _modulation(c_all, ada_w, ada_b):
    depth, d, d3 = ada_w.shape
    rows = c_all.shape[0]
    tn = 1024
    return pl.pallas_call(
        _mod_kernel,
        grid=(depth, d3 // tn),
        in_specs=[
            pl.BlockSpec((rows, d), lambda l, j: (0, 0)),
            pl.BlockSpec((None, d, tn), lambda l, j: (l, 0, j)),
            pl.BlockSpec((None, 1, tn), lambda l, j: (l, 0, j)),
        ],
        out_specs=pl.BlockSpec((None, rows, tn), lambda l, j: (l, 0, j)),
        out_shape=jax.ShapeDtypeStruct((depth, rows, d3), F32),
        compiler_params=pltpu.CompilerParams(
            dimension_semantics=("arbitrary", "arbitrary"), vmem_limit_bytes=VMEM_LIMIT),
        name="modulation",
    )(c_all, ada_w, ada_b.reshape(depth, 1, d3))


def _inproj_kernel(x_ref, mod_ref, nw_ref, w_ref, ws_ref, o_ref, os_ref, h_ref):
    @pl.when(pl.program_id(1) == 0)
    def _():
        x = x_ref[...]
        y = _rms_rows(x, nw_ref[...])
        shift = mod_ref[:, 0:D_MODEL]
        scale = mod_ref[:, D_MODEL:2 * D_MODEL]
        h_ref[...] = (y * (1.0 + scale) + shift).astype(BF16)
        os_ref[...] = _dot(h_ref[...], ws_ref[...])

    o_ref[...] = _dot(h_ref[...], w_ref[...]).astype(o_ref.dtype)


def _inproj(x, mod, norm_w, w, w_small, tm, tn):
    n, d = x.shape
    p = w.shape[1]
    ps = w_small.shape[1]
    g, r, _ = mod.shape
    tiles_per_group = (n // g) // tm
    return pl.pallas_call(
        _inproj_kernel,
        grid=(n // tm, p // tn),
        in_specs=[
            pl.BlockSpec((tm, d), lambda i, j: (i, 0)),
            pl.BlockSpec((None, r, 3 * d), lambda i, j: (i // tiles_per_group, 0, 0)),
            pl.BlockSpec((1, d), lambda i, j: (0, 0)),
            pl.BlockSpec((d, tn), lambda i, j: (0, j)),
            pl.BlockSpec((d, ps), lambda i, j: (0, 0)),
        ],
        out_specs=[pl.BlockSpec((tm, tn), lambda i, j: (i, j)),
                   pl.BlockSpec((tm, ps), lambda i, j: (i, 0))],
        out_shape=[jax.ShapeDtypeStruct((n, p), BF16), jax.ShapeDtypeStruct((n, ps), F32)],
        scratch_shapes=[pltpu.VMEM((tm, d), BF16)],
        compiler_params=pltpu.CompilerParams(
            dimension_semantics=("arbitrary", "arbitrary"), vmem_limit_bytes=VMEM_LIMIT),
        name="inproj",
    )(x, mod, norm_w.reshape(1, d), w, w_small)


def _outproj_kernel(y_ref, x_ref, mod_ref, w_ref, fw_ref, o_ref, *, final):
    acc = _dot(y_ref[...].astype(BF16), w_ref[...])
    gate = mod_ref[:, 2 * D_MODEL:3 * D_MODEL]
    xn = x_ref[...] + gate * acc
    if final:
        xn = _rms_rows(xn, fw_ref[...])
    o_ref[...] = xn


def _outproj(y, x, mod, w, final_w, tm, final):
    n, d = x.shape
    k = y.shape[1]
    g, r, _ = mod.shape
    tiles_per_group = (n // g) // tm
    return pl.pallas_call(
        functools.partial(_outproj_kernel, final=final),
        grid=(n // tm,),
        in_specs=[
            pl.BlockSpec((tm, k), lambda i: (i, 0)),
            pl.BlockSpec((tm, d), lambda i: (i, 0)),
            pl.BlockSpec((None, r, 3 * d), lambda i: (i // tiles_per_group, 0, 0)),
            pl.BlockSpec((k, d), lambda i: (0, 0)),
            pl.BlockSpec((1, d), lambda i: (0, 0)),
        ],
        out_specs=pl.BlockSpec((tm, d), lambda i: (i, 0)),
        out_shape=jax.ShapeDtypeStruct((n, d), F32),
        compiler_params=pltpu.CompilerParams(
            dimension_semantics=("arbitrary",), vmem_limit_bytes=VMEM_LIMIT),
        name="outproj",
    )(y, x, mod, w, final_w.reshape(1, d))


def _neumann_correction(a_list, n):
    ys = [-a for a in a_list]
    pbs = [a.astype(BF16) for a in a_list]
    for _ in range(int(math.log2(n)) - 2):
        for i in range(len(ys)):
            p = _dot(pbs[i], pbs[i])
            pbs[i] = p.astype(BF16)
            ys[i] = ys[i] + p + _dot(ys[i].astype(BF16), pbs[i])
        yield
    idx = range(len(ys))
    a_h = [a.astype(BF16) for a in a_list]
    a_l = [(a_list[i] - a_h[i].astype(F32)).astype(BF16) for i in idx]
    y_h = [y.astype(BF16) for y in ys]
    y_l = [(ys[i] - y_h[i].astype(F32)).astype(BF16) for i in idx]
    ay = [_dot(a_h[i], y_h[i]) + _dot(a_h[i], y_l[i]) + _dot(a_l[i], y_h[i]) for i in idx]
    e = [-(a_list[i] + ys[i] + ay[i]) for i in idx]
    return [ys[i] + e[i] + _dot(y_h[i], e[i].astype(BF16)) for i in idx]


def _run_staggered(stage_generators, offset):
    pending = list(stage_generators)
    active = []
    tick = 0
    while pending or active:
        if pending and tick % offset == 0:
            active.append(pending.pop(0))
        for g in list(active):
            try:
                next(g)
            except StopIteration:
                active.remove(g)
        tick += 1


EVEN_STAGGER = 13
ODD_STAGGER = 9


def _conv_shift_matrix():
    n = CHUNK
    r = np.arange((CONV_W - 1) * n)[:, None]
    col = np.arange(2 * n)[None, :]
    return jnp.asarray(col == n + (r % n) - (r // n + 1), BF16)


def _even_prompt_kernel(proj_ref, sm_ref, sp_ref, cw_ref, shift_ref, gnw_ref, mnw_ref,
                        y_ref, gdn_ref, conv_ref, mc_ref, mn_ref, mm_ref,
                        uprev_ref, qkv_ref):
    chunks = []
    for j in range(CHUNKS_PER_STEP):
        rows = pl.ds(j * CHUNK, CHUNK)
        chunks.append(_even_prompt_sequence(
            j == 0, proj_ref.at[0, rows], sm_ref.at[0, rows], sp_ref, cw_ref, shift_ref, gnw_ref,
            mnw_ref, y_ref.at[0, rows], gdn_ref.at[0], conv_ref.at[0], mc_ref.at[0], mn_ref.at[0],
            mm_ref.at[0], uprev_ref.at[0], qkv_ref.at[0]))
    _run_staggered(chunks, EVEN_STAGGER)


def _even_prompt_sequence(first_in_step, proj_ref, sm_ref, sp_ref, cw_ref, shift_ref, gnw_ref,
                          mnw_ref, y_ref, gdn_ref, conv_ref, mc_ref, mn_ref, mm_ref,
                          uprev_ref, qkv_ref):
    L = CHUNK

    if first_in_step:
        @pl.when(pl.program_id(1) == 0)
        def _():
            gdn_ref[...] = jnp.zeros_like(gdn_ref)
            mc_ref[...] = jnp.zeros_like(mc_ref)
            mn_ref[...] = jnp.zeros_like(mn_ref)
            mm_ref[...] = jnp.zeros_like(mm_ref)
            uprev_ref[...] = jnp.zeros_like(uprev_ref)

    tri_incl, tri_strict = _tri_masks(L)
    tri_f = tri_incl.astype(F32)

    ub = proj_ref[:, E_U:E_U + CONV_CH]
    shifted = _dot(shift_ref[...], jnp.concatenate([uprev_ref[...], ub], axis=0))
    uprev_ref[...] = ub
    u = ub.astype(F32)
    cw = cw_ref[...]
    uc = (cw[0:1] * shifted[2 * L:3 * L] + cw[1:2] * shifted[L:2 * L]
          + cw[2:3] * shifted[0:L] + cw[3:4] * u)
    conv_ref[...] = u[L - 3:L, :]
    qkv_ref[...] = _silu(uc)
    yield

    sm = sm_ref[...]
    head_lane = lax.broadcasted_iota(jnp.int32, (L, LANE), 1) < H_A
    beta = _sigmoid(sm[:, 0:128])
    g = -jnp.exp(sp_ref[1:2, :]) * _softplus(sm[:, 128:256] + sp_ref[0:1, :])
    ig = sm[:, 256:384] + sp_ref[2:3, :]
    lf = _log_sigmoid(sm[:, 384:512] + sp_ref[3:4, :])
    packed = jnp.where(head_lane, g, 0.0) + pltpu.roll(jnp.where(head_lane, lf, 0.0), H_A, 1)
    cum = _dot(tri_f, packed, precision=HI)
    cum_t = _transpose(cum + pltpu.roll(jnp.where(head_lane, ig, 0.0), H_A + H_B, 1))

    gnw = gnw_ref[...]
    mnw = mnw_ref[...]
    yield

    qkb, kf, decay, beta_c, gam, bh_c = [], [], [], [], [], []
    for h in range(H_A):
        q = _l2n(qkv_ref[:, h * DK_A:(h + 1) * DK_A]) * (DK_A ** -0.5)
        k = _l2n(qkv_ref[:, H_A * DK_A + h * DK_A:H_A * DK_A + (h + 1) * DK_A])
        kf.append(k)
        qkb.append(jnp.concatenate([q, k], axis=0).astype(BF16))
    yield
    qkk = [_dot_nt(qkb[h], qkb[h][L:2 * L]) for h in range(H_A)]
    yield
    a_list = []
    for h in range(H_A):
        bh_c.append(cum[:, h:h + 1])
        diff = bh_c[h] - cum_t[h:h + 1, :]
        decay.append(jnp.where(tri_incl, jnp.exp(jnp.where(tri_incl, diff, 0.0)), 0.0))
        beta_c.append(beta[:, h:h + 1])
        gam.append(jnp.exp(bh_c[h]))
        a_list.append(jnp.where(tri_strict, decay[h] * qkk[h][L:2 * L], 0.0) * beta_c[h])
    yield
    corr = yield from _neumann_correction(a_list, L)
    yield

    heads = range(H_A)
    s_old = [gdn_ref[h] for h in heads]
    qks = [_dot(qkb[h], s_old[h].astype(BF16)) for h in heads]
    yield
    rhs = [beta_c[h] * (qkv_ref[:, 2 * H_A * DK_A + h * DV_A:2 * H_A * DK_A + (h + 1) * DV_A]
                        - gam[h] * qks[h][L:2 * L]) for h in heads]
    ub = [(rhs[h] + _dot(corr[h].astype(BF16), rhs[h].astype(BF16))).astype(BF16) for h in heads]
    yield
    o = [gam[h] * qks[h][0:L] + _dot((qkk[h][0:L] * decay[h]).astype(BF16), ub[h])
         for h in heads]
    yield
    for h in heads:
        last = bh_c[h][L - 1:L, :]
        kw = (kf[h] * jnp.exp(last - bh_c[h])).astype(BF16)
        gdn_ref[h] = jnp.exp(last) * s_old[h] + _dot_tn(kw, ub[h])
    yield
    for h in heads:
        z = proj_ref[:, E_Z + h * DV_A:E_Z + (h + 1) * DV_A].astype(F32)
        y_ref[:, h * DV_A:(h + 1) * DV_A] = (_rms_rows(o[h], gnw) * _silu(z)).astype(BF16)
    yield

    heads = range(H_B)
    qs, qbs, kbs, vbs = [], [], [], []
    for h in heads:
        qk_in = proj_ref[:, E_QK + h * 128:E_QK + (h + 1) * 128]
        qbs.append(qk_in[:, 0:DK_B])
        qs.append(qbs[h].astype(F32))
        kbs.append((qk_in[:, DK_B:2 * DK_B].astype(F32) * (DK_B ** -0.5)).astype(BF16))
        vbs.append(proj_ref[:, E_V + h * DV_B:E_V + (h + 1) * DV_B])
    qkm = [_dot_nt(qbs[h], kbs[h]) for h in heads]
    yield
    b_c = [cum[:, H_A + h:H_A + h + 1] for h in heads]
    m0 = [mm_ref[0:1, h:h + 1] for h in heads]
    dmat = [jnp.where(tri_incl, b_c[h] - cum_t[H_A + h:H_A + h + 1, :]
                      + cum_t[H_A + H_B + h:H_A + H_B + h + 1, :], -jnp.inf) for h in heads]
    w0 = [b_c[h] + m0[h] for h in heads]
    m_t = [jnp.maximum(w0[h], jnp.max(dmat[h], axis=-1, keepdims=True)) for h in heads]
    p = [jnp.exp(dmat[h] - m_t[h]) * qkm[h] for h in heads]
    s0 = [jnp.exp(w0[h] - m_t[h]) for h in heads]
    yield
    c_old = [mc_ref[h] for h in heads]
    n_old = [mn_ref[h:h + 1, :] for h in heads]
    num = [s0[h] * _dot(qbs[h], c_old[h].astype(BF16)) + _dot(p[h].astype(BF16), vbs[h])
           for h in heads]
    den = [s0[h] * jnp.sum(qs[h] * n_old[h], axis=-1, keepdims=True)
           + jnp.sum(p[h], axis=-1, keepdims=True) for h in heads]
    hh = [num[h] / jnp.maximum(jnp.abs(den[h]), jnp.exp(-m_t[h])) for h in heads]
    yield
    for h in heads:
        m_end = m_t[h][L - 1:L, :]
        b_last = b_c[h][L - 1:L, :]
        we = jnp.exp(b_last - b_c[h] + ig[:, h:h + 1] - m_end)
        se = jnp.exp(b_last + m0[h] - m_end)
        kw = kbs[h].astype(F32) * we
        mc_ref[h] = se * c_old[h] + _dot_tn(kw.astype(BF16), vbs[h])
        mn_ref[h:h + 1, :] = se * n_old[h] + jnp.sum(kw, axis=0, keepdims=True)
        mm_ref[0:1, h:h + 1] = m_end
    yield
    for h in heads:
        o_pre = proj_ref[:, E_OP + h * DV_B:E_OP + (h + 1) * DV_B].astype(F32)
        z = proj_ref[:, E_Z + H_A * DV_A + h * DV_B:
                     E_Z + H_A * DV_A + (h + 1) * DV_B].astype(F32)
        y_ref[:, H_A * DV_A + h * DV_B:H_A * DV_A + (h + 1) * DV_B] = (
            _sigmoid(o_pre) * _rms_rows(hh[h], mnw) * _silu(z)).astype(BF16)


def _even_prompt(proj, small, sp, conv_w, gnw, mnw, batch, seq):
    step = CHUNK * CHUNKS_PER_STEP
    nc = seq // step
    nb = 1
    row = lambda b, c: (b, c, 0)
    const2 = lambda b, c: (0, 0)
    st3 = lambda b, c: (b, 0, 0)
    st4 = lambda b, c: (b, 0, 0, 0)
    return pl.pallas_call(
        _even_prompt_kernel,
        grid=(batch // nb, nc),
        in_specs=[
            pl.BlockSpec((nb, step, P_E), row),
            pl.BlockSpec((nb, step, SMALL_E), row),
            pl.BlockSpec((8, LANE), const2),
            pl.BlockSpec((CONV_W, CONV_CH), const2),
            pl.BlockSpec(((CONV_W - 1) * CHUNK, 2 * CHUNK), const2),
            pl.BlockSpec((1, DV_A), const2),
            pl.BlockSpec((1, DV_B), const2),
        ],
        out_specs=[
            pl.BlockSpec((nb, step, W_MIX), row),
            pl.BlockSpec((nb, H_A, DK_A, DV_A), st4),
            pl.BlockSpec((nb, CONV_W - 1, CONV_CH), st3),
            pl.BlockSpec((nb, H_B, DK_B, DV_B), st4),
            pl.BlockSpec((nb, H_B, DK_B), st3),
            pl.BlockSpec((nb, 1, LANE), st3),
        ],
        out_shape=[
            jax.ShapeDtypeStruct((batch, seq, W_MIX), BF16),
            jax.ShapeDtypeStruct((batch, H_A, DK_A, DV_A), F32),
            jax.ShapeDtypeStruct((batch, CONV_W - 1, CONV_CH), F32),
            jax.ShapeDtypeStruct((batch, H_B, DK_B, DV_B), F32),
            jax.ShapeDtypeStruct((batch, H_B, DK_B), F32),
            jax.ShapeDtypeStruct((batch, 1, LANE), F32),
        ],
        scratch_shapes=[pltpu.VMEM((nb, CHUNK, CONV_CH), BF16),
                        pltpu.VMEM((nb, CHUNK, CONV_CH), F32)],
        compiler_params=pltpu.CompilerParams(
            dimension_semantics=("arbitrary", "arbitrary"), vmem_limit_bytes=VMEM_LIMIT),
        name="even_prompt",
    )(proj, small, sp, conv_w, _conv_shift_matrix(), gnw, mnw)


def _ret_log_gamma(h):
    return math.log(1.0 - 2.0 ** (-5.0 - h))


GLA_LEVELS = int(math.log2(CHUNK))
GLA_MM_LEVELS = (1, 2)
LOG2E = math.log2(math.e)


def _gla_chunk_tables():
    n = CHUNK
    t = np.arange(n)[:, None]
    i = np.arange(n)[None, :]
    blocks = [(i <= t)]
    for lvl in GLA_MM_LEVELS:
        hs = 1 << lvl
        p = (t & ~(2 * hs - 1)) + hs - 1
        upper = (t & hs) != 0
        blocks.append(np.where(upper, (i > p) & (i <= t), (i > t) & (i <= p)))
    sums = np.concatenate(blocks, axis=0).astype(np.float32)
    s = np.arange(n)[None, :]
    level = np.full((n, n), GLA_LEVELS + 1, np.int32)
    level[np.arange(n), np.arange(n)] = GLA_LEVELS
    for lvl in range(GLA_LEVELS):
        hs = 1 << lvl
        same = (t >> (lvl + 1)) == (s >> (lvl + 1))
        level[same & ((t & hs) != 0) & ((s & hs) == 0)] = lvl
    return jnp.asarray(sums), jnp.asarray(level)


def _odd_prompt_kernel(proj_ref, glr_ref, w2_ref, b2_ref, sums_ref, level_ref, cc_ref, ss_ref,
                       cnw_ref, dnw_ref, y_ref, gla_ref, ret_ref, x_ref):
    chunks = []
    for j in range(CHUNKS_PER_STEP):
        rows = pl.ds(j * CHUNK, CHUNK)
        chunks.append(_odd_prompt_chunk(
            j == 0, proj_ref.at[rows], glr_ref.at[rows], w2_ref, b2_ref, sums_ref, level_ref,
            cc_ref.at[rows], ss_ref.at[rows], cnw_ref, dnw_ref, y_ref.at[rows], gla_ref, ret_ref,
            x_ref))
    _run_staggered(chunks, ODD_STAGGER)


def _odd_prompt_chunk(first_in_step, proj_ref, glr_ref, w2_ref, b2_ref, sums_ref, level_ref,
                      cc_ref, ss_ref, cnw_ref, dnw_ref, y_ref, gla_ref, ret_ref, x_ref):
    L = CHUNK

    if first_in_step:
        @pl.when(pl.program_id(1) == 0)
        def _():
            gla_ref[...] = jnp.zeros_like(gla_ref)
            ret_ref[...] = jnp.zeros_like(ret_ref)

    tri_incl, _ = _tri_masks(L)
    r64 = lax.broadcasted_iota(jnp.int32, (L, L), 0)
    c64 = lax.broadcasted_iota(jnp.int32, (L, L), 1)
    level = level_ref[...]

    glr = glr_ref[...].astype(BF16)
    gc2 = _log_sigmoid(_dot(glr, w2_ref[...]) + b2_ref[...]) * (LOG2E / GLA_TAU)
    x_ref[...] = _select_dot(sums_ref[...], gc2, 3)
    odd_row = (lax.broadcasted_iota(jnp.int32, (L, DK_C), 0) & 1) == 1
    yield

    def level_exponent(lvl, h):
        cols = slice(h * DK_C, (h + 1) * DK_C)
        if lvl == 0:
            return jnp.where(odd_row, gc2[:, cols], 0.0)
        if lvl in GLA_MM_LEVELS:
            r0 = (1 + GLA_MM_LEVELS.index(lvl)) * L
            return x_ref[r0:r0 + L, cols]
        hs = 1 << lvl
        pieces = []
        for lo in range(0, L, 2 * hs):
            ref = x_ref[lo + hs - 1:lo + hs, cols]
            pieces.append(ref - x_ref[lo:lo + hs, cols])
            pieces.append(x_ref[lo + hs:lo + 2 * hs, cols] - ref)
        return jnp.concatenate(pieces, axis=0)

    cnw = cnw_ref[...]
    dnw = dnw_ref[...]

    heads = range(H_C)
    q = [proj_ref[:, O_QC + h * DK_C:O_QC + (h + 1) * DK_C].astype(F32) * (DK_C ** -0.5)
         for h in heads]
    k = [proj_ref[:, O_KC + h * DK_C:O_KC + (h + 1) * DK_C].astype(F32) for h in heads]
    v = [proj_ref[:, O_VC + h * DV_C:O_VC + (h + 1) * DV_C] for h in heads]
    att = [jnp.where(level == GLA_LEVELS, _dot_nt(q[h].astype(BF16), k[h].astype(BF16)), 0.0)
           for h in heads]
    yield
    for lvl in range(GLA_LEVELS):
        for h in heads:
            e = jnp.exp2(level_exponent(lvl, h))
            m = _dot_nt((q[h] * e).astype(BF16), (k[h] * e).astype(BF16))
            att[h] = jnp.where(level == lvl, m, att[h])
        yield
    b2 = [x_ref[0:L, h * DK_C:(h + 1) * DK_C] for h in heads]
    s_gla = [gla_ref[h] for h in heads]
    o = [_dot((q[h] * jnp.exp2(b2[h])).astype(BF16), s_gla[h].astype(BF16))
         + _dot(att[h].astype(BF16), v[h]) for h in heads]
    yield
    for h in heads:
        bl = b2[h][L - 1:L, :]
        kw = (k[h] * jnp.exp2(bl - b2[h])).astype(BF16)
        col = jnp.transpose(jnp.broadcast_to(jnp.exp2(bl), (DK_C, DK_C)))
        upd = _dot_tn(kw, v[h])
        for half in range(DV_C // DK_C):
            lanes = slice(half * DK_C, (half + 1) * DK_C)
            gla_ref[h, :, lanes] = s_gla[h][:, lanes] * col + upd[:, lanes]
    yield
    for h in heads:
        z = proj_ref[:, O_Z + h * DV_C:O_Z + (h + 1) * DV_C].astype(F32)
        y_ref[:, h * DV_C:(h + 1) * DV_C] = (_rms_rows(o[h], cnw) * _silu(z)).astype(BF16)
    yield

    cc = cc_ref[...]
    ss = ss_ref[...]
    rel = (r64 - c64).astype(F32)
    tcol = lax.broadcasted_iota(jnp.int32, (L, 1), 0).astype(F32)
    heads = range(H_D)
    lg = [_ret_log_gamma(h) for h in heads]
    qb, kf, kb, vd = [], [], [], []
    for h in heads:
        qd = proj_ref[:, O_QD + h * DK_D:O_QD + (h + 1) * DK_D].astype(F32)
        kd = proj_ref[:, O_KD + h * DK_D:O_KD + (h + 1) * DK_D].astype(F32)
        qb.append((qd * cc + pltpu.roll(qd, DK_D // 2, 1) * ss).astype(BF16))
        kf.append((kd * cc + pltpu.roll(kd, DK_D // 2, 1) * ss) * (DK_D ** -0.5))
        kb.append(kf[h].astype(BF16))
        vd.append(proj_ref[:, O_VD + h * DV_D:O_VD + (h + 1) * DV_D])
    yield
    s_old = [ret_ref[h] for h in heads]
    attd = [_dot_nt(qb[h], kb[h])
            * jnp.where(tri_incl, jnp.exp(lg[h] * jnp.maximum(rel, 0.0)), 0.0) for h in heads]
    od = [_dot(qb[h], s_old[h].astype(BF16)) * jnp.exp(lg[h] * (tcol + 1.0))
          + _dot(attd[h].astype(BF16), vd[h]) for h in heads]
    yield
    for h in heads:
        end = jnp.exp(lg[h] * (L - 1.0 - tcol))
        ret_ref[h] = math.exp(lg[h] * L) * s_old[h] + _dot_tn((kf[h] * end).astype(BF16), vd[h])
    yield
    for h in heads:
        z = proj_ref[:, O_Z + H_C * DV_C + h * DV_D:
                     O_Z + H_C * DV_C + (h + 1) * DV_D].astype(F32)
        y_ref[:, H_C * DV_C + h * DV_D:H_C * DV_C + (h + 1) * DV_D] = (
            _rms_rows(od[h], dnw) * _silu(z)).astype(BF16)


def _odd_prompt(proj, glr, w2, b2, cc, ss, cnw, dnw, batch, seq):
    step = CHUNK * CHUNKS_PER_STEP
    nc = seq // step
    row = lambda b, c: (b * nc + c, 0)
    const2 = lambda b, c: (0, 0)
    sums, level = _gla_chunk_tables()
    n_sums = sums.shape[0]
    return pl.pallas_call(
        _odd_prompt_kernel,
        grid=(batch, nc),
        in_specs=[
            pl.BlockSpec((step, P_O), row),
            pl.BlockSpec((step, SMALL_O), row),
            pl.BlockSpec((LANE, H_C * DK_C), const2),
            pl.BlockSpec((1, H_C * DK_C), const2),
            pl.BlockSpec((n_sums, CHUNK), const2),
            pl.BlockSpec((CHUNK, CHUNK), const2),
            pl.BlockSpec((step, DK_D), lambda b, c: (c, 0)),
            pl.BlockSpec((step, DK_D), lambda b, c: (c, 0)),
            pl.BlockSpec((1, DV_C), const2),
            pl.BlockSpec((1, DV_D), const2),
        ],
        out_specs=[
            pl.BlockSpec((step, W_MIX), row),
            pl.BlockSpec((None, H_C, DK_C, DV_C), lambda b, c: (b, 0, 0, 0)),
            pl.BlockSpec((None, H_D, DK_D, DV_D), lambda b, c: (b, 0, 0, 0)),
        ],
        out_shape=[
            jax.ShapeDtypeStruct((batch * seq, W_MIX), BF16),
            jax.ShapeDtypeStruct((batch, H_C, DK_C, DV_C), F32),
            jax.ShapeDtypeStruct((batch, H_D, DK_D, DV_D), F32),
        ],
        scratch_shapes=[pltpu.VMEM((n_sums, H_C * DK_C), F32)],
        compiler_params=pltpu.CompilerParams(
            dimension_semantics=("arbitrary", "arbitrary"), vmem_limit_bytes=VMEM_LIMIT),
        name="odd_prompt",
    )(proj, glr, w2, b2, sums, level, cc, ss, cnw, dnw)


DEC_BB = 8


def _row_select(rows, row_index, acc, new):
    return jnp.where(rows == row_index, new, acc)


def _pad_rows16(x):
    return jnp.concatenate([x, jnp.zeros_like(x)], axis=0).astype(BF16)


def _transpose_bf16(x):
    eye = _eye(x.shape[1]).astype(BF16)
    return _dot_nt(eye, _pad_rows16(x)).astype(BF16)


def _even_sample_kernel(proj_ref, sm_ref, conv_ref, gdn_ref, mc_ref, mn_ref, mm_ref,
                        sp_ref, cw_ref, gnw_ref, mnw_ref,
                        y_ref, conv_o, gdn_o, mc_o, mn_o, mm_o, act_ref):
    nb = DEC_BB
    gnw = gnw_ref[...]
    mnw = mnw_ref[...]
    rows = lax.broadcasted_iota(jnp.int32, (nb, LANE), 0)

    sm = sm_ref[...]
    beta = _sigmoid(sm[:, 0:128])
    gam = jnp.exp(-jnp.exp(sp_ref[1:2, :]) * _softplus(sm[:, 128:256] + sp_ref[0:1, :]))
    ig = sm[:, 256:384] + sp_ref[2:3, :]
    lf = _log_sigmoid(sm[:, 384:512] + sp_ref[3:4, :])
    w0 = lf + mm_ref[...]
    m_t = jnp.maximum(w0, ig)
    s0 = jnp.exp(w0 - m_t)
    we = jnp.exp(ig - m_t)
    floor = jnp.exp(-m_t)
    mm_o[...] = m_t

    u = proj_ref[:, E_U:E_U + CONV_CH].astype(F32)
    cw = cw_ref[...]
    uc = (cw[0:1] * conv_ref[0] + cw[1:2] * conv_ref[1] + cw[2:3] * conv_ref[2] + cw[3:4] * u)
    conv_o[0] = conv_ref[1]
    conv_o[1] = conv_ref[2]
    conv_o[2] = u
    act_ref[...] = _silu(uc)

    for h in range(H_A):
        q = _l2n(act_ref[:, h * DK_A:(h + 1) * DK_A]) * (DK_A ** -0.5)
        k = _l2n(act_ref[:, H_A * DK_A + h * DK_A:H_A * DK_A + (h + 1) * DK_A])
        v = act_ref[:, 2 * H_A * DK_A + h * DV_A:2 * H_A * DK_A + (h + 1) * DV_A]
        kq = jnp.concatenate([k, q], axis=0).astype(BF16)
        kt = _transpose_bf16(k)
        ks = jnp.zeros((nb, DV_A), F32)
        qs = jnp.zeros((nb, DV_A), F32)
        for b in range(nb):
            r = _dot(kq, gdn_ref[b, h].astype(BF16))
            ks = _row_select(rows, b, ks, r[0:nb])
            qs = _row_select(rows, b, qs, r[nb:2 * nb])
        gam_h = gam[:, h:h + 1]
        uu = beta[:, h:h + 1] * (v - gam_h * ks)
        o = gam_h * qs + jnp.sum(q * k, axis=-1, keepdims=True) * uu
        for b in range(nb):
            usel = _pad_rows16(jnp.where(rows == b, uu, 0.0))
            gdn_o[b, h] = gam[b:b + 1, h:h + 1] * gdn_ref[b, h] + _dot(kt, usel)
        z = proj_ref[:, E_Z + h * DV_A:E_Z + (h + 1) * DV_A].astype(F32)
        y_ref[:, h * DV_A:(h + 1) * DV_A] = _rms_rows(o, gnw) * _silu(z)

    for h in range(H_B):
        qk_in = proj_ref[:, E_QK + h * 128:E_QK + (h + 1) * 128]
        qb = qk_in[:, 0:DK_B]
        q = qb.astype(F32)
        k = qk_in[:, DK_B:2 * DK_B].astype(F32) * (DK_B ** -0.5)
        v = proj_ref[:, E_V + h * DV_B:E_V + (h + 1) * DV_B].astype(F32)
        kt = _transpose_bf16(k)
        q16 = _pad_rows16(qb)
        qc = jnp.zeros((nb, DV_B), F32)
        for b in range(nb):
            r = _dot(q16, mc_ref[b, h].astype(BF16))
            qc = _row_select(rows, b, qc, r[0:nb])
        s0_h = s0[:, h:h + 1]
        we_h = we[:, h:h + 1]
        wv = we_h * v
        nn = s0_h * mn_ref[h] + we_h * k
        mn_o[h] = nn
        num = s0_h * qc + jnp.sum(q * k, axis=-1, keepdims=True) * wv
        den = jnp.sum(q * nn, axis=-1, keepdims=True)
        hh = num / jnp.maximum(jnp.abs(den), floor[:, h:h + 1])
        for b in range(nb):
            wsel = _pad_rows16(jnp.where(rows == b, wv, 0.0))
            mc_o[b, h] = s0[b:b + 1, h:h + 1] * mc_ref[b, h] + _dot(kt, wsel)
        o_pre = proj_ref[:, E_OP + h * DV_B:E_OP + (h + 1) * DV_B].astype(F32)
        z = proj_ref[:, E_Z + H_A * DV_A + h * DV_B:
                     E_Z + H_A * DV_A + (h + 1) * DV_B].astype(F32)
        y_ref[:, H_A * DV_A + h * DV_B:H_A * DV_A + (h + 1) * DV_B] = (
            _sigmoid(o_pre) * _rms_rows(hh, mnw) * _silu(z))


def _even_sample(proj, small, conv, gdn, mc, mn, mm, sp, cw, gnw, mnw):
    batch = proj.shape[0]
    nb = DEC_BB
    r2 = lambda i: (i, 0)
    m3 = lambda i: (0, i, 0)
    i4 = lambda i: (i, 0, 0, 0)
    c2 = lambda i: (0, 0)
    return pl.pallas_call(
        _even_sample_kernel,
        grid=(batch // nb,),
        in_specs=[
            pl.BlockSpec((nb, P_E), r2),
            pl.BlockSpec((nb, SMALL_E), r2),
            pl.BlockSpec((CONV_W - 1, nb, CONV_CH), m3),
            pl.BlockSpec((nb, H_A, DK_A, DV_A), i4),
            pl.BlockSpec((nb, H_B, DK_B, DV_B), i4),
            pl.BlockSpec((H_B, nb, DK_B), m3),
            pl.BlockSpec((nb, LANE), r2),
            pl.BlockSpec((8, LANE), c2),
            pl.BlockSpec((CONV_W, CONV_CH), c2),
            pl.BlockSpec((1, DV_A), c2),
            pl.BlockSpec((1, DV_B), c2),
        ],
        out_specs=[
            pl.BlockSpec((nb, W_MIX), r2),
            pl.BlockSpec((CONV_W - 1, nb, CONV_CH), m3),
            pl.BlockSpec((nb, H_A, DK_A, DV_A), i4),
            pl.BlockSpec((nb, H_B, DK_B, DV_B), i4),
            pl.BlockSpec((H_B, nb, DK_B), m3),
            pl.BlockSpec((nb, LANE), r2),
        ],
        out_shape=[
            jax.ShapeDtypeStruct((batch, W_MIX), F32),
            jax.ShapeDtypeStruct((CONV_W - 1, batch, CONV_CH), F32),
            jax.ShapeDtypeStruct((batch, H_A, DK_A, DV_A), F32),
            jax.ShapeDtypeStruct((batch, H_B, DK_B, DV_B), F32),
            jax.ShapeDtypeStruct((H_B, batch, DK_B), F32),
            jax.ShapeDtypeStruct((batch, LANE), F32),
        ],
        scratch_shapes=[pltpu.VMEM((nb, CONV_CH), F32)],
        compiler_params=pltpu.CompilerParams(
            dimension_semantics=("arbitrary",), vmem_limit_bytes=VMEM_LIMIT),
        name="even_sample",
    )(proj, small, conv, gdn, mc, mn, mm, sp, cw, gnw, mnw)


def _odd_sample_kernel(proj_ref, glr_ref, w2_ref, b2_ref, gla_ref, ret_ref, cc_ref, ss_ref,
                       cnw_ref, dnw_ref, y_ref, gla_o, ret_o):
    nb = DEC_BB
    cnw = cnw_ref[...]
    dnw = dnw_ref[...]
    cc = cc_ref[...]
    ss = ss_ref[...]
    rows = lax.broadcasted_iota(jnp.int32, (nb, DV_C), 0)
    g = _log_sigmoid(_dot(glr_ref[...].astype(BF16), w2_ref[...]) + b2_ref[...]) / GLA_TAU

    for h in range(H_C):
        q = proj_ref[:, O_QC + h * DK_C:O_QC + (h + 1) * DK_C].astype(F32) * (DK_C ** -0.5)
        k = proj_ref[:, O_KC + h * DK_C:O_KC + (h + 1) * DK_C]
        v = proj_ref[:, O_VC + h * DV_C:O_VC + (h + 1) * DV_C].astype(F32)
        eg = jnp.exp(g[:, h * DK_C:(h + 1) * DK_C])
        qe = _pad_rows16(q * eg)
        kt = _transpose_bf16(k)
        egt = _transpose(eg)
        qs = jnp.zeros((nb, DV_C), F32)
        for b in range(nb):
            qs = _row_select(rows, b, qs, _dot(qe, gla_ref[b, h].astype(BF16))[0:nb])
        o = qs + jnp.sum(q * k.astype(F32), axis=-1, keepdims=True) * v
        for b in range(nb):
            vsel = _pad_rows16(jnp.where(rows == b, v, 0.0))
            gla_o[b, h] = egt[:, b:b + 1] * gla_ref[b, h] + _dot(kt, vsel)
        z = proj_ref[:, O_Z + h * DV_C:O_Z + (h + 1) * DV_C].astype(F32)
        y_ref[:, h * DV_C:(h + 1) * DV_C] = _rms_rows(o, cnw) * _silu(z)

    for h in range(H_D):
        gamma = math.exp(_ret_log_gamma(h))
        qd = proj_ref[:, O_QD + h * DK_D:O_QD + (h + 1) * DK_D].astype(F32)
        kd = proj_ref[:, O_KD + h * DK_D:O_KD + (h + 1) * DK_D].astype(F32)
        qd = qd * cc + pltpu.roll(qd, DK_D // 2, 1) * ss
        kd = (kd * cc + pltpu.roll(kd, DK_D // 2, 1) * ss) * (DK_D ** -0.5)
        v = proj_ref[:, O_VD + h * DV_D:O_VD + (h + 1) * DV_D].astype(F32)
        qb = _pad_rows16(qd)
        kt = _transpose_bf16(kd)
        qs = jnp.zeros((nb, DV_D), F32)
        for b in range(nb):
            qs = _row_select(rows, b, qs, _dot(qb, ret_ref[b, h].astype(BF16))[0:nb])
        o = gamma * qs + jnp.sum(qd * kd, axis=-1, keepdims=True) * v
        for b in range(nb):
            vsel = _pad_rows16(jnp.where(rows == b, v, 0.0))
            ret_o[b, h] = gamma * ret_ref[b, h] + _dot(kt, vsel)
        z = proj_ref[:, O_Z + H_C * DV_C + h * DV_D:
                     O_Z + H_C * DV_C + (h + 1) * DV_D].astype(F32)
        y_ref[:, H_C * DV_C + h * DV_D:H_C * DV_C + (h + 1) * DV_D] = (
            _rms_rows(o, dnw) * _silu(z))


def _odd_sample(proj, glr, w2, b2, gla, ret, cc, ss, cnw, dnw):
    batch = proj.shape[0]
    nb = DEC_BB
    r2 = lambda i: (i, 0)
    i4 = lambda i: (i, 0, 0, 0)
    c2 = lambda i: (0, 0)
    return pl.pallas_call(
        _odd_sample_kernel,
        grid=(batch // nb,),
        in_specs=[
            pl.BlockSpec((nb, P_O), r2),
            pl.BlockSpec((nb, SMALL_O), r2),
            pl.BlockSpec((LANE, H_C * DK_C), c2),
            pl.BlockSpec((1, H_C * DK_C), c2),
            pl.BlockSpec((nb, H_C, DK_C, DV_C), i4),
            pl.BlockSpec((nb, H_D, DK_D, DV_D), i4),
            pl.BlockSpec((1, DK_D), c2),
            pl.BlockSpec((1, DK_D), c2),
            pl.BlockSpec((1, DV_C), c2),
            pl.BlockSpec((1, DV_D), c2),
        ],
        out_specs=[
            pl.BlockSpec((nb, W_MIX), r2),
            pl.BlockSpec((nb, H_C, DK_C, DV_C), i4),
            pl.BlockSpec((nb, H_D, DK_D, DV_D), i4),
        ],
        out_shape=[
            jax.ShapeDtypeStruct((batch, W_MIX), F32),
            jax.ShapeDtypeStruct((batch, H_C, DK_C, DV_C), F32),
            jax.ShapeDtypeStruct((batch, H_D, DK_D, DV_D), F32),
        ],
        compiler_params=pltpu.CompilerParams(
            dimension_semantics=("arbitrary",), vmem_limit_bytes=VMEM_LIMIT),
        name="odd_sample",
    )(proj, glr, w2, b2, gla, ret, cc, ss, cnw, dnw)


def _pad_cols(w, width):
    return jnp.pad(w, ((0, 0), (0, width - w.shape[1])))


def _prep_even_w_in(w):
    u = w[:, 0:3072]
    beta = w[:, 3072:3080]
    a = w[:, 3080:3088]
    qb = w[:, 3088:3600].reshape(-1, H_B, DK_B)
    kb = w[:, 3600:4112].reshape(-1, H_B, DK_B)
    vb = w[:, 4112:5136]
    ig = w[:, 5136:5144]
    fg = w[:, 5144:5152]
    op = w[:, 5152:6176]
    z = w[:, 6176:8224]
    qk = jnp.concatenate([qb, kb], axis=-1).reshape(-1, H_B * 2 * DK_B)
    small = jnp.concatenate([_pad_cols(beta, 128), _pad_cols(a, 128), _pad_cols(ig, 128),
                             _pad_cols(fg, 128)], axis=1)
    return jnp.concatenate([u, qk, vb, op, z], axis=1).astype(BF16), small.astype(BF16)


def _prep_odd_w_in(w):
    qc = w[:, 0:512]
    kc = w[:, 512:1024]
    vc = w[:, 1024:2048]
    glr = w[:, 2048:2064]
    qd = w[:, 2064:2576]
    kd = w[:, 2576:3088]
    vd = w[:, 3088:4112]
    z = w[:, 4112:6160]
    return (jnp.concatenate([qc, kc, vc, qd, kd, vd, z], axis=1).astype(BF16),
            _pad_cols(glr, SMALL_O).astype(BF16))


def _lane_row(vec, width=LANE):
    return jnp.pad(vec.astype(F32), (0, width - vec.shape[0]))


def _rotary_tables(pos):
    half = DK_D // 2
    inv = ROPE_BASE ** (-jnp.arange(half, dtype=F32) / half)
    ang = pos.astype(F32)[:, None] * inv[None, :]
    cos = jnp.cos(ang)
    sin = jnp.sin(ang)
    return jnp.concatenate([cos, cos], axis=-1), jnp.concatenate([-sin, sin], axis=-1)


def kernel(x_prompt, x_sample, c_prompt, c_sample, state_gdn, state_gdn_conv, state_mlstm_c,
           state_mlstm_n, state_mlstm_m, state_gla, state_ret, ada_w, ada_b, norm_w, ev_w_in,
           ev_w_out, gdn_conv_w, gdn_a_log, gdn_dt_bias, gdn_norm_w, mlstm_gate_b, mlstm_norm_w,
           od_w_in, od_w_out, gla_w2, gla_b2, gla_norm_w, ret_norm_w, final_norm_w):
    bp, tp, d = x_prompt.shape
    bs = x_sample.shape[0]
    n_p = bp * tp

    w_in_e, w_sm_e = _prep_even_w_in(ev_w_in[0])
    w_in_o, w_sm_o = _prep_odd_w_in(od_w_in[0])
    w_out_e = ev_w_out[0].astype(BF16)
    w_out_o = od_w_out[0].astype(BF16)
    sp = jnp.stack([_lane_row(gdn_dt_bias[0]), _lane_row(gdn_a_log[0]),
                    _lane_row(mlstm_gate_b[0, :H_B]), _lane_row(mlstm_gate_b[0, H_B:])]
                   + [jnp.zeros((LANE,), F32)] * 4)
    gnw = gdn_norm_w[0].reshape(1, DV_A)
    mnw = mlstm_norm_w[0].reshape(1, DV_B)
    cnw = gla_norm_w[0].reshape(1, DV_C)
    dnw = ret_norm_w[0].reshape(1, DV_D)
    w2 = jnp.pad(gla_w2[0], ((0, LANE - GLA_RANK), (0, 0))).astype(BF16)
    b2 = gla_b2[0].reshape(1, H_C * DK_C)

    mod = _modulation(jnp.concatenate([c_prompt, c_sample], axis=0), ada_w, ada_b)
    mod_p = [mod[l, :bp].reshape(bp, 1, 3 * d) for l in range(2)]
    mod_s = [mod[l, bp:].reshape(1, bs, 3 * d) for l in range(2)]

    xp = x_prompt.reshape(n_p, d)
    proj, small = _inproj(xp, mod_p[0], norm_w[0], w_in_e, w_sm_e, 1024, 2048)
    y, p_gdn, p_conv, p_mc, p_mn, p_mm = _even_prompt(
        proj.reshape(bp, tp, P_E), small.reshape(bp, tp, SMALL_E), sp, gdn_conv_w[0], gnw, mnw,
        bp, tp)
    xp = _outproj(y.reshape(n_p, W_MIX), xp, mod_p[0], w_out_e, final_norm_w, 512, False)
    proj, small = _inproj(xp, mod_p[1], norm_w[1], w_in_o, w_sm_o, 1024, 2048)
    cc_p, ss_p = _rotary_tables(jnp.arange(tp))
    y, p_gla, p_ret = _odd_prompt(proj, small, w2, b2, cc_p, ss_p, cnw, dnw, bp, tp)
    y_prompt = _outproj(y, xp, mod_p[1], w_out_o, final_norm_w, 512, True).reshape(bp, tp, d)

    xs = x_sample.reshape(bs, d)
    proj, small = _inproj(xs, mod_s[0], norm_w[0], w_in_e, w_sm_e, bs, 2048)
    mm_in = jnp.pad(state_mlstm_m[0], ((0, 0), (0, LANE - H_B)))
    y, s_conv, s_gdn, s_mc, s_mn, s_mm = _even_sample(
        proj, small, jnp.transpose(state_gdn_conv[0], (1, 0, 2)), state_gdn[0], state_mlstm_c[0],
        jnp.transpose(state_mlstm_n[0], (1, 0, 2)), mm_in, sp, gdn_conv_w[0], gnw, mnw)
    xs = _outproj(y, xs, mod_s[0], w_out_e, final_norm_w, bs, False)
    proj, small = _inproj(xs, mod_s[1], norm_w[1], w_in_o, w_sm_o, bs, 2048)
    cc_s, ss_s = _rotary_tables(PAST_LEN + jnp.arange(1))
    y, s_gla, s_ret = _odd_sample(proj, small, w2, b2, state_gla[0], state_ret[0], cc_s, ss_s,
                                  cnw, dnw)
    y_sample = _outproj(y, xs, mod_s[1], w_out_o, final_norm_w, bs, True)
    y_sample = y_sample.reshape(bs, 1, d)

    return (y_prompt, y_sample,
            p_gdn[None], p_conv[None], p_mc[None], p_mn[None], p_mm[None, :, 0, :H_B],
            p_gla[None], p_ret[None],
            s_gdn[None], jnp.transpose(s_conv, (1, 0, 2))[None], s_mc[None],
            jnp.transpose(s_mn, (1, 0, 2))[None], s_mm[None, :, :H_B], s_gla[None], s_ret[None])
```

```python
import functools
import math

import jax
import jax.numpy as jnp
import numpy as np
from jax import lax
from jax.experimental import pallas as pl
from jax.experimental.pallas import tpu as pltpu

F32 = jnp.float32
BF16 = jnp.bfloat16
HI = lax.Precision.HIGHEST

D_MODEL = 1024
W_MIX = 2048
H_A, DK_A, DV_A, CONV_W = 8, 128, 128, 4
CONV_CH = H_A * (2 * DK_A + DV_A)
H_B, DK_B, DV_B = 8, 64, 128
H_C, DK_C, DV_C, GLA_RANK, GLA_TAU = 4, 128, 256, 16, 16.0
H_D, DK_D, DV_D = 4, 128, 256
ROPE_BASE = 10000.0
PAST_LEN = 16384
EPS = 1e-6
LANE = 128

CHUNK = 64
CHUNKS_PER_STEP = 4

E_U, E_QK, E_V, E_OP, E_Z = 0, 3072, 4096, 5120, 6144
P_E = 8192
SMALL_E = 512
O_QC, O_KC, O_VC, O_QD, O_KD, O_VD, O_Z = 0, 512, 1024, 2048, 2560, 3072, 4096
P_O = 6144
SMALL_O = 128

VMEM_LIMIT = 56 * 1024 * 1024


def _sigmoid(x):
    return 0.5 * jnp.tanh(0.5 * x) + 0.5


def _softplus(x):
    return jnp.maximum(x, 0.0) + jnp.log(1.0 + jnp.exp(-jnp.abs(x)))


def _log_sigmoid(x):
    return -_softplus(-x)


def _silu(x):
    half = 0.5 * x
    return half + half * jnp.tanh(half)


def _rms_rows(x, w):
    return x * lax.rsqrt(jnp.mean(x * x, axis=-1, keepdims=True) + EPS) * w


def _l2n(x):
    return x * lax.rsqrt(jnp.sum(x * x, axis=-1, keepdims=True) + EPS)


def _dot(a, b, precision=None):
    return jnp.dot(a, b, preferred_element_type=F32, precision=precision)


def _dot_nt(a, b, precision=None):
    return lax.dot_general(a, b, (((1,), (1,)), ((), ())), preferred_element_type=F32,
                           precision=precision)


def _dot_tn(a, b, precision=None):
    return lax.dot_general(a, b, (((0,), (0,)), ((), ())), preferred_element_type=F32,
                           precision=precision)


def _eye(n):
    r = lax.broadcasted_iota(jnp.int32, (n, n), 0)
    c = lax.broadcasted_iota(jnp.int32, (n, n), 1)
    return (r == c).astype(F32)


def _bf16_terms(x, terms):
    parts = []
    rest = x
    for _ in range(terms):
        piece = rest.astype(BF16)
        parts.append(piece)
        rest = rest - piece.astype(F32)
    return parts


def _select_dot(sel, x, terms):
    n = x.shape[1]
    out = _dot(sel.astype(BF16), jnp.concatenate(_bf16_terms(x, terms), axis=1))
    acc = out[:, 0:n]
    for i in range(1, terms):
        acc = acc + out[:, i * n:(i + 1) * n]
    return acc


def _transpose(x):
    return _dot_nt(_eye(x.shape[1]), x, precision=HI)


def _tri_masks(n):
    r = lax.broadcasted_iota(jnp.int32, (n, n), 0)
    c = lax.broadcasted_iota(jnp.int32, (n, n), 1)
    return r >= c, r > c


def _mod_kernel(c_ref, w_ref, b_ref, o_ref):
    cs = _silu(c_ref[...]).astype(BF16)
    o_ref[...] = _dot(cs, w_ref[...].astype(BF16)) + b_ref[...]


def _modulation(c_all, ada_w, ada_b):
    depth, d, d3 = ada_w.shape
    rows = c_all.shape[0]
    tn = 1024
    return pl.pallas_call(
        _mod_kernel,
        grid=(depth, d3 // tn),
        in_specs=[
            pl.BlockSpec((rows, d), lambda l, j: (0, 0)),
            pl.BlockSpec((None, d, tn), lambda l, j: (l, 0, j)),
            pl.BlockSpec((None, 1, tn), lambda l, j: (l, 0, j)),
        ],
        out_specs=pl.BlockSpec((None, rows, tn), lambda l, j: (l, 0, j)),
        out_shape=jax.ShapeDtypeStruct((depth, rows, d3), F32),
        compiler_params=pltpu.CompilerParams(
            dimension_semantics=("arbitrary", "arbitrary"), vmem_limit_bytes=VMEM_LIMIT),
        name="modulation",
    )(c_all, ada_w, ada_b.reshape(depth, 1, d3))


def _inproj_kernel(x_ref, mod_ref, nw_ref, w_ref, ws_ref, o_ref, os_ref, h_ref):
    @pl.when(pl.program_id(1) == 0)
    def _():
        x = x_ref[...]
        y = _rms_rows(x, nw_ref[...])
        shift = mod_ref[:, 0:D_MODEL]
        scale = mod_ref[:, D_MODEL:2 * D_MODEL]
        h_ref[...] = (y * (1.0 + scale) + shift).astype(BF16)
        os_ref[...] = _dot(h_ref[...], ws_ref[...])

    o_ref[...] = _dot(h_ref[...], w_ref[...]).astype(o_ref.dtype)


def _inproj(x, mod, norm_w, w, w_small, tm, tn):
    n, d = x.shape
    p = w.shape[1]
    ps = w_small.shape[1]
    g, r, _ = mod.shape
    tiles_per_group = (n // g) // tm
    return pl.pallas_call(
        _inproj_kernel,
        grid=(n // tm, p // tn),
        in_specs=[
            pl.BlockSpec((tm, d), lambda i, j: (i, 0)),
            pl.BlockSpec((None, r, 3 * d), lambda i, j: (i // tiles_per_group, 0, 0)),
            pl.BlockSpec((1, d), lambda i, j: (0, 0)),
            pl.BlockSpec((d, tn), lambda i, j: (0, j)),
            pl.BlockSpec((d, ps), lambda i, j: (0, 0)),
        ],
        out_specs=[pl.BlockSpec((tm, tn), lambda i, j: (i, j)),
                   pl.BlockSpec((tm, ps), lambda i, j: (i, 0))],
        out_shape=[jax.ShapeDtypeStruct((n, p), BF16), jax.ShapeDtypeStruct((n, ps), F32)],
        scratch_shapes=[pltpu.VMEM((tm, d), BF16)],
        compiler_params=pltpu.CompilerParams(
            dimension_semantics=("arbitrary", "arbitrary"), vmem_limit_bytes=VMEM_LIMIT,
            allow_input_fusion=[False, False, False, True, True]),
        name="inproj",
    )(x, mod, norm_w.reshape(1, d), w, w_small)


def _outproj_kernel(y_ref, x_ref, mod_ref, w_ref, fw_ref, o_ref, *, final):
    acc = _dot(y_ref[...].astype(BF16), w_ref[...])
    gate = mod_ref[:, 2 * D_MODEL:3 * D_MODEL]
    xn = x_ref[...] + gate * acc
    if final:
        xn = _rms_rows(xn, fw_ref[...])
    o_ref[...] = xn


def _outproj(y, x, mod, w, final_w, tm, final):
    n, d = x.shape
    k = y.shape[1]
    g, r, _ = mod.shape
    tiles_per_group = (n // g) // tm
    return pl.pallas_call(
        functools.partial(_outproj_kernel, final=final),
        grid=(n // tm,),
        in_specs=[
            pl.BlockSpec((tm, k), lambda i: (i, 0)),
            pl.BlockSpec((tm, d), lambda i: (i, 0)),
            pl.BlockSpec((None, r, 3 * d), lambda i: (i // tiles_per_group, 0, 0)),
            pl.BlockSpec((k, d), lambda i: (0, 0)),
            pl.BlockSpec((1, d), lambda i: (0, 0)),
        ],
        out_specs=pl.BlockSpec((tm, d), lambda i: (i, 0)),
        out_shape=jax.ShapeDtypeStruct((n, d), F32),
        compiler_params=pltpu.CompilerParams(
            dimension_semantics=("arbitrary",), vmem_limit_bytes=VMEM_LIMIT),
        name="outproj",
    )(y, x, mod, w, final_w.reshape(1, d))


def _neumann_correction(a_list, n):
    ys = [-a for a in a_list]
    pbs = [a.astype(BF16) for a in a_list]
    for _ in range(int(math.log2(n)) - 2):
        for i in range(len(ys)):
            p = _dot(pbs[i], pbs[i])
            pbs[i] = p.astype(BF16)
            ys[i] = ys[i] + p + _dot(ys[i].astype(BF16), pbs[i])
        yield
    idx = range(len(ys))
    a_h = [a.astype(BF16) for a in a_list]
    a_l = [(a_list[i] - a_h[i].astype(F32)).astype(BF16) for i in idx]
    y_h = [y.astype(BF16) for y in ys]
    y_l = [(ys[i] - y_h[i].astype(F32)).astype(BF16) for i in idx]
    ay = [_dot(a_h[i], y_h[i]) + _dot(a_h[i], y_l[i]) + _dot(a_l[i], y_h[i]) for i in idx]
    e = [-(a_list[i] + ys[i] + ay[i]) for i in idx]
    return [ys[i] + e[i] + _dot(y_h[i], e[i].astype(BF16)) for i in idx]


def _run_staggered(stage_generators, offset):
    pending = list(stage_generators)
    active = []
    tick = 0
    while pending or active:
        if pending and tick % offset == 0:
            active.append(pending.pop(0))
        for g in list(active):
            try:
                next(g)
            except StopIteration:
                active.remove(g)
        tick += 1


EVEN_STAGGER = 13
ODD_STAGGER = 9


def _conv_shift_matrix():
    n = CHUNK
    r = np.arange((CONV_W - 1) * n)[:, None]
    col = np.arange(2 * n)[None, :]
    return jnp.asarray(col == n + (r % n) - (r // n + 1), BF16)


def _even_prompt_kernel(proj_ref, sm_ref, sp_ref, cw_ref, shift_ref, gnw_ref, mnw_ref,
                        y_ref, gdn_ref, conv_ref, mc_ref, mn_ref, mm_ref,
                        uprev_ref, qkv_ref):
    chunks = []
    for j in range(CHUNKS_PER_STEP):
        rows = pl.ds(j * CHUNK, CHUNK)
        chunks.append(_even_prompt_sequence(
            j == 0, proj_ref.at[0, rows], sm_ref.at[0, rows], sp_ref, cw_ref, shift_ref, gnw_ref,
            mnw_ref, y_ref.at[0, rows], gdn_ref.at[0], conv_ref.at[0], mc_ref.at[0], mn_ref.at[0],
            mm_ref.at[0], uprev_ref.at[0], qkv_ref.at[0]))
    _run_staggered(chunks, EVEN_STAGGER)


def _even_prompt_sequence(first_in_step, proj_ref, sm_ref, sp_ref, cw_ref, shift_ref, gnw_ref,
                          mnw_ref, y_ref, gdn_ref, conv_ref, mc_ref, mn_ref, mm_ref,
                          uprev_ref, qkv_ref):
    L = CHUNK

    if first_in_step:
        @pl.when(pl.program_id(1) == 0)
        def _():
            gdn_ref[...] = jnp.zeros_like(gdn_ref)
            mc_ref[...] = jnp.zeros_like(mc_ref)
            mn_ref[...] = jnp.zeros_like(mn_ref)
            mm_ref[...] = jnp.zeros_like(mm_ref)
            uprev_ref[...] = jnp.zeros_like(uprev_ref)

    tri_incl, tri_strict = _tri_masks(L)
    tri_f = tri_incl.astype(F32)

    ub = proj_ref[:, E_U:E_U + CONV_CH]
    shifted = _dot(shift_ref[...], jnp.concatenate([uprev_ref[...], ub], axis=0))
    uprev_ref[...] = ub
    u = ub.astype(F32)
    cw = cw_ref[...]
    uc = (cw[0:1] * shifted[2 * L:3 * L] + cw[1:2] * shifted[L:2 * L]
          + cw[2:3] * shifted[0:L] + cw[3:4] * u)
    conv_ref[...] = u[L - 3:L, :]
    qkv_ref[...] = _silu(uc)
    yield

    sm = sm_ref[...]
    head_lane = lax.broadcasted_iota(jnp.int32, (L, LANE), 1) < H_A
    beta = _sigmoid(sm[:, 0:128])
    g = -jnp.exp(sp_ref[1:2, :]) * _softplus(sm[:, 128:256] + sp_ref[0:1, :])
    ig = sm[:, 256:384] + sp_ref[2:3, :]
    lf = _log_sigmoid(sm[:, 384:512] + sp_ref[3:4, :])
    packed = jnp.where(head_lane, g, 0.0) + pltpu.roll(jnp.where(head_lane, lf, 0.0), H_A, 1)
    cum = _dot(tri_f, packed, precision=HI)
    cum_t = _transpose(cum + pltpu.roll(jnp.where(head_lane, ig, 0.0), H_A + H_B, 1))

    gnw = gnw_ref[...]
    mnw = mnw_ref[...]
    yield

    qkb, kf, decay, beta_c, gam, bh_c = [], [], [], [], [], []
    for h in range(H_A):
        q = _l2n(qkv_ref[:, h * DK_A:(h + 1) * DK_A]) * (DK_A ** -0.5)
        k = _l2n(qkv_ref[:, H_A * DK_A + h * DK_A:H_A * DK_A + (h + 1) * DK_A])
        kf.append(k)
        qkb.append(jnp.concatenate([q, k], axis=0).astype(BF16))
    yield
    qkk = [_dot_nt(qkb[h], qkb[h][L:2 * L]) for h in range(H_A)]
    yield
    a_list = []
    for h in range(H_A):
        bh_c.append(cum[:, h:h + 1])
        diff = bh_c[h] - cum_t[h:h + 1, :]
        decay.append(jnp.where(tri_incl, jnp.exp(jnp.where(tri_incl, diff, 0.0)), 0.0))
        beta_c.append(beta[:, h:h + 1])
        gam.append(jnp.exp(bh_c[h]))
        a_list.append(jnp.where(tri_strict, decay[h] * qkk[h][L:2 * L], 0.0) * beta_c[h])
    yield
    corr = yield from _neumann_correction(a_list, L)
    yield

    heads = range(H_A)
    s_old = [gdn_ref[h] for h in heads]
    qks = [_dot(qkb[h], s_old[h].astype(BF16)) for h in heads]
    yield
    rhs = [beta_c[h] * (qkv_ref[:, 2 * H_A * DK_A + h * DV_A:2 * H_A * DK_A + (h + 1) * DV_A]
                        - gam[h] * qks[h][L:2 * L]) for h in heads]
    ub = [(rhs[h] + _dot(corr[h].astype(BF16), rhs[h].astype(BF16))).astype(BF16) for h in heads]
    yield
    o = [gam[h] * qks[h][0:L] + _dot((qkk[h][0:L] * decay[h]).astype(BF16), ub[h])
         for h in heads]
    yield
    for h in heads:
        last = bh_c[h][L - 1:L, :]
        kw = (kf[h] * jnp.exp(last - bh_c[h])).astype(BF16)
        gdn_ref[h] = jnp.exp(last) * s_old[h] + _dot_tn(kw, ub[h])
    yield
    for h in heads:
        z = proj_ref[:, E_Z + h * DV_A:E_Z + (h + 1) * DV_A].astype(F32)
        y_ref[:, h * DV_A:(h + 1) * DV_A] = (_rms_rows(o[h], gnw) * _silu(z)).astype(BF16)
    yield

    heads = range(H_B)
    qs, qbs, kbs, vbs = [], [], [], []
    for h in heads:
        qk_in = proj_ref[:, E_QK + h * 128:E_QK + (h + 1) * 128]
        qbs.append(qk_in[:, 0:DK_B])
        qs.append(qbs[h].astype(F32))
        kbs.append((qk_in[:, DK_B:2 * DK_B].astype(F32) * (DK_B ** -0.5)).astype(BF16))
        vbs.append(proj_ref[:, E_V + h * DV_B:E_V + (h + 1) * DV_B])
    qkm = [_dot_nt(qbs[h], kbs[h]) for h in heads]
    yield
    b_c = [cum[:, H_A + h:H_A + h + 1] for h in heads]
    m0 = [mm_ref[0:1, h:h + 1] for h in heads]
    dmat = [jnp.where(tri_incl, b_c[h] - cum_t[H_A + h:H_A + h + 1, :]
                      + cum_t[H_A + H_B + h:H_A + H_B + h + 1, :], -jnp.inf) for h in heads]
    w0 = [b_c[h] + m0[h] for h in heads]
    m_t = [jnp.maximum(w0[h], jnp.max(dmat[h], axis=-1, keepdims=True)) for h in heads]
    p = [jnp.exp(dmat[h] - m_t[h]) * qkm[h] for h in heads]
    s0 = [jnp.exp(w0[h] - m_t[h]) for h in heads]
    yield
    c_old = [mc_ref[h] for h in heads]
    n_old = [mn_ref[h:h + 1, :] for h in heads]
    num = [s0[h] * _dot(qbs[h], c_old[h].astype(BF16)) + _dot(p[h].astype(BF16), vbs[h])
           for h in heads]
    den = [s0[h] * jnp.sum(qs[h] * n_old[h], axis=-1, keepdims=True)
           + jnp.sum(p[h], axis=-1, keepdims=True) for h in heads]
    hh = [num[h] / jnp.maximum(jnp.abs(den[h]), jnp.exp(-m_t[h])) for h in heads]
    yield
    for h in heads:
        m_end = m_t[h][L - 1:L, :]
        b_last = b_c[h][L - 1:L, :]
        we = jnp.exp(b_last - b_c[h] + ig[:, h:h + 1] - m_end)
        se = jnp.exp(b_last + m0[h] - m_end)
        kw = kbs[h].astype(F32) * we
        mc_ref[h] = se * c_old[h] + _dot_tn(kw.astype(BF16), vbs[h])
        mn_ref[h:h + 1, :] = se * n_old[h] + jnp.sum(kw, axis=0, keepdims=True)
        mm_ref[0:1, h:h + 1] = m_end
    yield
    for h in heads:
        o_pre = proj_ref[:, E_OP + h * DV_B:E_OP + (h + 1) * DV_B].astype(F32)
        z = proj_ref[:, E_Z + H_A * DV_A + h * DV_B:
                     E_Z + H_A * DV_A + (h + 1) * DV_B].astype(F32)
        y_ref[:, H_A * DV_A + h * DV_B:H_A * DV_A + (h + 1) * DV_B] = (
            _sigmoid(o_pre) * _rms_rows(hh[h], mnw) * _silu(z)).astype(BF16)


def _even_prompt(proj, small, sp, conv_w, gnw, mnw, batch, seq):
    step = CHUNK * CHUNKS_PER_STEP
    nc = seq // step
    nb = 1
    row = lambda b, c: (b, c, 0)
    const2 = lambda b, c: (0, 0)
    st3 = lambda b, c: (b, 0, 0)
    st4 = lambda b, c: (b, 0, 0, 0)
    return pl.pallas_call(
        _even_prompt_kernel,
        grid=(batch // nb, nc),
        in_specs=[
            pl.BlockSpec((nb, step, P_E), row),
            pl.BlockSpec((nb, step, SMALL_E), row),
            pl.BlockSpec((8, LANE), const2),
            pl.BlockSpec((CONV_W, CONV_CH), const2),
            pl.BlockSpec(((CONV_W - 1) * CHUNK, 2 * CHUNK), const2),
            pl.BlockSpec((1, DV_A), const2),
            pl.BlockSpec((1, DV_B), const2),
        ],
        out_specs=[
            pl.BlockSpec((nb, step, W_MIX), row),
            pl.BlockSpec((nb, H_A, DK_A, DV_A), st4),
            pl.BlockSpec((nb, CONV_W - 1, CONV_CH), st3),
            pl.BlockSpec((nb, H_B, DK_B, DV_B), st4),
            pl.BlockSpec((nb, H_B, DK_B), st3),
            pl.BlockSpec((nb, 1, LANE), st3),
        ],
        out_shape=[
            jax.ShapeDtypeStruct((batch, seq, W_MIX), BF16),
            jax.ShapeDtypeStruct((batch, H_A, DK_A, DV_A), F32),
            jax.ShapeDtypeStruct((batch, CONV_W - 1, CONV_CH), F32),
            jax.ShapeDtypeStruct((batch, H_B, DK_B, DV_B), F32),
            jax.ShapeDtypeStruct((batch, H_B, DK_B), F32),
            jax.ShapeDtypeStruct((batch, 1, LANE), F32),
        ],
        scratch_shapes=[pltpu.VMEM((nb, CHUNK, CONV_CH), BF16),
                        pltpu.VMEM((nb, CHUNK, CONV_CH), F32)],
        compiler_params=pltpu.CompilerParams(
            dimension_semantics=("arbitrary", "arbitrary"), vmem_limit_bytes=VMEM_LIMIT),
        name="even_prompt",
    )(proj, small, sp, conv_w, _conv_shift_matrix(), gnw, mnw)


def _ret_log_gamma(h):
    return math.log(1.0 - 2.0 ** (-5.0 - h))


GLA_LEVELS = int(math.log2(CHUNK))
GLA_MM_LEVELS = (1, 2)
LOG2E = math.log2(math.e)


def _gla_chunk_tables():
    n = CHUNK
    t = np.arange(n)[:, None]
    i = np.arange(n)[None, :]
    blocks = [(i <= t)]
    for lvl in GLA_MM_LEVELS:
        hs = 1 << lvl
        p = (t & ~(2 * hs - 1)) + hs - 1
        upper = (t & hs) != 0
        blocks.append(np.where(upper, (i > p) & (i <= t), (i > t) & (i <= p)))
    sums = np.concatenate(blocks, axis=0).astype(np.float32)
    s = np.arange(n)[None, :]
    level = np.full((n, n), GLA_LEVELS + 1, np.int32)
    level[np.arange(n), np.arange(n)] = GLA_LEVELS
    for lvl in range(GLA_LEVELS):
        hs = 1 << lvl
        same = (t >> (lvl + 1)) == (s >> (lvl + 1))
        level[same & ((t & hs) != 0) & ((s & hs) == 0)] = lvl
    return jnp.asarray(sums), jnp.asarray(level)


def _odd_prompt_kernel(proj_ref, glr_ref, w2_ref, b2_ref, sums_ref, level_ref, cc_ref, ss_ref,
                       cnw_ref, dnw_ref, y_ref, gla_ref, ret_ref, x_ref):
    chunks = []
    for j in range(CHUNKS_PER_STEP):
        rows = pl.ds(j * CHUNK, CHUNK)
        chunks.append(_odd_prompt_chunk(
            j == 0, proj_ref.at[rows], glr_ref.at[rows], w2_ref, b2_ref, sums_ref, level_ref,
            cc_ref.at[rows], ss_ref.at[rows], cnw_ref, dnw_ref, y_ref.at[rows], gla_ref, ret_ref,
            x_ref))
    _run_staggered(chunks, ODD_STAGGER)


def _odd_prompt_chunk(first_in_step, proj_ref, glr_ref, w2_ref, b2_ref, sums_ref, level_ref,
                      cc_ref, ss_ref, cnw_ref, dnw_ref, y_ref, gla_ref, ret_ref, x_ref):
    L = CHUNK

    if first_in_step:
        @pl.when(pl.program_id(1) == 0)
        def _():
            gla_ref[...] = jnp.zeros_like(gla_ref)
            ret_ref[...] = jnp.zeros_like(ret_ref)

    tri_incl, _ = _tri_masks(L)
    r64 = lax.broadcasted_iota(jnp.int32, (L, L), 0)
    c64 = lax.broadcasted_iota(jnp.int32, (L, L), 1)
    level = level_ref[...]

    glr = glr_ref[...].astype(BF16)
    gc2 = _log_sigmoid(_dot(glr, w2_ref[...]) + b2_ref[...]) * (LOG2E / GLA_TAU)
    x_ref[...] = _select_dot(sums_ref[...], gc2, 3)
    odd_row = (lax.broadcasted_iota(jnp.int32, (L, DK_C), 0) & 1) == 1
    yield

    def level_exponent(lvl, h):
        cols = slice(h * DK_C, (h + 1) * DK_C)
        if lvl == 0:
            return jnp.where(odd_row, gc2[:, cols], 0.0)
        if lvl in GLA_MM_LEVELS:
            r0 = (1 + GLA_MM_LEVELS.index(lvl)) * L
            return x_ref[r0:r0 + L, cols]
        hs = 1 << lvl
        pieces = []
        for lo in range(0, L, 2 * hs):
            ref = x_ref[lo + hs - 1:lo + hs, cols]
            pieces.append(ref - x_ref[lo:lo + hs, cols])
            pieces.append(x_ref[lo + hs:lo + 2 * hs, cols] - ref)
        return jnp.concatenate(pieces, axis=0)

    cnw = cnw_ref[...]
    dnw = dnw_ref[...]

    heads = range(H_C)
    q = [proj_ref[:, O_QC + h * DK_C:O_QC + (h + 1) * DK_C].astype(F32) * (DK_C ** -0.5)
         for h in heads]
    k = [proj_ref[:, O_KC + h * DK_C:O_KC + (h + 1) * DK_C].astype(F32) for h in heads]
    v = [proj_ref[:, O_VC + h * DV_C:O_VC + (h + 1) * DV_C] for h in heads]
    att = [jnp.where(level == GLA_LEVELS, _dot_nt(q[h].astype(BF16), k[h].astype(BF16)), 0.0)
           for h in heads]
    yield
    for lvl in range(GLA_LEVELS):
        for h in heads:
            e = jnp.exp2(level_exponent(lvl, h))
            m = _dot_nt((q[h] * e).astype(BF16), (k[h] * e).astype(BF16))
            att[h] = jnp.where(level == lvl, m, att[h])
        yield
    b2 = [x_ref[0:L, h * DK_C:(h + 1) * DK_C] for h in heads]
    s_gla = [gla_ref[h] for h in heads]
    o = [_dot((q[h] * jnp.exp2(b2[h])).astype(BF16), s_gla[h].astype(BF16))
         + _dot(att[h].astype(BF16), v[h]) for h in heads]
    yield
    for h in heads:
        bl = b2[h][L - 1:L, :]
        kw = (k[h] * jnp.exp2(bl - b2[h])).astype(BF16)
        col = jnp.transpose(jnp.broadcast_to(jnp.exp2(bl), (DK_C, DK_C)))
        upd = _dot_tn(kw, v[h])
        for half in range(DV_C // DK_C):
            lanes = slice(half * DK_C, (half + 1) * DK_C)
            gla_ref[h, :, lanes] = s_gla[h][:, lanes] * col + upd[:, lanes]
    yield
    for h in heads:
        z = proj_ref[:, O_Z + h * DV_C:O_Z + (h + 1) * DV_C].astype(F32)
        y_ref[:, h * DV_C:(h + 1) * DV_C] = (_rms_rows(o[h], cnw) * _silu(z)).astype(BF16)
    yield

    cc = cc_ref[...]
    ss = ss_ref[...]
    rel = (r64 - c64).astype(F32)
    tcol = lax.broadcasted_iota(jnp.int32, (L, 1), 0).astype(F32)
    heads = range(H_D)
    lg = [_ret_log_gamma(h) for h in heads]
    qb, kf, kb, vd = [], [], [], []
    for h in heads:
        qd = proj_ref[:, O_QD + h * DK_D:O_QD + (h + 1) * DK_D].astype(F32)
        kd = proj_ref[:, O_KD + h * DK_D:O_KD + (h + 1) * DK_D].astype(F32)
        qb.append((qd * cc + pltpu.roll(qd, DK_D // 2, 1) * ss).astype(BF16))
        kf.append((kd * cc + pltpu.roll(kd, DK_D // 2, 1) * ss) * (DK_D ** -0.5))
        kb.append(kf[h].astype(BF16))
        vd.append(proj_ref[:, O_VD + h * DV_D:O_VD + (h + 1) * DV_D])
    yield
    s_old = [ret_ref[h] for h in heads]
    attd = [_dot_nt(qb[h], kb[h])
            * jnp.where(tri_incl, jnp.exp(lg[h] * jnp.maximum(rel, 0.0)), 0.0) for h in heads]
    od = [_dot(qb[h], s_old[h].astype(BF16)) * jnp.exp(lg[h] * (tcol + 1.0))
          + _dot(attd[h].astype(BF16), vd[h]) for h in heads]
    yield
    for h in heads:
        end = jnp.exp(lg[h] * (L - 1.0 - tcol))
        ret_ref[h] = math.exp(lg[h] * L) * s_old[h] + _dot_tn((kf[h] * end).astype(BF16), vd[h])
    yield
    for h in heads:
        z = proj_ref[:, O_Z + H_C * DV_C + h * DV_D:
                     O_Z + H_C * DV_C + (h + 1) * DV_D].astype(F32)
        y_ref[:, H_C * DV_C + h * DV_D:H_C * DV_C + (h + 1) * DV_D] = (
            _rms_rows(od[h], dnw) * _silu(z)).astype(BF16)


def _odd_prompt(proj, glr, w2, b2, cc, ss, cnw, dnw, batch, seq):
    step = CHUNK * CHUNKS_PER_STEP
    nc = seq // step
    row = lambda b, c: (b * nc + c, 0)
    const2 = lambda b, c: (0, 0)
    sums, level = _gla_chunk_tables()
    n_sums = sums.shape[0]
    return pl.pallas_call(
        _odd_prompt_kernel,
        grid=(batch, nc),
        in_specs=[
            pl.BlockSpec((step, P_O), row),
            pl.BlockSpec((step, SMALL_O), row),
            pl.BlockSpec((LANE, H_C * DK_C), const2),
            pl.BlockSpec((1, H_C * DK_C), const2),
            pl.BlockSpec((n_sums, CHUNK), const2),
            pl.BlockSpec((CHUNK, CHUNK), const2),
            pl.BlockSpec((step, DK_D), lambda b, c: (c, 0)),
            pl.BlockSpec((step, DK_D), lambda b, c: (c, 0)),
            pl.BlockSpec((1, DV_C), const2),
            pl.BlockSpec((1, DV_D), const2),
        ],
        out_specs=[
            pl.BlockSpec((step, W_MIX), row),
            pl.BlockSpec((None, H_C, DK_C, DV_C), lambda b, c: (b, 0, 0, 0)),
            pl.BlockSpec((None, H_D, DK_D, DV_D), lambda b, c: (b, 0, 0, 0)),
        ],
        out_shape=[
            jax.ShapeDtypeStruct((batch * seq, W_MIX), BF16),
            jax.ShapeDtypeStruct((batch, H_C, DK_C, DV_C), F32),
            jax.ShapeDtypeStruct((batch, H_D, DK_D, DV_D), F32),
        ],
        scratch_shapes=[pltpu.VMEM((n_sums, H_C * DK_C), F32)],
        compiler_params=pltpu.CompilerParams(
            dimension_semantics=("arbitrary", "arbitrary"), vmem_limit_bytes=VMEM_LIMIT),
        name="odd_prompt",
    )(proj, glr, w2, b2, sums, level, cc, ss, cnw, dnw)


DEC_BB = 8


def _row_select(rows, row_index, acc, new):
    return jnp.where(rows == row_index, new, acc)


def _pad_rows16(x):
    return jnp.concatenate([x, jnp.zeros_like(x)], axis=0).astype(BF16)


def _transpose_bf16(x):
    eye = _eye(x.shape[1]).astype(BF16)
    return _dot_nt(eye, _pad_rows16(x)).astype(BF16)


def _even_sample_kernel(proj_ref, sm_ref, conv_ref, gdn_ref, mc_ref, mn_ref, mm_ref,
                        sp_ref, cw_ref, gnw_ref, mnw_ref,
                        y_ref, conv_o, gdn_o, mc_o, mn_o, mm_o, act_ref):
    nb = DEC_BB
    gnw = gnw_ref[...]
    mnw = mnw_ref[...]
    rows = lax.broadcasted_iota(jnp.int32, (nb, LANE), 0)

    sm = sm_ref[...]
    beta = _sigmoid(sm[:, 0:128])
    gam = jnp.exp(-jnp.exp(sp_ref[1:2, :]) * _softplus(sm[:, 128:256] + sp_ref[0:1, :]))
    ig = sm[:, 256:384] + sp_ref[2:3, :]
    lf = _log_sigmoid(sm[:, 384:512] + sp_ref[3:4, :])
    w0 = lf + mm_ref[...]
    m_t = jnp.maximum(w0, ig)
    s0 = jnp.exp(w0 - m_t)
    we = jnp.exp(ig - m_t)
    floor = jnp.exp(-m_t)
    mm_o[...] = m_t

    u = proj_ref[:, E_U:E_U + CONV_CH].astype(F32)
    cw = cw_ref[...]
    uc = (cw[0:1] * conv_ref[0] + cw[1:2] * conv_ref[1] + cw[2:3] * conv_ref[2] + cw[3:4] * u)
    conv_o[0] = conv_ref[1]
    conv_o[1] = conv_ref[2]
    conv_o[2] = u
    act_ref[...] = _silu(uc)

    for h in range(H_A):
        q = _l2n(act_ref[:, h * DK_A:(h + 1) * DK_A]) * (DK_A ** -0.5)
        k = _l2n(act_ref[:, H_A * DK_A + h * DK_A:H_A * DK_A + (h + 1) * DK_A])
        v = act_ref[:, 2 * H_A * DK_A + h * DV_A:2 * H_A * DK_A + (h + 1) * DV_A]
        kq = jnp.concatenate([k, q], axis=0).astype(BF16)
        kt = _transpose_bf16(k)
        ks = jnp.zeros((nb, DV_A), F32)
        qs = jnp.zeros((nb, DV_A), F32)
        for b in range(nb):
            r = _dot(kq, gdn_ref[b, h].astype(BF16))
            ks = _row_select(rows, b, ks, r[0:nb])
            qs = _row_select(rows, b, qs, r[nb:2 * nb])
        gam_h = gam[:, h:h + 1]
        uu = beta[:, h:h + 1] * (v - gam_h * ks)
        o = gam_h * qs + jnp.sum(q * k, axis=-1, keepdims=True) * uu
        for b in range(nb):
            usel = _pad_rows16(jnp.where(rows == b, uu, 0.0))
            gdn_o[b, h] = gam[b:b + 1, h:h + 1] * gdn_ref[b, h] + _dot(kt, usel)
        z = proj_ref[:, E_Z + h * DV_A:E_Z + (h + 1) * DV_A].astype(F32)
        y_ref[:, h * DV_A:(h + 1) * DV_A] = _rms_rows(o, gnw) * _silu(z)

    for h in range(H_B):
        qk_in = proj_ref[:, E_QK + h * 128:E_QK + (h + 1) * 128]
        qb = qk_in[:, 0:DK_B]
        q = qb.astype(F32)
        k = qk_in[:, DK_B:2 * DK_B].astype(F32) * (DK_B ** -0.5)
        v = proj_ref[:, E_V + h * DV_B:E_V + (h + 1) * DV_B].astype(F32)
        kt = _transpose_bf16(k)
        q16 = _pad_rows16(qb)
        qc = jnp.zeros((nb, DV_B), F32)
        for b in range(nb):
            r = _dot(q16, mc_ref[b, h].astype(BF16))
            qc = _row_select(rows, b, qc, r[0:nb])
        s0_h = s0[:, h:h + 1]
        we_h = we[:, h:h + 1]
        wv = we_h * v
        nn = s0_h * mn_ref[h] + we_h * k
        mn_o[h] = nn
        num = s0_h * qc + jnp.sum(q * k, axis=-1, keepdims=True) * wv
        den = jnp.sum(q * nn, axis=-1, keepdims=True)
        hh = num / jnp.maximum(jnp.abs(den), floor[:, h:h + 1])
        for b in range(nb):
            wsel = _pad_rows16(jnp.where(rows == b, wv, 0.0))
            mc_o[b, h] = s0[b:b + 1, h:h + 1] * mc_ref[b, h] + _dot(kt, wsel)
        o_pre = proj_ref[:, E_OP + h * DV_B:E_OP + (h + 1) * DV_B].astype(F32)
        z = proj_ref[:, E_Z + H_A * DV_A + h * DV_B:
                     E_Z + H_A * DV_A + (h + 1) * DV_B].astype(F32)
        y_ref[:, H_A * DV_A + h * DV_B:H_A * DV_A + (h + 1) * DV_B] = (
            _sigmoid(o_pre) * _rms_rows(hh, mnw) * _silu(z))


def _even_sample(proj, small, conv, gdn, mc, mn, mm, sp, cw, gnw, mnw):
    batch = proj.shape[0]
    nb = DEC_BB
    r2 = lambda i: (i, 0)
    m3 = lambda i: (0, i, 0)
    i4 = lambda i: (i, 0, 0, 0)
    c2 = lambda i: (0, 0)
    return pl.pallas_call(
        _even_sample_kernel,
        grid=(batch // nb,),
        in_specs=[
            pl.BlockSpec((nb, P_E), r2),
            pl.BlockSpec((nb, SMALL_E), r2),
            pl.BlockSpec((CONV_W - 1, nb, CONV_CH), m3),
            pl.BlockSpec((nb, H_A, DK_A, DV_A), i4),
            pl.BlockSpec((nb, H_B, DK_B, DV_B), i4),
            pl.BlockSpec((H_B, nb, DK_B), m3),
            pl.BlockSpec((nb, LANE), r2),
            pl.BlockSpec((8, LANE), c2),
            pl.BlockSpec((CONV_W, CONV_CH), c2),
            pl.BlockSpec((1, DV_A), c2),
            pl.BlockSpec((1, DV_B), c2),
        ],
        out_specs=[
            pl.BlockSpec((nb, W_MIX), r2),
            pl.BlockSpec((CONV_W - 1, nb, CONV_CH), m3),
            pl.BlockSpec((nb, H_A, DK_A, DV_A), i4),
            pl.BlockSpec((nb, H_B, DK_B, DV_B), i4),
            pl.BlockSpec((H_B, nb, DK_B), m3),
            pl.BlockSpec((nb, LANE), r2),
        ],
        out_shape=[
            jax.ShapeDtypeStruct((batch, W_MIX), F32),
            jax.ShapeDtypeStruct((CONV_W - 1, batch, CONV_CH), F32),
            jax.ShapeDtypeStruct((batch, H_A, DK_A, DV_A), F32),
            jax.ShapeDtypeStruct((batch, H_B, DK_B, DV_B), F32),
            jax.ShapeDtypeStruct((H_B, batch, DK_B), F32),
            jax.ShapeDtypeStruct((batch, LANE), F32),
        ],
        scratch_shapes=[pltpu.VMEM((nb, CONV_CH), F32)],
        compiler_params=pltpu.CompilerParams(
            dimension_semantics=("arbitrary",), vmem_limit_bytes=VMEM_LIMIT),
        name="even_sample",
    )(proj, small, conv, gdn, mc, mn, mm, sp, cw, gnw, mnw)


def _odd_sample_kernel(proj_ref, glr_ref, w2_ref, b2_ref, gla_ref, ret_ref, cc_ref, ss_ref,
                       cnw_ref, dnw_ref, y_ref, gla_o, ret_o):
    nb = DEC_BB
    cnw = cnw_ref[...]
    dnw = dnw_ref[...]
    cc = cc_ref[...]
    ss = ss_ref[...]
    rows = lax.broadcasted_iota(jnp.int32, (nb, DV_C), 0)
    g = _log_sigmoid(_dot(glr_ref[...].astype(BF16), w2_ref[...]) + b2_ref[...]) / GLA_TAU

    for h in range(H_C):
        q = proj_ref[:, O_QC + h * DK_C:O_QC + (h + 1) * DK_C].astype(F32) * (DK_C ** -0.5)
        k = proj_ref[:, O_KC + h * DK_C:O_KC + (h + 1) * DK_C]
        v = proj_ref[:, O_VC + h * DV_C:O_VC + (h + 1) * DV_C].astype(F32)
        eg = jnp.exp(g[:, h * DK_C:(h + 1) * DK_C])
        qe = _pad_rows16(q * eg)
        kt = _transpose_bf16(k)
        egt = _transpose(eg)
        qs = jnp.zeros((nb, DV_C), F32)
        for b in range(nb):
            qs = _row_select(rows, b, qs, _dot(qe, gla_ref[b, h].astype(BF16))[0:nb])
        o = qs + jnp.sum(q * k.astype(F32), axis=-1, keepdims=True) * v
        for b in range(nb):
            vsel = _pad_rows16(jnp.where(rows == b, v, 0.0))
            gla_o[b, h] = egt[:, b:b + 1] * gla_ref[b, h] + _dot(kt, vsel)
        z = proj_ref[:, O_Z + h * DV_C:O_Z + (h + 1) * DV_C].astype(F32)
        y_ref[:, h * DV_C:(h + 1) * DV_C] = _rms_rows(o, cnw) * _silu(z)

    for h in range(H_D):
        gamma = math.exp(_ret_log_gamma(h))
        qd = proj_ref[:, O_QD + h * DK_D:O_QD + (h + 1) * DK_D].astype(F32)
        kd = proj_ref[:, O_KD + h * DK_D:O_KD + (h + 1) * DK_D].astype(F32)
        qd = qd * cc + pltpu.roll(qd, DK_D // 2, 1) * ss
        kd = (kd * cc + pltpu.roll(kd, DK_D // 2, 1) * ss) * (DK_D ** -0.5)
        v = proj_ref[:, O_VD + h * DV_D:O_VD + (h + 1) * DV_D].astype(F32)
        qb = _pad_rows16(qd)
        kt = _transpose_bf16(kd)
        qs = jnp.zeros((nb, DV_D), F32)
        for b in range(nb):
            qs = _row_select(rows, b, qs, _dot(qb, ret_ref[b, h].astype(BF16))[0:nb])
        o = gamma * qs + jnp.sum(qd * kd, axis=-1, keepdims=True) * v
        for b in range(nb):
            vsel = _pad_rows16(jnp.where(rows == b, v, 0.0))
            ret_o[b, h] = gamma * ret_ref[b, h] + _dot(kt, vsel)
        z = proj_ref[:, O_Z + H_C * DV_C + h * DV_D:
                     O_Z + H_C * DV_C + (h + 1) * DV_D].astype(F32)
        y_ref[:, H_C * DV_C + h * DV_D:H_C * DV_C + (h + 1) * DV_D] = (
            _rms_rows(o, dnw) * _silu(z))


def _odd_sample(proj, glr, w2, b2, gla, ret, cc, ss, cnw, dnw):
    batch = proj.shape[0]
    nb = DEC_BB
    r2 = lambda i: (i, 0)
    i4 = lambda i: (i, 0, 0, 0)
    c2 = lambda i: (0, 0)
    return pl.pallas_call(
        _odd_sample_kernel,
        grid=(batch // nb,),
        in_specs=[
            pl.BlockSpec((nb, P_O), r2),
            pl.BlockSpec((nb, SMALL_O), r2),
            pl.BlockSpec((LANE, H_C * DK_C), c2),
            pl.BlockSpec((1, H_C * DK_C), c2),
            pl.BlockSpec((nb, H_C, DK_C, DV_C), i4),
            pl.BlockSpec((nb, H_D, DK_D, DV_D), i4),
            pl.BlockSpec((1, DK_D), c2),
            pl.BlockSpec((1, DK_D), c2),
            pl.BlockSpec((1, DV_C), c2),
            pl.BlockSpec((1, DV_D), c2),
        ],
        out_specs=[
            pl.BlockSpec((nb, W_MIX), r2),
            pl.BlockSpec((nb, H_C, DK_C, DV_C), i4),
            pl.BlockSpec((nb, H_D, DK_D, DV_D), i4),
        ],
        out_shape=[
            jax.ShapeDtypeStruct((batch, W_MIX), F32),
            jax.ShapeDtypeStruct((batch, H_C, DK_C, DV_C), F32),
            jax.ShapeDtypeStruct((batch, H_D, DK_D, DV_D), F32),
        ],
        compiler_params=pltpu.CompilerParams(
            dimension_semantics=("arbitrary",), vmem_limit_bytes=VMEM_LIMIT),
        name="odd_sample",
    )(proj, glr, w2, b2, gla, ret, cc, ss, cnw, dnw)


def _pad_cols(w, width):
    return jnp.pad(w, ((0, 0), (0, width - w.shape[1])))


def _prep_even_w_in(w):
    u = w[:, 0:3072]
    beta = w[:, 3072:3080]
    a = w[:, 3080:3088]
    qb = w[:, 3088:3600].reshape(-1, H_B, DK_B)
    kb = w[:, 3600:4112].reshape(-1, H_B, DK_B)
    vb = w[:, 4112:5136]
    ig = w[:, 5136:5144]
    fg = w[:, 5144:5152]
    op = w[:, 5152:6176]
    z = w[:, 6176:8224]
    qk = jnp.concatenate([qb, kb], axis=-1).reshape(-1, H_B * 2 * DK_B)
    small = jnp.concatenate([_pad_cols(beta, 128), _pad_cols(a, 128), _pad_cols(ig, 128),
                             _pad_cols(fg, 128)], axis=1)
    return jnp.concatenate([u, qk, vb, op, z], axis=1).astype(BF16), small.astype(BF16)


def _prep_odd_w_in(w):
    qc = w[:, 0:512]
    kc = w[:, 512:1024]
    vc = w[:, 1024:2048]
    glr = w[:, 2048:2064]
    qd = w[:, 2064:2576]
    kd = w[:, 2576:3088]
    vd = w[:, 3088:4112]
    z = w[:, 4112:6160]
    return (jnp.concatenate([qc, kc, vc, qd, kd, vd, z], axis=1).astype(BF16),
            _pad_cols(glr, SMALL_O).astype(BF16))


def _lane_row(vec, width=LANE):
    return jnp.pad(vec.astype(F32), (0, width - vec.shape[0]))


def _rotary_tables(pos):
    half = DK_D // 2
    inv = ROPE_BASE ** (-jnp.arange(half, dtype=F32) / half)
    ang = pos.astype(F32)[:, None] * inv[None, :]
    cos = jnp.cos(ang)
    sin = jnp.sin(ang)
    return jnp.concatenate([cos, cos], axis=-1), jnp.concatenate([-sin, sin], axis=-1)


def kernel(x_prompt, x_sample, c_prompt, c_sample, state_gdn, state_gdn_conv, state_mlstm_c,
           state_mlstm_n, state_mlstm_m, state_gla, state_ret, ada_w, ada_b, norm_w, ev_w_in,
           ev_w_out, gdn_conv_w, gdn_a_log, gdn_dt_bias, gdn_norm_w, mlstm_gate_b, mlstm_norm_w,
           od_w_in, od_w_out, gla_w2, gla_b2, gla_norm_w, ret_norm_w, final_norm_w):
    bp, tp, d = x_prompt.shape
    bs = x_sample.shape[0]
    n_p = bp * tp

    w_in_e, w_sm_e = _prep_even_w_in(ev_w_in[0])
    w_in_o, w_sm_o = _prep_odd_w_in(od_w_in[0])
    w_out_e = ev_w_out[0].astype(BF16)
    w_out_o = od_w_out[0].astype(BF16)
    sp = jnp.stack([_lane_row(gdn_dt_bias[0]), _lane_row(gdn_a_log[0]),
                    _lane_row(mlstm_gate_b[0, :H_B]), _lane_row(mlstm_gate_b[0, H_B:])]
                   + [jnp.zeros((LANE,), F32)] * 4)
    gnw = gdn_norm_w[0].reshape(1, DV_A)
    mnw = mlstm_norm_w[0].reshape(1, DV_B)
    cnw = gla_norm_w[0].reshape(1, DV_C)
    dnw = ret_norm_w[0].reshape(1, DV_D)
    w2 = jnp.pad(gla_w2[0], ((0, LANE - GLA_RANK), (0, 0))).astype(BF16)
    b2 = gla_b2[0].reshape(1, H_C * DK_C)

    mod = _modulation(jnp.concatenate([c_prompt, c_sample], axis=0), ada_w, ada_b)
    mod_p = [mod[l, :bp].reshape(bp, 1, 3 * d) for l in range(2)]
    mod_s = [mod[l, bp:].reshape(1, bs, 3 * d) for l in range(2)]

    xp = x_prompt.reshape(n_p, d)
    proj, small = _inproj(xp, mod_p[0], norm_w[0], w_in_e, w_sm_e, 1024, 2048)
    y, p_gdn, p_conv, p_mc, p_mn, p_mm = _even_prompt(
        proj.reshape(bp, tp, P_E), small.reshape(bp, tp, SMALL_E), sp, gdn_conv_w[0], gnw, mnw,
        bp, tp)
    xp = _outproj(y.reshape(n_p, W_MIX), xp, mod_p[0], w_out_e, final_norm_w, 512, False)
    proj, small = _inproj(xp, mod_p[1], norm_w[1], w_in_o, w_sm_o, 1024, 2048)
    cc_p, ss_p = _rotary_tables(jnp.arange(tp))
    y, p_gla, p_ret = _odd_prompt(proj, small, w2, b2, cc_p, ss_p, cnw, dnw, bp, tp)
    y_prompt = _outproj(y, xp, mod_p[1], w_out_o, final_norm_w, 512, True).reshape(bp, tp, d)

    xs = x_sample.reshape(bs, d)
    proj, small = _inproj(xs, mod_s[0], norm_w[0], w_in_e, w_sm_e, bs, 2048)
    mm_in = jnp.pad(state_mlstm_m[0], ((0, 0), (0, LANE - H_B)))
    y, s_conv, s_gdn, s_mc, s_mn, s_mm = _even_sample(
        proj, small, jnp.transpose(state_gdn_conv[0], (1, 0, 2)), state_gdn[0], state_mlstm_c[0],
        jnp.transpose(state_mlstm_n[0], (1, 0, 2)), mm_in, sp, gdn_conv_w[0], gnw, mnw)
    xs = _outproj(y, xs, mod_s[0], w_out_e, final_norm_w, bs, False)
    proj, small = _inproj(xs, mod_s[1], norm_w[1], w_in_o, w_sm_o, bs, 2048)
    cc_s, ss_s = _rotary_tables(PAST_LEN + jnp.arange(1))
    y, s_gla, s_ret = _odd_sample(proj, small, w2, b2, state_gla[0], state_ret[0], cc_s, ss_s,
                                  cnw, dnw)
    y_sample = _outproj(y, xs, mod_s[1], w_out_o, final_norm_w, bs, True)
    y_sample = y_sample.reshape(bs, 1, d)

    return (y_prompt, y_sample,
            p_gdn[None], p_conv[None], p_mc[None], p_mn[None], p_mm[None, :, 0, :H_B],
            p_gla[None], p_ret[None],
            s_gdn[None], jnp.transpose(s_conv, (1, 0, 2))[None], s_mc[None],
            jnp.transpose(s_mn, (1, 0, 2))[None], s_mm[None, :, :H_B], s_gla[None], s_ret[None])
```
